```python
import jax
import jax.numpy as jnp
from jax import lax
import numpy as np

D_MODEL = 1024
BATCH = 4
SEQ = 4096
DEPTH = 4

GRID_W = 64
CTX_LEN = 256
HEAD_DIM = 128
ROPE_THETA = 10000.0
EPS = 1e-6
NEG_INF = -1e30

RET_HEADS = 4
RET_DK = HEAD_DIM
RET_DV = HEAD_DIM
RET_WIDTH = RET_HEADS * RET_DV
RET_CHUNK = 128

ATT_Q_HEADS = 4
ATT_KV_HEADS = 2
ATT_GROUP = ATT_Q_HEADS // ATT_KV_HEADS
ATT_WIDTH = ATT_Q_HEADS * HEAD_DIM
ATT_KV_WIDTH = ATT_KV_HEADS * HEAD_DIM
WINDOW = 128
ATT_BLOCK = 128

EVEN_SPLITS = (RET_WIDTH, 2 * RET_WIDTH, 3 * RET_WIDTH, 4 * RET_WIDTH, 4 * RET_WIDTH + ATT_WIDTH, 4 * RET_WIDTH + ATT_WIDTH + ATT_KV_WIDTH)
EVEN_IN = 4 * RET_WIDTH + ATT_WIDTH + 2 * ATT_KV_WIDTH
EVEN_OUT = RET_WIDTH + ATT_WIDTH

HG_DK = 128
HG_HEADS = D_MODEL // HG_DK
HG_DV = D_MODEL // HG_HEADS
HG_FWIDTH = HG_HEADS * HG_DK
HG_VWIDTH = HG_HEADS * HG_DV
HG_CHUNK = 16
ODD_SPLITS = (HG_FWIDTH, 2 * HG_FWIDTH, 3 * HG_FWIDTH, 3 * HG_FWIDTH + HG_VWIDTH)
ODD_IN = 3 * HG_FWIDTH + 2 * HG_VWIDTH

MOE_GROUPS = 4
MOE_EXPERTS_PER_GROUP = 8
MOE_EXPERTS = MOE_GROUPS * MOE_EXPERTS_PER_GROUP
MOE_TOP_K = 2
MOE_FF = D_MODEL // 2
MOE_BLOCK = 128

kernel_name = 'hybrid_retention_swa_hgrn2_hmoe_dit'


def rms_norm(x, w):
    xf = x.astype(jnp.float32)
    y = xf * lax.rsqrt(jnp.mean(xf * xf, axis=-1, keepdims=True) + EPS)
    return (y * w.astype(jnp.float32)).astype(x.dtype)


def head_rms(o):
    return o * lax.rsqrt(jnp.mean(o * o, axis=-1, keepdims=True) + EPS)


def flip_seq(t):
    return jnp.flip(t, axis=1)


def axial_rope_tables(n_rows):
    row = jnp.repeat(jnp.arange(n_rows, dtype=jnp.float32), GRID_W)
    col = jnp.tile(jnp.arange(GRID_W, dtype=jnp.float32), n_rows)
    axis_dim = HEAD_DIM // 2
    inv_freq = ROPE_THETA ** (-jnp.arange(0, axis_dim, 2, dtype=jnp.float32) / axis_dim)
    ang_r = row[:, None] * inv_freq[None, :]
    ang_c = col[:, None] * inv_freq[None, :]
    return (jnp.cos(ang_r), jnp.sin(ang_r), jnp.cos(ang_c), jnp.sin(ang_c))


def _rotate(x, cos, sin):
    m = cos.shape[-1]
    x1, x2 = x[..., :m], x[..., m:]
    cs = cos[None, :, None, :]
    sn = sin[None, :, None, :]
    return jnp.concatenate([x1 * cs - x2 * sn, x2 * cs + x1 * sn], axis=-1)


def apply_axial_rope(x, tables):
    cos_r, sin_r, cos_c, sin_c = tables
    xf = x.astype(jnp.float32)
    half = HEAD_DIM // 2
    out = jnp.concatenate([_rotate(xf[..., :half], cos_r, sin_r), _rotate(xf[..., half:], cos_c, sin_c)], axis=-1)
    return out.astype(x.dtype)


def retention_scan(q, k, v, log_gamma, s0):
    B, T, H, dk = q.shape
    dv = v.shape[-1]
    C = RET_CHUNK
    N = T // C
    qc = q.astype(jnp.float32).reshape(B, N, C, H, dk)
    kc = k.astype(jnp.float32).reshape(B, N, C, H, dk)
    vc = v.astype(jnp.float32).reshape(B, N, C, H, dv)
    pos = jnp.arange(C, dtype=jnp.float32)
    rel = pos[:, None] - pos[None, :]
    decay = jnp.where(rel[None] >= 0, jnp.exp(jnp.maximum(rel, 0.0)[None] * log_gamma[:, None, None]), 0.0)
    scores = jnp.einsum('bnihd,bnjhd->bnhij', qc, kc) * decay
    intra = jnp.einsum('bnhij,bnjhe->bnihe', scores, vc)
    k_decay = jnp.exp((C - 1 - pos)[:, None] * log_gamma[None, :])
    u = jnp.einsum('bnjhd,jh,bnjhe->bnhde', kc, k_decay, vc)
    chunk_decay = jnp.exp(C * log_gamma)[None, :, None, None]

    def step(s, u_n):
        return chunk_decay * s + u_n, s

    s_fin, s_prev = lax.scan(step, s0.astype(jnp.float32), jnp.moveaxis(u, 1, 0))
    q_decay = jnp.exp((pos + 1.0)[:, None] * log_gamma[None, :])
    cross = jnp.einsum('bnihd,ih,nbhde->bnihe', qc, q_decay, s_prev)
    return (intra + cross).reshape(B, T, H, dv), s_fin


def bidirectional_retention(c_qkv, l_qkv, log_gamma):
    qc, kc, vc = c_qkv
    ql, kl, vl = l_qkv
    B, _, H, dk = qc.shape
    zero = jnp.zeros((B, H, dk, vc.shape[-1]), jnp.float32)
    o_cf, s_cf = retention_scan(qc, kc, vc, log_gamma[0], zero)
    o_cb, s_cb = retention_scan(flip_seq(qc), flip_seq(kc), flip_seq(vc), log_gamma[1], zero)
    o_lf, _ = retention_scan(ql, kl, vl, log_gamma[0], s_cf)
    o_lb, _ = retention_scan(flip_seq(ql), flip_seq(kl), flip_seq(vl), log_gamma[1], s_cb)
    return o_cf + flip_seq(o_cb), o_lf + flip_seq(o_lb)


def windowed_gqa_with_sink(c_qkv, l_qkv, sink, ctx_out):
    qc, kc, vc = c_qkv
    ql, kl, vl = l_qkv
    B, L = ql.shape[:2]
    Bs = ATT_BLOCK
    nb = L // Bs
    scale = HEAD_DIM ** -0.5
    sink = sink.astype(jnp.float32).reshape(ATT_KV_HEADS, ATT_GROUP)
    kc32 = kc.astype(jnp.float32)
    vc32 = vc.astype(jnp.float32)
    n_ctx = kc.shape[1]
    qg = ql.astype(jnp.float32).reshape(B, nb, Bs, ATT_KV_HEADS, ATT_GROUP, HEAD_DIM) * scale
    pad = ((0, 0), (Bs, Bs), (0, 0), (0, 0))
    kp = jnp.pad(kl.astype(jnp.float32), pad).reshape(B, nb + 2, Bs, ATT_KV_HEADS, HEAD_DIM)
    vp = jnp.pad(vl.astype(jnp.float32), pad).reshape(B, nb + 2, Bs, ATT_KV_HEADS, HEAD_DIM)
    kw = jnp.concatenate([kp[:, :-2], kp[:, 1:-1], kp[:, 2:]], axis=2)
    vw = jnp.concatenate([vp[:, :-2], vp[:, 1:-1], vp[:, 2:]], axis=2)
    s_loc = jnp.einsum('bnqhgd,bnkhd->bnhgqk', qg, kw)
    q_pos = jnp.arange(L).reshape(nb, Bs)
    k_pos = (jnp.arange(nb)[:, None] - 1) * Bs + jnp.arange(3 * Bs)[None, :]
    valid = (k_pos[:, None, :] >= 0) & (k_pos[:, None, :] < L) & (jnp.abs(q_pos[:, :, None] - k_pos[:, None, :]) <= WINDOW)
    s_loc = jnp.where(valid[None, :, None, None], s_loc, NEG_INF)
    s_ctx = jnp.einsum('bnqhgd,bkhd->bnhgqk', qg, kc32)
    s_sink = jnp.broadcast_to(sink[None, None, :, :, None, None], s_loc.shape[:-1] + (1,))
    p = jax.nn.softmax(jnp.concatenate([s_loc, s_ctx, s_sink], axis=-1), axis=-1)
    n_loc = 3 * Bs
    o_l = (jnp.einsum('bnhgqk,bnkhd->bnqhgd', p[..., :n_loc], vw)
           + jnp.einsum('bnhgqk,bkhd->bnqhgd', p[..., n_loc:n_loc + n_ctx], vc32))
    o_l = o_l.reshape(B, L, ATT_WIDTH)
    if not ctx_out:
        return None, o_l
    qgc = qc.astype(jnp.float32).reshape(B, n_ctx, ATT_KV_HEADS, ATT_GROUP, HEAD_DIM) * scale
    s_cc = jnp.einsum('bqhgd,bkhd->bhgqk', qgc, kc32)
    s_sink_c = jnp.broadcast_to(sink[None, :, :, None, None], s_cc.shape[:-1] + (1,))
    pc = jax.nn.softmax(jnp.concatenate([s_cc, s_sink_c], axis=-1), axis=-1)
    o_c = jnp.einsum('bhgqk,bkhd->bqhgd', pc[..., :n_ctx], vc32).reshape(B, n_ctx, ATT_WIDTH)
    return o_c, o_l


def even_mixer(h_c, h_l, w_in, w_out, ret_decay_raw, att_sink, rope, ctx_out):
    B = h_l.shape[0]

    def project(h, rotate):
        T = h.shape[1]
        rq, rk, rv, rg, aq, ak, av = jnp.split(h @ w_in, EVEN_SPLITS, axis=-1)
        rq = rq.reshape(B, T, RET_HEADS, RET_DK)
        rk = rk.reshape(B, T, RET_HEADS, RET_DK) * (RET_DK ** -0.5)
        rv = rv.reshape(B, T, RET_HEADS, RET_DV)
        aq = aq.reshape(B, T, ATT_Q_HEADS, HEAD_DIM)
        ak = ak.reshape(B, T, ATT_KV_HEADS, HEAD_DIM)
        av = av.reshape(B, T, ATT_KV_HEADS, HEAD_DIM)
        if rotate:
            rq, rk, aq, ak = (apply_axial_rope(t, rope) for t in (rq, rk, aq, ak))
        return (rq, rk, rv), rg, (aq, ak, av)

    c_ret, c_gate, c_att = project(h_c, False)
    l_ret, l_gate, l_att = project(h_l, True)
    log_gamma = -jnp.exp(ret_decay_raw.astype(jnp.float32))
    o_ret_c, o_ret_l = bidirectional_retention(c_ret, l_ret, log_gamma)
    o_att_c, o_att_l = windowed_gqa_with_sink(c_att, l_att, att_sink, ctx_out)

    def merge(o_ret, gate, o_att):
        T = o_ret.shape[1]
        ret = jax.nn.silu(gate.astype(jnp.float32)) * head_rms(o_ret).reshape(B, T, RET_WIDTH)
        y = jnp.concatenate([ret, o_att], axis=-1).astype(h_l.dtype) @ w_out
        return y.astype(h_l.dtype)

    y_l = merge(o_ret_l, l_gate, o_att_l)
    y_c = merge(o_ret_c, c_gate, o_att_c) if ctx_out else None
    return y_c, y_l


def hgrn2_scan(q, k, v, log_f, s0):
    B, T, H, dk = q.shape
    dv = v.shape[-1]
    C = HG_CHUNK
    N = T // C

    def chunks(t):
        return t.reshape(B, N, C, H, t.shape[-1]).transpose(1, 0, 3, 2, 4)

    lower = jnp.tril(jnp.ones((C, C), dtype=bool))[:, :, None]

    def step(s, inp):
        qn, kn, vn, fn = inp
        a = jnp.cumsum(fn, axis=2)
        o_inter = jnp.einsum('bhcd,bhde->bhce', qn * jnp.exp(a), s)
        diff = a[:, :, :, None, :] - a[:, :, None, :, :]
        dec = jnp.exp(jnp.where(lower, diff, NEG_INF))
        scores = jnp.einsum('bhid,bhjd,bhijd->bhij', qn, kn, dec)
        o_intra = jnp.einsum('bhij,bhje->bhie', scores, vn)
        a_last = a[:, :, -1:, :]
        s_new = jnp.exp(a_last[:, :, 0, :])[..., None] * s + jnp.einsum('bhjd,bhje->bhde', kn * jnp.exp(a_last - a), vn)
        return s_new, o_inter + o_intra

    s_fin, o = lax.scan(step, s0.astype(jnp.float32), (chunks(q), chunks(k), chunks(v), chunks(log_f)))
    return o.transpose(1, 0, 3, 2, 4).reshape(B, T, H, dv), s_fin


def odd_mixer(h_c, h_l, w_in, w_out, lower_bound, norm_w, ctx_out):
    B = h_l.shape[0]
    lb = lower_bound.astype(jnp.float32).reshape(HG_HEADS, HG_DK)

    def project(h):
        T = h.shape[1]
        q, zf, zb, i, g = jnp.split(h @ w_in, ODD_SPLITS, axis=-1)
        heads = lambda t, d: t.astype(jnp.float32).reshape(B, T, HG_HEADS, d)
        f_fwd = lb + (1.0 - lb) * jax.nn.sigmoid(heads(zf, HG_DK))
        f_bwd = lb + (1.0 - lb) * jax.nn.sigmoid(heads(zb, HG_DK))
        return heads(q, HG_DK), f_fwd, f_bwd, heads(i, HG_DV), g

    def scan_dir(q, f, i, s0, reverse):
        if reverse:
            q, f, i = flip_seq(q), flip_seq(f), flip_seq(i)
        o, s = hgrn2_scan(q, 1.0 - f, i, jnp.log(f), s0)
        return (flip_seq(o) if reverse else o), s

    qc, ffc, fbc, ic, gc = project(h_c)
    ql, ffl, fbl, il, gl = project(h_l)
    zero = jnp.zeros((B, HG_HEADS, HG_DK, HG_DV), jnp.float32)
    o_cf, s_cf = scan_dir(qc, ffc, ic, zero, False)
    o_cb, s_cb = scan_dir(qc, fbc, ic, zero, True)
    o_lf, _ = scan_dir(ql, ffl, il, s_cf, False)
    o_lb, _ = scan_dir(ql, fbl, il, s_cb, True)

    def readout(o, g):
        T = o.shape[1]
        y = rms_norm(o, norm_w) * jax.nn.silu(g.astype(jnp.float32).reshape(B, T, HG_HEADS, HG_DV))
        return (y.reshape(B, T, HG_VWIDTH).astype(h_l.dtype) @ w_out).astype(h_l.dtype)

    y_l = readout(o_lf + o_lb, gl)
    y_c = readout(o_cf + o_cb, gc) if ctx_out else None
    return y_c, y_l


def hierarchical_moe(h, wg, bg, we, be, w1, w3, w2):
    N, D = h.shape
    g_prob = jax.nn.softmax((h @ wg).astype(jnp.float32) + bg.astype(jnp.float32), axis=-1)
    g_p, g_idx = lax.top_k(g_prob, 1)
    e_all = jnp.einsum('nd,gde->nge', h, we).astype(jnp.float32) + be.astype(jnp.float32)
    sel = jnp.broadcast_to(g_idx[:, :, None], (N, 1, MOE_EXPERTS_PER_GROUP))
    e_logits = jnp.take_along_axis(e_all, sel, axis=1)[:, 0]
    e_top, e_idx = lax.top_k(e_logits, MOE_TOP_K)
    gate = g_p * jax.nn.softmax(e_top, axis=-1)
    expert = g_idx * MOE_EXPERTS_PER_GROUP + e_idx
    NK = N * MOE_TOP_K
    flat_e = expert.reshape(-1)
    flat_tok = jnp.arange(NK) // MOE_TOP_K
    flat_w = gate.reshape(-1)
    order = jnp.argsort(flat_e)
    s_e = flat_e[order]
    s_tok = flat_tok[order]
    s_w = flat_w[order]
    counts = jnp.bincount(flat_e, length=MOE_EXPERTS)
    padded = (counts + MOE_BLOCK - 1) // MOE_BLOCK * MOE_BLOCK
    pad_end = jnp.cumsum(padded)
    pad_start = pad_end - padded
    seg_start = jnp.cumsum(counts) - counts
    dest = pad_start[s_e] + jnp.arange(NK) - seg_start[s_e]
    n_blocks = -(-NK // MOE_BLOCK) + MOE_EXPERTS
    cap = n_blocks * MOE_BLOCK
    slot_tok = jnp.full((cap,), N, dtype=jnp.int32).at[dest].set(s_tok.astype(jnp.int32))
    block_expert = jnp.minimum(jnp.searchsorted(pad_end, jnp.arange(n_blocks) * MOE_BLOCK, side='right'), MOE_EXPERTS - 1)
    h_pad = jnp.concatenate([h, jnp.zeros((1, D), h.dtype)], axis=0)
    xb = h_pad[slot_tok].reshape(n_blocks, MOE_BLOCK, D)

    def expert_block(args):
        xblk, e = args
        return (jax.nn.silu(xblk @ w1[e]) * (xblk @ w3[e])) @ w2[e]

    yb = lax.map(expert_block, (xb, block_expert))
    y_sorted = yb.reshape(cap, D)[dest]
    return jnp.zeros((N, D), h.dtype).at[s_tok].add((y_sorted * s_w[:, None]).astype(h.dtype))


def setup_inputs(seed: int = 0) -> dict:
    key = jax.random.key(seed)
    ks = jax.random.split(key, 24)
    f32 = jnp.float32
    n_even = (DEPTH + 1) // 2
    n_odd = DEPTH // 2

    def nrm(k, shape, scale):
        return jax.random.normal(k, shape, f32) * scale

    base_decay = jnp.log(-jnp.log1p(-(2.0 ** (-5.0 - jnp.arange(RET_HEADS, dtype=f32)))))
    return {
        'x': nrm(ks[0], (BATCH, SEQ, D_MODEL), 1.0),
        'c': nrm(ks[1], (BATCH, D_MODEL), 1.0),
        'ctx': nrm(ks[2], (BATCH, CTX_LEN, D_MODEL), 1.0),
        'c_ctx': nrm(ks[3], (D_MODEL,), 1.0),
        'ada_w': nrm(ks[4], (DEPTH, D_MODEL, 6 * D_MODEL), 0.5 * D_MODEL ** -0.5),
        'ada_b': nrm(ks[5], (DEPTH, 6 * D_MODEL), 0.02),
        'norm_w': 1.0 + nrm(ks[6], (DEPTH, 2, D_MODEL), 0.02),
        'final_norm_w': 1.0 + nrm(ks[7], (D_MODEL,), 0.02),
        'ev_w_in': nrm(ks[8], (n_even, D_MODEL, EVEN_IN), D_MODEL ** -0.5),
        'ev_w_out': nrm(ks[9], (n_even, EVEN_OUT, D_MODEL), EVEN_OUT ** -0.5),
        'ret_decay_raw': base_decay[None, None, :] + nrm(ks[10], (n_even, 2, RET_HEADS), 0.05),
        'att_sink': nrm(ks[11], (n_even, ATT_Q_HEADS), 0.5),
        'od_w_in': nrm(ks[12], (n_odd, D_MODEL, ODD_IN), D_MODEL ** -0.5),
        'od_w_out': nrm(ks[13], (n_odd, HG_VWIDTH, D_MODEL), HG_VWIDTH ** -0.5),
        'hg_lb_logits': nrm(ks[14], (DEPTH, HG_FWIDTH), 0.1),
        'hg_norm_w': 1.0 + nrm(ks[15], (n_odd, HG_DV), 0.02),
        'moe_wg': nrm(ks[16], (DEPTH, D_MODEL, MOE_GROUPS), D_MODEL ** -0.5),
        'moe_bg': nrm(ks[17], (DEPTH, MOE_GROUPS), 0.01),
        'moe_we': nrm(ks[18], (DEPTH, MOE_GROUPS, D_MODEL, MOE_EXPERTS_PER_GROUP), D_MODEL ** -0.5),
        'moe_be': nrm(ks[19], (DEPTH, MOE_GROUPS, MOE_EXPERTS_PER_GROUP), 0.01),
        'moe_w1': nrm(ks[20], (DEPTH, MOE_EXPERTS, D_MODEL, MOE_FF), D_MODEL ** -0.5),
        'moe_w3': nrm(ks[21], (DEPTH, MOE_EXPERTS, D_MODEL, MOE_FF), D_MODEL ** -0.5),
        'moe_w2': nrm(ks[22], (DEPTH, MOE_EXPERTS, MOE_FF, D_MODEL), MOE_FF ** -0.5),
    }


def reference(x, c, ctx, c_ctx, ada_w, ada_b, norm_w, final_norm_w, ev_w_in, ev_w_out, ret_decay_raw, att_sink,
              od_w_in, od_w_out, hg_lb_logits, hg_norm_w, moe_wg, moe_bg, moe_we, moe_be, moe_w1, moe_w3, moe_w2):
    B, L, D = x.shape
    n_ctx = ctx.shape[1]
    n_rows = L // GRID_W
    rope = axial_rope_tables(n_rows)
    lb_cum = jnp.cumsum(jax.nn.softmax(hg_lb_logits.astype(jnp.float32), axis=0), axis=0)
    lower_bounds = lb_cum - lb_cum[:1]
    for l in range(DEPTH):
        last = l == DEPTH - 1
        mod_l = jnp.split(jax.nn.silu(c) @ ada_w[l] + ada_b[l], 6, axis=-1)
        mod_c = jnp.split(jax.nn.silu(c_ctx) @ ada_w[l] + ada_b[l], 6, axis=-1)
        h_l = rms_norm(x, norm_w[l, 0]) * (1.0 + mod_l[1][:, None]) + mod_l[0][:, None]
        h_c = rms_norm(ctx, norm_w[l, 0]) * (1.0 + mod_c[1]) + mod_c[0]
        p = l // 2
        if l % 2 == 0:
            y_c, y_l = even_mixer(h_c, h_l, ev_w_in[p], ev_w_out[p], ret_decay_raw[p], att_sink[p], rope, not last)
        else:
            y_c, y_l = odd_mixer(h_c, h_l, od_w_in[p], od_w_out[p], lower_bounds[l], hg_norm_w[p], not last)
        x = x + mod_l[2][:, None] * y_l
        h2_l = rms_norm(x, norm_w[l, 1]) * (1.0 + mod_l[4][:, None]) + mod_l[3][:, None]
        moe_params = (moe_wg[l], moe_bg[l], moe_we[l], moe_be[l], moe_w1[l], moe_w3[l], moe_w2[l])
        if last:
            y2 = hierarchical_moe(h2_l.reshape(B * L, D), *moe_params)
            x = x + mod_l[5][:, None] * y2.reshape(B, L, D)
        else:
            ctx = ctx + mod_c[2] * y_c
            h2_c = rms_norm(ctx, norm_w[l, 1]) * (1.0 + mod_c[4]) + mod_c[3]
            tokens = jnp.concatenate([h2_c.reshape(B * n_ctx, D), h2_l.reshape(B * L, D)], axis=0)
            y2 = hierarchical_moe(tokens, *moe_params)
            ctx = ctx + mod_c[5] * y2[:B * n_ctx].reshape(B, n_ctx, D)
            x = x + mod_l[5][:, None] * y2[B * n_ctx:].reshape(B, L, D)
    return rms_norm(x, final_norm_w)
```

```python
import functools

import numpy as np
import jax
import jax.numpy as jnp
from jax import lax
from jax.experimental import pallas as pl
from jax.experimental.pallas import tpu as pltpu

F32 = jnp.float32
BF16 = jnp.bfloat16
HIGHEST = lax.Precision.HIGHEST

EPS = 1e-6
NEG_INF = -1e30
LANES = 128
GRID_W = 64
ROPE_THETA = 10000.0
HEAD_DIM = 128
RET_HEADS = 4
ATT_Q_HEADS = 4
ATT_KV_HEADS = 2
ATT_GROUP = ATT_Q_HEADS // ATT_KV_HEADS
WINDOW = 128
HG_HEADS = 8
MOE_GROUPS = 4
MOE_EPG = 8
MOE_EXPERTS = MOE_GROUPS * MOE_EPG
MOE_TOP_K = 2
MOE_BLOCK = 128
CHUNK = 128
ROW_TILE = 256
MOD_ROWS = 8
VMEM_LIMIT = 56 * 1024 * 1024


def _cparams(n_axes, **kw):
    return pltpu.CompilerParams(dimension_semantics=("arbitrary",) * n_axes,
                                vmem_limit_bytes=VMEM_LIMIT, **kw)


def _dot(a, b):
    return jnp.dot(a, b, preferred_element_type=F32)


def _dot_nt(a, b):
    return lax.dot_general(a, b, (((1,), (1,)), ((), ())), preferred_element_type=F32)


def _dot_tn(a, b):
    return lax.dot_general(a, b, (((0,), (0,)), ((), ())), preferred_element_type=F32)


def _silu(x):
    return x * jax.nn.sigmoid(x)


def _norm_mod(x, nw, scale, shift):
    ms = jnp.mean(x * x, axis=-1, keepdims=True)
    y = x * lax.rsqrt(ms + EPS) * nw
    return y * (1.0 + scale) + shift


def _ada_kernel(c_ref, w_ref, b_ref, o_ref):
    s = _silu(c_ref[...])
    o_ref[0] = jnp.dot(s, w_ref[0], precision=HIGHEST, preferred_element_type=F32) + b_ref[0]


def _ada_modulation(cvec, ada_w, ada_b):
    depth, d, n6 = ada_w.shape
    tn = 1536
    return pl.pallas_call(
        _ada_kernel,
        out_shape=jax.ShapeDtypeStruct((depth, MOD_ROWS, n6), F32),
        grid=(depth, n6 // tn),
        in_specs=[pl.BlockSpec((MOD_ROWS, d), lambda l, j: (0, 0)),
                  pl.BlockSpec((1, d, tn), lambda l, j: (l, 0, j)),
                  pl.BlockSpec((1, 1, tn), lambda l, j: (l, 0, j))],
        out_specs=pl.BlockSpec((1, MOD_ROWS, tn), lambda l, j: (l, 0, j)),
        compiler_params=_cparams(2),
    )(cvec, ada_w, ada_b.reshape(depth, 1, n6))


def _swap_halves(x):
    lane = lax.broadcasted_iota(jnp.int32, x.shape, 1)
    return jnp.where((lane % 64) < 32, pltpu.roll(x, 96, 1), pltpu.roll(x, 32, 1))


def _proj_kernel(x_ref, nw_ref, shift_ref, scale_ref, w_ref, *rest, col_ops, chunk):
    if col_ops is None:
        (o_ref,) = rest
    else:
        cos_ref, sin_ref, o_ref = rest
    h = _norm_mod(x_ref[...], nw_ref[...], scale_ref[0], shift_ref[0]).astype(BF16)
    nout = o_ref.shape[1]
    for c0 in range(0, nout, chunk):
        acc = _dot(h, w_ref[:, c0:c0 + chunk])
        for hd in range(chunk // LANES):
            col = c0 + hd * LANES
            a = acc[:, hd * LANES:(hd + 1) * LANES]
            if col_ops is not None:
                rope, pre, post = col_ops[col // LANES]
                if pre != 1.0:
                    a = a * pre
                if rope:
                    a = a * cos_ref[...] + _swap_halves(a) * sin_ref[...]
                if post != 1.0:
                    a = a * post
            o_ref[:, col:col + LANES] = a


def _tile_mod_row(i, tiles_per_batch, ctx_tiles, batch):
    return jnp.where(i % tiles_per_batch < ctx_tiles, batch, i // tiles_per_batch)


def _mod_spec(chunk_idx, d, tiles_per_batch, ctx_tiles, batch):
    return pl.BlockSpec(
        (1, 1, d), lambda i: (_tile_mod_row(i, tiles_per_batch, ctx_tiles, batch), 0, chunk_idx))


def _in_projection(xs, nw, mods, w_bf, dims, col_ops=None, rope=None):
    nt, d = xs.shape
    nout = w_bf.shape[1]
    batch, n_ctx, seq = dims
    tpb = (n_ctx + seq) // ROW_TILE
    ctx_tiles = n_ctx // ROW_TILE
    in_specs = [pl.BlockSpec((ROW_TILE, d), lambda i: (i, 0)),
                pl.BlockSpec((1, d), lambda i: (0, 0)),
                _mod_spec(0, d, tpb, ctx_tiles, batch),
                _mod_spec(1, d, tpb, ctx_tiles, batch),
                pl.BlockSpec((d, nout), lambda i: (0, 0))]
    args = [xs, nw.reshape(1, d), mods, mods, w_bf]
    if col_ops is not None:
        rope_spec = pl.BlockSpec((ROW_TILE, LANES), lambda i: (i % tpb, 0))
        in_specs += [rope_spec, rope_spec]
        args += [rope[0], rope[1]]
    return pl.pallas_call(
        functools.partial(_proj_kernel, col_ops=col_ops, chunk=512),
        out_shape=jax.ShapeDtypeStruct((nt, nout), F32),
        grid=(nt // ROW_TILE,),
        in_specs=in_specs,
        out_specs=pl.BlockSpec((ROW_TILE, nout), lambda i: (i, 0)),
        compiler_params=_cparams(1),
    )(*args)


def _rope_tables(n_ctx, seq):
    n_rows = seq // GRID_W
    row = jnp.repeat(jnp.arange(n_rows, dtype=F32), GRID_W)
    col = jnp.tile(jnp.arange(GRID_W, dtype=F32), n_rows)
    axis_dim = HEAD_DIM // 2
    inv_freq = ROPE_THETA ** (-jnp.arange(0, axis_dim, 2, dtype=F32) / axis_dim)
    ang_r = row[:, None] * inv_freq[None, :]
    ang_c = col[:, None] * inv_freq[None, :]
    cos = jnp.concatenate([jnp.cos(ang_r), jnp.cos(ang_r), jnp.cos(ang_c), jnp.cos(ang_c)], axis=-1)
    sin = jnp.concatenate([-jnp.sin(ang_r), jnp.sin(ang_r), -jnp.sin(ang_c), jnp.sin(ang_c)], axis=-1)
    cos = jnp.concatenate([jnp.ones((n_ctx, HEAD_DIM), F32), cos], axis=0)
    sin = jnp.concatenate([jnp.zeros((n_ctx, HEAD_DIM), F32), sin], axis=0)
    return cos, sin


def _backward_chunk(t, ctx_chunks, n_chunks):
    return jnp.where(t < ctx_chunks, ctx_chunks - 1 - t, n_chunks - 1 - (t - ctx_chunks))


def _ret_kernel(raw_ref, q_ref, k_ref, v_ref, g_ref, o_ref, of_scr, s_scr, dec_scr, qd_scr, kd_scr,
                *, ctx_chunks, n_chunks):
    h = pl.program_id(1)
    c = CHUNK
    ii = lax.broadcasted_iota(jnp.int32, (c, c), 0).astype(F32)
    jj = lax.broadcasted_iota(jnp.int32, (c, c), 1).astype(F32)
    lg = []
    for d in range(2):
        lgd = -jnp.exp(jnp.full((c, c), raw_ref[d, h], F32))
        lg.append(lgd)
        rel = (ii - jj) if d == 0 else (jj - ii)
        dec_scr[d] = jnp.where(rel >= 0, jnp.exp(jnp.maximum(rel, 0.0) * lgd), 0.0)
        qd_scr[d] = jnp.exp(((ii + 1.0) if d == 0 else (c - ii)) * lgd)
        kd_scr[d] = jnp.exp(((c - 1.0 - ii) if d == 0 else ii) * lgd)

    def chunk(n, d):
        rows = pl.ds(pl.multiple_of(n * c, c), c)
        q = q_ref[rows, :]
        k = k_ref[rows, :]
        vb = v_ref[rows, :].astype(BF16)
        sc = _dot_nt(q.astype(BF16), k.astype(BF16)) * dec_scr[d]
        s = s_scr[...]
        o = _dot(sc.astype(BF16), vb) + _dot((q * qd_scr[d]).astype(BF16), s.astype(BF16))
        chunk_decay = jnp.exp(float(c) * lg[d][0:1, :])
        s_scr[...] = chunk_decay * s + _dot_tn((k * kd_scr[d]).astype(BF16), vb)
        return rows, o

    s_scr[...] = jnp.zeros_like(s_scr)

    def fwd(n, carry):
        rows, o = chunk(n, 0)
        of_scr[rows, :] = o
        return carry

    lax.fori_loop(0, n_chunks, fwd, 0)
    s_scr[...] = jnp.zeros_like(s_scr)

    def bwd(t, carry):
        rows, o = chunk(_backward_chunk(t, ctx_chunks, n_chunks), 1)
        o = o + of_scr[rows, :]
        y = o * lax.rsqrt(jnp.mean(o * o, axis=-1, keepdims=True) + EPS)
        o_ref[rows, :] = (_silu(g_ref[rows, :]) * y).astype(BF16)
        return carry

    lax.fori_loop(0, n_chunks, bwd, 0)


def _retention(proj, ret_decay_raw, dims):
    batch, n_ctx, seq = dims
    tb = n_ctx + seq
    spec = lambda off: pl.BlockSpec((tb, LANES), lambda b, h: (b, off + h))
    return pl.pallas_call(
        functools.partial(_ret_kernel, ctx_chunks=n_ctx // CHUNK, n_chunks=tb // CHUNK),
        out_shape=jax.ShapeDtypeStruct((batch * tb, RET_HEADS * LANES), BF16),
        grid=(batch, RET_HEADS),
        in_specs=[pl.BlockSpec(memory_space=pltpu.SMEM),
                  spec(0), spec(RET_HEADS), spec(2 * RET_HEADS), spec(3 * RET_HEADS)],
        out_specs=pl.BlockSpec((tb, LANES), lambda b, h: (b, h)),
        scratch_shapes=[pltpu.VMEM((tb, LANES), F32), pltpu.VMEM((LANES, LANES), F32),
                        pltpu.VMEM((2, CHUNK, CHUNK), F32), pltpu.VMEM((2, CHUNK, LANES), F32),
                        pltpu.VMEM((2, CHUNK, LANES), F32)],
        compiler_params=_cparams(2),
    )(ret_decay_raw, proj, proj, proj, proj)


def _att_kernel(sink_ref, q0_ref, q1_ref, k_ref, v_ref, o_ref, kb_scr, vb_scr, *, n_ctx, seq):
    hk = pl.program_id(1)
    c = CHUNK
    win = 3 * c
    kb_scr[...] = k_ref[...].astype(BF16)
    vb_scr[...] = v_ref[...].astype(BF16)
    kc = kb_scr[0:n_ctx, :]
    vc = vb_scr[0:n_ctx, :]
    ii = lax.broadcasted_iota(jnp.int32, (c, win), 0)
    jj = lax.broadcasted_iota(jnp.int32, (c, win), 1)

    def softmax_out(s_ctx, sink, s_loc=None, v_loc=None):
        m = jnp.maximum(jnp.max(s_ctx, axis=-1, keepdims=True), sink)
        if s_loc is not None:
            m = jnp.maximum(m, jnp.max(s_loc, axis=-1, keepdims=True))
        p_ctx = jnp.exp(s_ctx - m)
        den = jnp.sum(p_ctx, axis=-1, keepdims=True) + jnp.exp(sink - m)
        o = _dot(p_ctx.astype(BF16), vc)
        if s_loc is not None:
            p_loc = jnp.exp(s_loc - m)
            den = den + jnp.sum(p_loc, axis=-1, keepdims=True)
            o = o + _dot(p_loc.astype(BF16), v_loc)
        return o / den

    for g, q_ref in enumerate((q0_ref, q1_ref)):
        sink = jnp.full((c, 1), sink_ref[hk * ATT_GROUP + g], F32)
        cols = slice(g * LANES, (g + 1) * LANES)

        for cc in range(n_ctx // c):
            qb = q_ref[cc * c:(cc + 1) * c, :].astype(BF16)
            o_ref[cc * c:(cc + 1) * c, cols] = softmax_out(_dot_nt(qb, kc), sink).astype(BF16)

        def block(n, carry):
            q_rows = pl.ds(pl.multiple_of(n_ctx + n * c, c), c)
            qb = q_ref[q_rows, :].astype(BF16)
            start = jnp.clip(n * c - c, 0, seq - win)
            k_rows = pl.ds(pl.multiple_of(n_ctx + start, c), win)
            rel = ii - jj + (n * c - start)
            s_loc = jnp.where(jnp.abs(rel) <= WINDOW, _dot_nt(qb, kb_scr[k_rows, :]), NEG_INF)
            o = softmax_out(_dot_nt(qb, kc), sink, s_loc, vb_scr[k_rows, :])
            o_ref[q_rows, cols] = o.astype(BF16)
            return carry

        lax.fori_loop(0, seq // c, block, 0)


def _attention(proj, att_sink, dims, col0):
    batch, n_ctx, seq = dims
    tb = n_ctx + seq
    qb, kb, vb = col0, col0 + ATT_Q_HEADS, col0 + ATT_Q_HEADS + ATT_KV_HEADS
    return pl.pallas_call(
        functools.partial(_att_kernel, n_ctx=n_ctx, seq=seq),
        out_shape=jax.ShapeDtypeStruct((batch * tb, ATT_Q_HEADS * LANES), BF16),
        grid=(batch, ATT_KV_HEADS),
        in_specs=[pl.BlockSpec(memory_space=pltpu.SMEM),
                  pl.BlockSpec((tb, LANES), lambda b, h: (b, qb + ATT_GROUP * h)),
                  pl.BlockSpec((tb, LANES), lambda b, h: (b, qb + ATT_GROUP * h + 1)),
                  pl.BlockSpec((tb, LANES), lambda b, h: (b, kb + h)),
                  pl.BlockSpec((tb, LANES), lambda b, h: (b, vb + h))],
        out_specs=pl.BlockSpec((tb, ATT_GROUP * LANES), lambda b, h: (b, h)),
        scratch_shapes=[pltpu.VMEM((tb, LANES), BF16), pltpu.VMEM((tb, LANES), BF16)],
        compiler_params=_cparams(2),
    )(att_sink, proj, proj, proj, proj)


HG_LEVELS = 7


def _hgrn_constants():
    c = CHUNK
    i = np.arange(c)[:, None]
    r = np.arange(c)[None, :]
    mats = [r <= i]
    upper_m, lower_m = [], []
    half = 1
    while half < c:
        beta = (i // (2 * half)) * (2 * half) + half - 1
        upper = (i % (2 * half)) >= half
        upper_m.append(upper & (r > beta) & (r <= i))
        lower_m.append((~upper) & (r > i) & (r <= beta))
        half *= 2
    fwd = [m.astype(np.float32) for m in mats + upper_m + lower_m]
    bwd = [m[::-1, ::-1] for m in fwd]
    stack = np.stack([np.concatenate(fwd, 0), np.concatenate(bwd, 0)])
    x = i ^ r
    level = np.where(r > i, -1, np.where(r == i, 0, np.floor(np.log2(np.maximum(x, 1))) + 1)).astype(np.int32)
    levels = np.stack([level, level[::-1, ::-1]])
    return stack, levels


def _hgrn_kernel(q_ref, zf_ref, zb_ref, v_ref, g_ref, lbl_ref, nw_ref, m_ref, lv_ref, o_ref,
                 of_scr, st_scr, *, layer, ctx_chunks, n_chunks):
    c = CHUNK
    z = lbl_ref[...]
    e = jnp.exp(z - jnp.max(z, axis=0, keepdims=True))
    sm = e / jnp.sum(e, axis=0, keepdims=True)
    lb = jnp.sum(sm[1:layer + 1], axis=0, keepdims=True)

    def chunk(n, d, z_ref):
        rows = pl.ds(pl.multiple_of(n * c, c), c)
        q = q_ref[rows, :]
        v = v_ref[rows, :]
        f = lb + (1.0 - lb) * jax.nn.sigmoid(z_ref[rows, :])
        g = jnp.log(f)
        k = 1.0 - f
        g_hi = g.astype(BF16)
        g_lo = (g - g_hi.astype(F32)).astype(BF16)
        res = _dot(m_ref[d], jnp.concatenate([g_hi, g_lo], axis=1))
        ex = res[:, :c] + res[:, c:]
        a = ex[0:c]
        st = st_scr[...]
        o = _dot_nt((q * jnp.exp(a)).astype(BF16), st.astype(BF16))
        lv = lv_ref[d]
        sc = jnp.where(lv == 0, _dot_nt(q.astype(BF16), k.astype(BF16)), 0.0)
        for m in range(1, HG_LEVELS + 1):
            qe = (q * jnp.exp(ex[m * c:(m + 1) * c])).astype(BF16)
            ke = (k * jnp.exp(ex[(HG_LEVELS + m) * c:(HG_LEVELS + m + 1) * c])).astype(BF16)
            sc = jnp.where(lv == m, _dot_nt(qe, ke), sc)
        vb = v.astype(BF16)
        o = o + _dot(sc.astype(BF16), vb)
        a_last = a[c - 1:c] if d == 0 else a[0:1]
        kd = (k * jnp.exp(a_last - a)).astype(BF16)
        st_scr[...] = st * jnp.exp(a_last) + _dot_tn(vb, kd)
        return rows, o

    st_scr[...] = jnp.zeros_like(st_scr)

    def fwd(n, carry):
        rows, o = chunk(n, 0, zf_ref)
        of_scr[rows, :] = o
        return carry

    lax.fori_loop(0, n_chunks, fwd, 0)
    st_scr[...] = jnp.zeros_like(st_scr)

    def bwd(t, carry):
        rows, o = chunk(_backward_chunk(t, ctx_chunks, n_chunks), 1, zb_ref)
        o = o + of_scr[rows, :]
        y = o * lax.rsqrt(jnp.mean(o * o, axis=-1, keepdims=True) + EPS) * nw_ref[...]
        o_ref[rows, :] = (y * _silu(g_ref[rows, :])).astype(BF16)
        return carry

    lax.fori_loop(0, n_chunks, bwd, 0)


def _hgrn2(proj, hg_lb_logits, hg_norm_w, layer, dims):
    batch, n_ctx, seq = dims
    tb = n_ctx + seq
    depth = hg_lb_logits.shape[0]
    stack, levels = _hgrn_constants()
    spec = lambda off: pl.BlockSpec((tb, LANES), lambda b, h: (b, off + h))
    n_mats = 1 + 2 * HG_LEVELS
    return pl.pallas_call(
        functools.partial(_hgrn_kernel, layer=layer, ctx_chunks=n_ctx // CHUNK, n_chunks=tb // CHUNK),
        out_shape=jax.ShapeDtypeStruct((batch * tb, HG_HEADS * LANES), BF16),
        grid=(batch, HG_HEADS),
        in_specs=[spec(0), spec(HG_HEADS), spec(2 * HG_HEADS), spec(3 * HG_HEADS), spec(4 * HG_HEADS),
                  pl.BlockSpec((depth, LANES), lambda b, h: (0, h)),
                  pl.BlockSpec((1, LANES), lambda b, h: (0, 0)),
                  pl.BlockSpec((2, n_mats * CHUNK, CHUNK), lambda b, h: (0, 0, 0)),
                  pl.BlockSpec((2, CHUNK, CHUNK), lambda b, h: (0, 0, 0))],
        out_specs=pl.BlockSpec((tb, LANES), lambda b, h: (b, h)),
        scratch_shapes=[pltpu.VMEM((tb, LANES), F32), pltpu.VMEM((LANES, LANES), F32)],
        compiler_params=_cparams(2),
    )(proj, proj, proj, proj, proj, hg_lb_logits, hg_norm_w.reshape(1, LANES),
      jnp.asarray(stack, BF16), jnp.asarray(levels, jnp.int32))


def _route(logits):
    lane = lax.broadcasted_iota(jnp.int32, logits.shape, 1).astype(F32)
    big = float(LANES)
    first = lambda hit: jnp.min(jnp.where(hit, lane, big), axis=-1, keepdims=True)
    is_g = lane < MOE_GROUPS
    gl = jnp.where(is_g, logits, NEG_INF)
    gmax = jnp.max(gl, axis=-1, keepdims=True)
    gsum = jnp.sum(jnp.where(is_g, jnp.exp(gl - gmax), 0.0), axis=-1, keepdims=True)
    g_p = 1.0 / gsum
    lo = MOE_GROUPS + first(gl == gmax) * MOE_EPG
    el = jnp.where((lane >= lo) & (lane < lo + MOE_EPG), logits, NEG_INF)
    e1v = jnp.max(el, axis=-1, keepdims=True)
    e1 = first(el == e1v)
    el2 = jnp.where(lane == e1, NEG_INF, el)
    e2v = jnp.max(el2, axis=-1, keepdims=True)
    e2 = first(el2 == e2v)
    t = jnp.exp(e2v - e1v)
    w1 = 1.0 / (1.0 + t)
    w2 = t * w1
    return jnp.where(lane == 0, e1 - MOE_GROUPS,
                     jnp.where(lane == 1, e2 - MOE_GROUPS,
                               jnp.where(lane == 2, g_p * w1, jnp.where(lane == 3, g_p * w2, 0.0))))


def _out_route_kernel(*refs, n_parts):
    a_refs = refs[:n_parts]
    (w_ref, x_ref, gate_ref, nw_ref, shift_ref, scale_ref, wr_ref, br_ref,
     xo_ref, h2_ref, route_ref) = refs[n_parts:]
    k0 = 0
    y = None
    for a_ref in a_refs:
        kk = a_ref.shape[1]
        t = _dot(a_ref[...], w_ref[k0:k0 + kk, :])
        y = t if y is None else y + t
        k0 += kk
    x = x_ref[...] + gate_ref[0] * y
    xo_ref[...] = x
    h2 = _norm_mod(x, nw_ref[...], scale_ref[0], shift_ref[0])
    h2_ref[...] = h2
    logits = jnp.dot(h2, wr_ref[...], precision=HIGHEST, preferred_element_type=F32) + br_ref[...]
    route_ref[...] = _route(logits)


def _out_projection_route(parts, w_bf, xs, nw, mods, wr, br, dims):
    nt, d = xs.shape
    batch, n_ctx, seq = dims
    tpb = (n_ctx + seq) // ROW_TILE
    ctx_tiles = n_ctx // ROW_TILE
    row = lambda width: pl.BlockSpec((ROW_TILE, width), lambda i: (i, 0))
    full = lambda shape: pl.BlockSpec(shape, lambda i: (0,) * len(shape))
    mod = lambda chunk_idx: _mod_spec(chunk_idx, d, tpb, ctx_tiles, batch)
    return pl.pallas_call(
        functools.partial(_out_route_kernel, n_parts=len(parts)),
        out_shape=(jax.ShapeDtypeStruct((nt, d), F32), jax.ShapeDtypeStruct((nt, d), F32),
                   jax.ShapeDtypeStruct((nt, LANES), F32)),
        grid=(nt // ROW_TILE,),
        in_specs=[row(p.shape[1]) for p in parts] + [
            full(w_bf.shape), row(d), mod(2), full((1, d)), mod(3), mod(4), full(wr.shape), full(br.shape)],
        out_specs=(row(d), row(d), row(LANES)),
        compiler_params=_cparams(1),
    )(*parts, w_bf, xs, mods, nw.reshape(1, d), mods, mods, wr, br)


def _dispatch_plan(route, n_tok):
    n_assign = n_tok * MOE_TOP_K
    flat_e = route[:, :MOE_TOP_K].astype(jnp.int32).reshape(-1)
    onehot = (flat_e[:, None] == jnp.arange(MOE_EXPERTS, dtype=jnp.int32)[None, :]).astype(jnp.int32)
    csum = jnp.cumsum(onehot, axis=0)
    rank = jnp.take_along_axis(csum, flat_e[:, None], axis=1)[:, 0] - 1
    counts = csum[-1]
    padded = (counts + MOE_BLOCK - 1) // MOE_BLOCK * MOE_BLOCK
    pad_end = jnp.cumsum(padded)
    pad_start = pad_end - padded
    dest = pad_start[flat_e] + rank
    n_blocks = -(-n_assign // MOE_BLOCK) + MOE_EXPERTS
    slot_assign = jnp.full((n_blocks * MOE_BLOCK,), -1, jnp.int32).at[dest].set(
        jnp.arange(n_assign, dtype=jnp.int32))
    block_expert = jnp.minimum(
        jnp.searchsorted(pad_end, jnp.arange(n_blocks, dtype=jnp.int32) * MOE_BLOCK, side='right'),
        MOE_EXPERTS - 1).astype(jnp.int32)
    n_used = (pad_end[-1] // MOE_BLOCK).astype(jnp.int32).reshape(1)
    return block_expert, n_used, slot_assign, n_blocks


def _expert_kernel(be_ref, nused_ref, slot_ref, h_hbm, w1_ref, w3_ref, w2_ref, out_hbm,
                   xbuf, ybuf, w1b, w3b, w2b, gsem, ssem):
    i = pl.program_id(0)
    n_used = nused_ref[0]
    cur = i % 2

    def gather_rows(blk, buf, wait):
        def body(r, carry):
            a = slot_ref[blk * MOE_BLOCK + r]

            @pl.when(a >= 0)
            def _():
                cp = pltpu.make_async_copy(h_hbm.at[pl.ds(a // MOE_TOP_K, 1)], xbuf.at[buf, pl.ds(r, 1)],
                                           gsem.at[buf])
                cp.wait() if wait else cp.start()
            return carry
        lax.fori_loop(0, MOE_BLOCK, body, 0)

    def scatter_rows(blk, buf, wait):
        def body(r, carry):
            a = slot_ref[blk * MOE_BLOCK + r]

            @pl.when(a >= 0)
            def _():
                cp = pltpu.make_async_copy(ybuf.at[buf, pl.ds(r, 1)], out_hbm.at[pl.ds(a, 1)], ssem.at[buf])
                cp.wait() if wait else cp.start()
            return carry
        lax.fori_loop(0, MOE_BLOCK, body, 0)

    @pl.when(i < n_used)
    def _():
        @pl.when(i == 0)
        def _():
            xbuf[...] = jnp.zeros_like(xbuf)
            gather_rows(0, 0, wait=False)

        @pl.when(i + 1 < n_used)
        def _():
            gather_rows(i + 1, 1 - cur, wait=False)

        gather_rows(i, cur, wait=True)

        @pl.when((i == 0) | (be_ref[i] != be_ref[jnp.maximum(i - 1, 0)]))
        def _():
            w1b[...] = w1_ref[0].astype(BF16)
            w3b[...] = w3_ref[0].astype(BF16)
            w2b[...] = w2_ref[0].astype(BF16)

        @pl.when(i >= 2)
        def _():
            scatter_rows(i - 2, cur, wait=True)

        xb = xbuf[cur].astype(BF16)
        act = _silu(_dot(xb, w1b[...])) * _dot(xb, w3b[...])
        ybuf[cur] = _dot(act.astype(BF16), w2b[...])
        scatter_rows(i, cur, wait=False)

        @pl.when(i == n_used - 1)
        def _():
            @pl.when(i >= 1)
            def _():
                scatter_rows(i - 1, 1 - cur, wait=True)
            scatter_rows(i, cur, wait=True)


def _experts(h2, plan, w1, w3, w2):
    block_expert, n_used, slot_assign, n_blocks = plan
    nt, d = h2.shape
    ff = w1.shape[2]
    wspec = lambda shape: pl.BlockSpec((1,) + shape, lambda i, be, nu, sl: (be[i], 0, 0))
    grid_spec = pltpu.PrefetchScalarGridSpec(
        num_scalar_prefetch=3,
        grid=(n_blocks,),
        in_specs=[pl.BlockSpec(memory_space=pl.ANY), wspec((d, ff)), wspec((d, ff)), wspec((ff, d))],
        out_specs=pl.BlockSpec(memory_space=pl.ANY),
        scratch_shapes=[pltpu.VMEM((2, MOE_BLOCK, d), F32), pltpu.VMEM((2, MOE_BLOCK, d), F32),
                        pltpu.VMEM((d, ff), BF16), pltpu.VMEM((d, ff), BF16), pltpu.VMEM((ff, d), BF16),
                        pltpu.SemaphoreType.DMA((2,)), pltpu.SemaphoreType.DMA((2,))])
    return pl.pallas_call(
        _expert_kernel,
        out_shape=jax.ShapeDtypeStruct((nt * MOE_TOP_K, d), F32),
        grid_spec=grid_spec,
        compiler_params=_cparams(1, has_side_effects=True),
    )(block_expert, n_used, slot_assign, h2, w1, w3, w2)


def _combine_kernel(x_ref, y_ref, route_ref, gate_ref, *rest, final):
    d = x_ref.shape[1]
    r = route_ref[...]
    y = r[:, 2:3] * y_ref[:, 0:d] + r[:, 3:4] * y_ref[:, d:2 * d]
    x = x_ref[...] + gate_ref[0] * y
    if final:
        fnw_ref, o_ref = rest
        x = x * lax.rsqrt(jnp.mean(x * x, axis=-1, keepdims=True) + EPS) * fnw_ref[...]
    else:
        (o_ref,) = rest
    o_ref[...] = x


def _combine(xs, y2, route, mods, dims, final_norm_w=None):
    nt, d = xs.shape
    batch, n_ctx, seq = dims
    tpb = (n_ctx + seq) // ROW_TILE
    ctx_tiles = n_ctx // ROW_TILE
    row = lambda width: pl.BlockSpec((ROW_TILE, width), lambda i: (i, 0))
    final = final_norm_w is not None
    in_specs = [row(d), row(MOE_TOP_K * d), row(LANES), _mod_spec(5, d, tpb, ctx_tiles, batch)]
    args = [xs, y2.reshape(nt, MOE_TOP_K * d), route, mods]
    if final:
        in_specs.append(pl.BlockSpec((1, d), lambda i: (0, 0)))
        args.append(final_norm_w.reshape(1, d))
    return pl.pallas_call(
        functools.partial(_combine_kernel, final=final),
        out_shape=jax.ShapeDtypeStruct((nt, d), F32),
        grid=(nt // ROW_TILE,),
        in_specs=in_specs,
        out_specs=row(d),
        compiler_params=_cparams(1),
    )(*args)


def _even_col_ops():
    rh = RET_HEADS
    ops = [(True, 1.0, 1.0)] * rh + [(True, float(HEAD_DIM) ** -0.5, 1.0)] * rh + [(False, 1.0, 1.0)] * (2 * rh)
    ops += [(True, 1.0, float(HEAD_DIM) ** -0.5)] * ATT_Q_HEADS + [(True, 1.0, 1.0)] * ATT_KV_HEADS
    ops += [(False, 1.0, 1.0)] * ATT_KV_HEADS
    return tuple(ops)


def kernel(x, c, ctx, c_ctx, ada_w, ada_b, norm_w, final_norm_w, ev_w_in, ev_w_out, ret_decay_raw, att_sink,
           od_w_in, od_w_out, hg_lb_logits, hg_norm_w, moe_wg, moe_bg, moe_we, moe_be, moe_w1, moe_w3, moe_w2):
    batch, seq, d = x.shape
    n_ctx = ctx.shape[1]
    depth = ada_w.shape[0]
    assert batch + 1 <= MOD_ROWS and n_ctx % ROW_TILE == 0 and seq % ROW_TILE == 0
    assert seq % GRID_W == 0 and seq >= 3 * CHUNK
    dims = (batch, n_ctx, seq)
    tb = n_ctx + seq
    nt = batch * tb

    xs = jnp.concatenate([ctx, x], axis=1).reshape(nt, d)
    cvec = jnp.concatenate([c, c_ctx[None, :], jnp.zeros((MOD_ROWS - batch - 1, d), F32)], axis=0)
    mods_all = _ada_modulation(cvec, ada_w, ada_b)
    rope = _rope_tables(n_ctx, seq)
    even_ops = _even_col_ops()

    for l in range(depth):
        p = l // 2
        mods = mods_all[l].reshape(MOD_ROWS, 1, 6 * d)
        if l % 2 == 0:
            proj = _in_projection(xs, norm_w[l, 0], mods, ev_w_in[p].astype(BF16), dims, even_ops, rope)
            ret = _retention(proj, ret_decay_raw[p], dims)
            att = _attention(proj, att_sink[p], dims, 4 * RET_HEADS)
            parts, w_out = [ret, att], ev_w_out[p]
        else:
            proj = _in_projection(xs, norm_w[l, 0], mods, od_w_in[p].astype(BF16), dims)
            parts, w_out = [_hgrn2(proj, hg_lb_logits, hg_norm_w[p], l, dims)], od_w_out[p]
        n_logit = MOE_GROUPS + MOE_EXPERTS
        wr = jnp.concatenate([moe_wg[l], jnp.moveaxis(moe_we[l], 0, 1).reshape(d, MOE_EXPERTS),
                              jnp.zeros((d, LANES - n_logit), F32)], axis=1)
        br = jnp.concatenate([moe_bg[l], moe_be[l].reshape(-1), jnp.zeros((LANES - n_logit,), F32)])[None, :]
        xs, h2, route = _out_projection_route(parts, w_out.astype(BF16), xs, norm_w[l, 1], mods, wr, br, dims)
        plan = _dispatch_plan(route, nt)
        y2 = _experts(h2, plan, moe_w1[l], moe_w3[l], moe_w2[l])
        xs = _combine(xs, y2, route, mods, dims, final_norm_w if l == depth - 1 else None)

    return xs.reshape(batch, tb, d)[:, n_ctx:]
```

```python
import functools

import numpy as np
import jax
import jax.numpy as jnp
from jax import lax
from jax.experimental import pallas as pl
from jax.experimental.pallas import tpu as pltpu

F32 = jnp.float32
BF16 = jnp.bfloat16
HIGHEST = lax.Precision.HIGHEST

EPS = 1e-6
NEG_INF = -1e30
LANES = 128
GRID_W = 64
ROPE_THETA = 10000.0
HEAD_DIM = 128
RET_HEADS = 4
ATT_Q_HEADS = 4
ATT_KV_HEADS = 2
ATT_GROUP = ATT_Q_HEADS // ATT_KV_HEADS
WINDOW = 128
HG_HEADS = 8
MOE_GROUPS = 4
MOE_EPG = 8
MOE_EXPERTS = MOE_GROUPS * MOE_EPG
MOE_TOP_K = 2
MOE_BLOCK = 128
CHUNK = 128
ROW_TILE = 256
MOD_ROWS = 8
VMEM_LIMIT = 56 * 1024 * 1024


def _cparams(n_axes, **kw):
    return pltpu.CompilerParams(dimension_semantics=("arbitrary",) * n_axes,
                                vmem_limit_bytes=VMEM_LIMIT, **kw)


def _dot(a, b):
    return jnp.dot(a, b, preferred_element_type=F32)


def _dot_nt(a, b):
    return lax.dot_general(a, b, (((1,), (1,)), ((), ())), preferred_element_type=F32)


def _dot_tn(a, b):
    return lax.dot_general(a, b, (((0,), (0,)), ((), ())), preferred_element_type=F32)


def _silu(x):
    return x * jax.nn.sigmoid(x)


def _norm_mod(x, nw, scale, shift):
    ms = jnp.mean(x * x, axis=-1, keepdims=True)
    y = x * lax.rsqrt(ms + EPS) * nw
    return y * (1.0 + scale) + shift


def _ada_kernel(c_ref, w_ref, b_ref, o_ref):
    s = _silu(c_ref[...])
    o_ref[0] = jnp.dot(s, w_ref[0], precision=HIGHEST, preferred_element_type=F32) + b_ref[0]


def _ada_modulation(cvec, ada_w, ada_b):
    depth, d, n6 = ada_w.shape
    tn = 1536
    return pl.pallas_call(
        _ada_kernel,
        out_shape=jax.ShapeDtypeStruct((depth, MOD_ROWS, n6), F32),
        grid=(depth, n6 // tn),
        in_specs=[pl.BlockSpec((MOD_ROWS, d), lambda l, j: (0, 0)),
                  pl.BlockSpec((1, d, tn), lambda l, j: (l, 0, j)),
                  pl.BlockSpec((1, 1, tn), lambda l, j: (l, 0, j))],
        out_specs=pl.BlockSpec((1, MOD_ROWS, tn), lambda l, j: (l, 0, j)),
        compiler_params=_cparams(2),
    )(cvec, ada_w, ada_b.reshape(depth, 1, n6))


def _swap_halves(x):
    lane = lax.broadcasted_iota(jnp.int32, x.shape, 1)
    return jnp.where((lane % 64) < 32, pltpu.roll(x, 96, 1), pltpu.roll(x, 32, 1))


def _proj_kernel(x_ref, nw_ref, shift_ref, scale_ref, w_ref, *rest, col_ops, chunk):
    if col_ops is None:
        (o_ref,) = rest
    else:
        cos_ref, sin_ref, o_ref = rest
    h = _norm_mod(x_ref[...], nw_ref[...], scale_ref[0], shift_ref[0]).astype(BF16)
    nout = o_ref.shape[1]
    for c0 in range(0, nout, chunk):
        acc = _dot(h, w_ref[:, c0:c0 + chunk])
        for hd in range(chunk // LANES):
            col = c0 + hd * LANES
            a = acc[:, hd * LANES:(hd + 1) * LANES]
            if col_ops is not None:
                rope, pre, post = col_ops[col // LANES]
                if pre != 1.0:
                    a = a * pre
                if rope:
                    a = a * cos_ref[...] + _swap_halves(a) * sin_ref[...]
                if post != 1.0:
                    a = a * post
            o_ref[:, col:col + LANES] = a


def _tile_mod_row(i, tiles_per_batch, ctx_tiles, batch):
    return jnp.where(i % tiles_per_batch < ctx_tiles, batch, i // tiles_per_batch)


def _mod_spec(chunk_idx, d, tiles_per_batch, ctx_tiles, batch):
    return pl.BlockSpec(
        (1, 1, d), lambda i: (_tile_mod_row(i, tiles_per_batch, ctx_tiles, batch), 0, chunk_idx))


def _in_projection(xs, nw, mods, w_bf, dims, col_ops=None, rope=None):
    nt, d = xs.shape
    nout = w_bf.shape[1]
    batch, n_ctx, seq = dims
    tpb = (n_ctx + seq) // ROW_TILE
    ctx_tiles = n_ctx // ROW_TILE
    in_specs = [pl.BlockSpec((ROW_TILE, d), lambda i: (i, 0)),
                pl.BlockSpec((1, d), lambda i: (0, 0)),
                _mod_spec(0, d, tpb, ctx_tiles, batch),
                _mod_spec(1, d, tpb, ctx_tiles, batch),
                pl.BlockSpec((d, nout), lambda i: (0, 0))]
    args = [xs, nw.reshape(1, d), mods, mods, w_bf]
    if col_ops is not None:
        rope_spec = pl.BlockSpec((ROW_TILE, LANES), lambda i: (i % tpb, 0))
        in_specs += [rope_spec, rope_spec]
        args += [rope[0], rope[1]]
    return pl.pallas_call(
        functools.partial(_proj_kernel, col_ops=col_ops, chunk=512),
        out_shape=jax.ShapeDtypeStruct((nt, nout), F32),
        grid=(nt // ROW_TILE,),
        in_specs=in_specs,
        out_specs=pl.BlockSpec((ROW_TILE, nout), lambda i: (i, 0)),
        compiler_params=_cparams(1),
    )(*args)


def _rope_tables(n_ctx, seq):
    n_rows = seq // GRID_W
    row = jnp.repeat(jnp.arange(n_rows, dtype=F32), GRID_W)
    col = jnp.tile(jnp.arange(GRID_W, dtype=F32), n_rows)
    axis_dim = HEAD_DIM // 2
    inv_freq = ROPE_THETA ** (-jnp.arange(0, axis_dim, 2, dtype=F32) / axis_dim)
    ang_r = row[:, None] * inv_freq[None, :]
    ang_c = col[:, None] * inv_freq[None, :]
    cos = jnp.concatenate([jnp.cos(ang_r), jnp.cos(ang_r), jnp.cos(ang_c), jnp.cos(ang_c)], axis=-1)
    sin = jnp.concatenate([-jnp.sin(ang_r), jnp.sin(ang_r), -jnp.sin(ang_c), jnp.sin(ang_c)], axis=-1)
    cos = jnp.concatenate([jnp.ones((n_ctx, HEAD_DIM), F32), cos], axis=0)
    sin = jnp.concatenate([jnp.zeros((n_ctx, HEAD_DIM), F32), sin], axis=0)
    return cos, sin


def _backward_chunk(t, ctx_chunks, n_chunks):
    return jnp.where(t < ctx_chunks, ctx_chunks - 1 - t, n_chunks - 1 - (t - ctx_chunks))


def _ret_kernel(raw_ref, q_ref, k_ref, v_ref, g_ref, o_ref, of_scr, ob_scr, s_scr, dec_scr, qd_scr, kd_scr,
                *, ctx_chunks, n_chunks):
    h = pl.program_id(1)
    c = CHUNK
    ii = lax.broadcasted_iota(jnp.int32, (c, c), 0).astype(F32)
    jj = lax.broadcasted_iota(jnp.int32, (c, c), 1).astype(F32)
    lg = []
    for d in range(2):
        lgd = -jnp.exp(jnp.full((c, c), raw_ref[d, h], F32))
        lg.append(lgd)
        rel = (ii - jj) if d == 0 else (jj - ii)
        dec_scr[d] = jnp.where(rel >= 0, jnp.exp(jnp.maximum(rel, 0.0) * lgd), 0.0)
        qd_scr[d] = jnp.exp(((ii + 1.0) if d == 0 else (c - ii)) * lgd)
        kd_scr[d] = jnp.exp(((c - 1.0 - ii) if d == 0 else ii) * lgd)

    def chunk(n, d):
        rows = pl.ds(pl.multiple_of(n * c, c), c)
        q = q_ref[rows, :]
        k = k_ref[rows, :]
        vb = v_ref[rows, :].astype(BF16)
        sc = _dot_nt(q.astype(BF16), k.astype(BF16)) * dec_scr[d]
        s = s_scr[d]
        o = _dot(sc.astype(BF16), vb) + _dot((q * qd_scr[d]).astype(BF16), s.astype(BF16))
        chunk_decay = jnp.exp(float(c) * lg[d][0:1, :])
        s_scr[d] = chunk_decay * s + _dot_tn((k * kd_scr[d]).astype(BF16), vb)
        return rows, o

    s_scr[...] = jnp.zeros_like(s_scr)

    def scan(t, carry):
        rows, o = chunk(t, 0)
        of_scr[rows, :] = o
        rows, o = chunk(_backward_chunk(t, ctx_chunks, n_chunks), 1)
        ob_scr[rows, :] = o
        return carry

    lax.fori_loop(0, n_chunks, scan, 0)

    def readout(n, carry):
        rows = pl.ds(pl.multiple_of(n * c, c), c)
        o = of_scr[rows, :] + ob_scr[rows, :]
        y = o * lax.rsqrt(jnp.mean(o * o, axis=-1, keepdims=True) + EPS)
        o_ref[rows, :] = (_silu(g_ref[rows, :]) * y).astype(BF16)
        return carry

    lax.fori_loop(0, n_chunks, readout, 0)


def _retention(proj, ret_decay_raw, dims):
    batch, n_ctx, seq = dims
    tb = n_ctx + seq
    spec = lambda off: pl.BlockSpec((tb, LANES), lambda b, h: (b, off + h))
    return pl.pallas_call(
        functools.partial(_ret_kernel, ctx_chunks=n_ctx // CHUNK, n_chunks=tb // CHUNK),
        out_shape=jax.ShapeDtypeStruct((batch * tb, RET_HEADS * LANES), BF16),
        grid=(batch, RET_HEADS),
        in_specs=[pl.BlockSpec(memory_space=pltpu.SMEM),
                  spec(0), spec(RET_HEADS), spec(2 * RET_HEADS), spec(3 * RET_HEADS)],
        out_specs=pl.BlockSpec((tb, LANES), lambda b, h: (b, h)),
        scratch_shapes=[pltpu.VMEM((tb, LANES), F32), pltpu.VMEM((tb, LANES), F32),
                        pltpu.VMEM((2, LANES, LANES), F32),
                        pltpu.VMEM((2, CHUNK, CHUNK), F32), pltpu.VMEM((2, CHUNK, LANES), F32),
                        pltpu.VMEM((2, CHUNK, LANES), F32)],
        compiler_params=_cparams(2),
    )(ret_decay_raw, proj, proj, proj, proj)


def _att_kernel(sink_ref, q0_ref, q1_ref, k_ref, v_ref, o_ref, kb_scr, vb_scr, *, n_ctx, seq):
    hk = pl.program_id(1)
    c = CHUNK
    win = 3 * c
    kb_scr[...] = k_ref[...].astype(BF16)
    vb_scr[...] = v_ref[...].astype(BF16)
    kc = kb_scr[0:n_ctx, :]
    vc = vb_scr[0:n_ctx, :]
    ii = lax.broadcasted_iota(jnp.int32, (c, win), 0)
    jj = lax.broadcasted_iota(jnp.int32, (c, win), 1)

    def softmax_out(s_ctx, sink, s_loc=None, v_loc=None):
        m = jnp.maximum(jnp.max(s_ctx, axis=-1, keepdims=True), sink)
        if s_loc is not None:
            m = jnp.maximum(m, jnp.max(s_loc, axis=-1, keepdims=True))
        p_ctx = jnp.exp(s_ctx - m)
        den = jnp.sum(p_ctx, axis=-1, keepdims=True) + jnp.exp(sink - m)
        o = _dot(p_ctx.astype(BF16), vc)
        if s_loc is not None:
            p_loc = jnp.exp(s_loc - m)
            den = den + jnp.sum(p_loc, axis=-1, keepdims=True)
            o = o + _dot(p_loc.astype(BF16), v_loc)
        return o / den

    for g, q_ref in enumerate((q0_ref, q1_ref)):
        sink = jnp.full((c, 1), sink_ref[hk * ATT_GROUP + g], F32)
        cols = slice(g * LANES, (g + 1) * LANES)

        for cc in range(n_ctx // c):
            qb = q_ref[cc * c:(cc + 1) * c, :].astype(BF16)
            o_ref[cc * c:(cc + 1) * c, cols] = softmax_out(_dot_nt(qb, kc), sink).astype(BF16)

        def block(n, carry):
            q_rows = pl.ds(pl.multiple_of(n_ctx + n * c, c), c)
            qb = q_ref[q_rows, :].astype(BF16)
            start = jnp.clip(n * c - c, 0, seq - win)
            k_rows = pl.ds(pl.multiple_of(n_ctx + start, c), win)
            rel = ii - jj + (n * c - start)
            s_loc = jnp.where(jnp.abs(rel) <= WINDOW, _dot_nt(qb, kb_scr[k_rows, :]), NEG_INF)
            o = softmax_out(_dot_nt(qb, kc), sink, s_loc, vb_scr[k_rows, :])
            o_ref[q_rows, cols] = o.astype(BF16)
            return carry

        lax.fori_loop(0, seq // c, block, 0)


def _attention(proj, att_sink, dims, col0):
    batch, n_ctx, seq = dims
    tb = n_ctx + seq
    qb, kb, vb = col0, col0 + ATT_Q_HEADS, col0 + ATT_Q_HEADS + ATT_KV_HEADS
    return pl.pallas_call(
        functools.partial(_att_kernel, n_ctx=n_ctx, seq=seq),
        out_shape=jax.ShapeDtypeStruct((batch * tb, ATT_Q_HEADS * LANES), BF16),
        grid=(batch, ATT_KV_HEADS),
        in_specs=[pl.BlockSpec(memory_space=pltpu.SMEM),
                  pl.BlockSpec((tb, LANES), lambda b, h: (b, qb + ATT_GROUP * h)),
                  pl.BlockSpec((tb, LANES), lambda b, h: (b, qb + ATT_GROUP * h + 1)),
                  pl.BlockSpec((tb, LANES), lambda b, h: (b, kb + h)),
                  pl.BlockSpec((tb, LANES), lambda b, h: (b, vb + h))],
        out_specs=pl.BlockSpec((tb, ATT_GROUP * LANES), lambda b, h: (b, h)),
        scratch_shapes=[pltpu.VMEM((tb, LANES), BF16), pltpu.VMEM((tb, LANES), BF16)],
        compiler_params=_cparams(2),
    )(att_sink, proj, proj, proj, proj)


HG_LEVELS = 7


def _hgrn_constants():
    c = CHUNK
    i = np.arange(c)[:, None]
    r = np.arange(c)[None, :]
    mats = [r <= i]
    half = 1
    while half < c:
        beta = (i // (2 * half)) * (2 * half) + half - 1
        upper = (i % (2 * half)) >= half
        mats.append((upper & (r > beta) & (r <= i)) | ((~upper) & (r > i) & (r <= beta)))
        half *= 2
    fwd = [m.astype(np.float32) for m in mats]
    bwd = [m[::-1, ::-1] for m in fwd]
    stack = np.stack([np.concatenate(fwd, 0), np.concatenate(bwd, 0)])
    x = i ^ r
    level = np.where(r > i, -1, np.where(r == i, 0, np.floor(np.log2(np.maximum(x, 1))) + 1)).astype(np.int32)
    levels = np.stack([level, level[::-1, ::-1]])
    return stack, levels


def _hgrn_kernel(q_ref, zf_ref, zb_ref, v_ref, g_ref, lbl_ref, nw_ref, m_ref, lv_ref, o_ref,
                 of_scr, ob_scr, st_scr, *, layer, ctx_chunks, n_chunks):
    c = CHUNK
    z = lbl_ref[...]
    e = jnp.exp(z - jnp.max(z, axis=0, keepdims=True))
    sm = e / jnp.sum(e, axis=0, keepdims=True)
    lb = jnp.sum(sm[1:layer + 1], axis=0, keepdims=True)
    row = lax.broadcasted_iota(jnp.int32, (c, LANES), 0)
    upper = [[((row if d == 0 else c - 1 - row) & (1 << (m - 1))) != 0 for m in range(1, HG_LEVELS + 1)]
             for d in range(2)]

    def chunk(n, d, z_ref):
        rows = pl.ds(pl.multiple_of(n * c, c), c)
        q = q_ref[rows, :]
        v = v_ref[rows, :]
        f = lb + (1.0 - lb) * jax.nn.sigmoid(z_ref[rows, :])
        g = jnp.log(f)
        k = 1.0 - f
        g_hi = g.astype(BF16)
        g_lo = (g - g_hi.astype(F32)).astype(BF16)
        res = _dot(m_ref[d], jnp.concatenate([g_hi, g_lo], axis=1))
        ex = res[:, :c] + res[:, c:]
        a = ex[0:c]
        st = st_scr[d]
        o = _dot_nt((q * jnp.exp(a)).astype(BF16), st.astype(BF16))
        lv = lv_ref[d]
        sc = jnp.where(lv == 0, _dot_nt(q.astype(BF16), k.astype(BF16)), 0.0)
        for m in range(1, HG_LEVELS + 1):
            zz = (jnp.where(upper[d][m - 1], q, k) * jnp.exp(ex[m * c:(m + 1) * c])).astype(BF16)
            sc = jnp.where(lv == m, _dot_nt(zz, zz), sc)
        vb = v.astype(BF16)
        o = o + _dot(sc.astype(BF16), vb)
        a_last = a[c - 1:c] if d == 0 else a[0:1]
        kd = (k * jnp.exp(a_last - a)).astype(BF16)
        st_scr[d] = st * jnp.exp(a_last) + _dot_tn(vb, kd)
        return rows, o

    st_scr[...] = jnp.zeros_like(st_scr)

    def scan(t, carry):
        rows, o = chunk(t, 0, zf_ref)
        of_scr[rows, :] = o
        rows, o = chunk(_backward_chunk(t, ctx_chunks, n_chunks), 1, zb_ref)
        ob_scr[rows, :] = o
        return carry

    lax.fori_loop(0, n_chunks, scan, 0)

    def readout(n, carry):
        rows = pl.ds(pl.multiple_of(n * c, c), c)
        o = of_scr[rows, :] + ob_scr[rows, :]
        y = o * lax.rsqrt(jnp.mean(o * o, axis=-1, keepdims=True) + EPS) * nw_ref[...]
        o_ref[rows, :] = (y * _silu(g_ref[rows, :])).astype(BF16)
        return carry

    lax.fori_loop(0, n_chunks, readout, 0)


def _hgrn2(proj, hg_lb_logits, hg_norm_w, layer, dims):
    batch, n_ctx, seq = dims
    tb = n_ctx + seq
    depth = hg_lb_logits.shape[0]
    stack, levels = _hgrn_constants()
    spec = lambda off: pl.BlockSpec((tb, LANES), lambda b, h: (b, off + h))
    n_mats = 1 + HG_LEVELS
    return pl.pallas_call(
        functools.partial(_hgrn_kernel, layer=layer, ctx_chunks=n_ctx // CHUNK, n_chunks=tb // CHUNK),
        out_shape=jax.ShapeDtypeStruct((batch * tb, HG_HEADS * LANES), BF16),
        grid=(batch, HG_HEADS),
        in_specs=[spec(0), spec(HG_HEADS), spec(2 * HG_HEADS), spec(3 * HG_HEADS), spec(4 * HG_HEADS),
                  pl.BlockSpec((depth, LANES), lambda b, h: (0, h)),
                  pl.BlockSpec((1, LANES), lambda b, h: (0, 0)),
                  pl.BlockSpec((2, n_mats * CHUNK, CHUNK), lambda b, h: (0, 0, 0)),
                  pl.BlockSpec((2, CHUNK, CHUNK), lambda b, h: (0, 0, 0))],
        out_specs=pl.BlockSpec((tb, LANES), lambda b, h: (b, h)),
        scratch_shapes=[pltpu.VMEM((tb, LANES), F32), pltpu.VMEM((tb, LANES), F32),
                        pltpu.VMEM((2, LANES, LANES), F32)],
        compiler_params=_cparams(2),
    )(proj, proj, proj, proj, proj, hg_lb_logits, hg_norm_w.reshape(1, LANES),
      jnp.asarray(stack, BF16), jnp.asarray(levels, jnp.int32))


def _route(logits, tri, counts):
    lane = lax.broadcasted_iota(jnp.int32, logits.shape, 1).astype(F32)
    big = float(LANES)
    first = lambda hit: jnp.min(jnp.where(hit, lane, big), axis=-1, keepdims=True)
    is_g = lane < MOE_GROUPS
    gl = jnp.where(is_g, logits, NEG_INF)
    gmax = jnp.max(gl, axis=-1, keepdims=True)
    gsum = jnp.sum(jnp.where(is_g, jnp.exp(gl - gmax), 0.0), axis=-1, keepdims=True)
    g_p = 1.0 / gsum
    lo = MOE_GROUPS + first(gl == gmax) * MOE_EPG
    el = jnp.where((lane >= lo) & (lane < lo + MOE_EPG), logits, NEG_INF)
    e1v = jnp.max(el, axis=-1, keepdims=True)
    e1 = first(el == e1v)
    el2 = jnp.where(lane == e1, NEG_INF, el)
    e2v = jnp.max(el2, axis=-1, keepdims=True)
    e2 = first(el2 == e2v)
    t = jnp.exp(e2v - e1v)
    w1 = 1.0 / (1.0 + t)
    w2 = t * w1
    id1 = e1 - MOE_GROUPS
    id2 = e2 - MOE_GROUPS
    onehot = jnp.where((lane == id1) | (lane == id2), 1.0, 0.0)
    before = counts + _dot(tri, onehot.astype(BF16))
    rank1 = jnp.sum(jnp.where(lane == id1, before, 0.0), axis=-1, keepdims=True)
    rank2 = jnp.sum(jnp.where(lane == id2, before, 0.0), axis=-1, keepdims=True)
    slab = jnp.zeros_like(logits)
    for col, val in enumerate((id1, id2, g_p * w1, g_p * w2, rank1, rank2)):
        slab = jnp.where(lane == col, val, slab)
    return slab, counts + jnp.sum(onehot, axis=0, keepdims=True)


def _out_route_kernel(*refs, n_parts):
    a_refs = refs[:n_parts]
    (w_ref, x_ref, gate_ref, nw_ref, shift_ref, scale_ref, wr_ref, br_ref, tri_ref,
     xo_ref, h2_ref, route_ref, cnt_ref) = refs[n_parts:]

    @pl.when(pl.program_id(0) == 0)
    def _():
        cnt_ref[...] = jnp.zeros_like(cnt_ref)

    k0 = 0
    y = None
    for a_ref in a_refs:
        kk = a_ref.shape[1]
        t = _dot(a_ref[...], w_ref[k0:k0 + kk, :])
        y = t if y is None else y + t
        k0 += kk
    x = x_ref[...] + gate_ref[0] * y
    xo_ref[...] = x
    h2 = _norm_mod(x, nw_ref[...], scale_ref[0], shift_ref[0])
    h2_ref[...] = h2
    logits = jnp.dot(h2, wr_ref[...], precision=HIGHEST, preferred_element_type=F32) + br_ref[...]
    slab, counts = _route(logits, tri_ref[...], cnt_ref[0:1, :])
    route_ref[...] = slab
    cnt_ref[...] = jnp.broadcast_to(counts, cnt_ref.shape)


def _out_projection_route(parts, w_bf, xs, nw, mods, wr, br, dims):
    nt, d = xs.shape
    batch, n_ctx, seq = dims
    tpb = (n_ctx + seq) // ROW_TILE
    ctx_tiles = n_ctx // ROW_TILE
    row = lambda width: pl.BlockSpec((ROW_TILE, width), lambda i: (i, 0))
    full = lambda shape: pl.BlockSpec(shape, lambda i: (0,) * len(shape))
    mod = lambda chunk_idx: _mod_spec(chunk_idx, d, tpb, ctx_tiles, batch)
    tri = jnp.asarray(np.tril(np.ones((ROW_TILE, ROW_TILE), np.float32), -1), BF16)
    return pl.pallas_call(
        functools.partial(_out_route_kernel, n_parts=len(parts)),
        out_shape=(jax.ShapeDtypeStruct((nt, d), F32), jax.ShapeDtypeStruct((nt, d), F32),
                   jax.ShapeDtypeStruct((nt, LANES), F32), jax.ShapeDtypeStruct((8, LANES), F32)),
        grid=(nt // ROW_TILE,),
        in_specs=[row(p.shape[1]) for p in parts] + [
            full(w_bf.shape), row(d), mod(2), full((1, d)), mod(3), mod(4), full(wr.shape), full(br.shape),
            full(tri.shape)],
        out_specs=(row(d), row(d), row(LANES), full((8, LANES))),
        compiler_params=_cparams(1),
    )(*parts, w_bf, xs, mods, nw.reshape(1, d), mods, mods, wr, br, tri)


def _dispatch_plan(route, counts_f, n_tok):
    n_assign = n_tok * MOE_TOP_K
    flat_e = route[:, 0:MOE_TOP_K].astype(jnp.int32).reshape(-1)
    rank = route[:, 4:4 + MOE_TOP_K].astype(jnp.int32).reshape(-1)
    counts = counts_f[0, :MOE_EXPERTS].astype(jnp.int32)
    padded = (counts + MOE_BLOCK - 1) // MOE_BLOCK * MOE_BLOCK
    pad_end = jnp.cumsum(padded)
    pad_start = pad_end - padded
    dest = pad_start[flat_e] + rank
    n_blocks = -(-n_assign // MOE_BLOCK) + MOE_EXPERTS
    n_slots = n_blocks * MOE_BLOCK
    slot_assign = jnp.full((n_slots,), -1, jnp.int32).at[dest].set(jnp.arange(n_assign, dtype=jnp.int32))
    block_expert = jnp.minimum(
        jnp.searchsorted(pad_end, jnp.arange(n_blocks, dtype=jnp.int32) * MOE_BLOCK, side='right'),
        MOE_EXPERTS - 1).astype(jnp.int32)
    slot = jnp.arange(-MOE_BLOCK, n_slots + MOE_BLOCK, dtype=jnp.int32)
    spare = n_assign + ((slot // MOE_BLOCK) % 2) * MOE_BLOCK + slot % MOE_BLOCK
    pad = jnp.full((MOE_BLOCK,), -1, jnp.int32)
    assign = jnp.concatenate([pad, slot_assign, pad])
    src = jnp.where(assign >= 0, assign // MOE_TOP_K, 0)
    dst = jnp.where(assign >= 0, (assign % MOE_TOP_K) * n_tok + assign // MOE_TOP_K, spare)
    return block_expert, src[MOE_BLOCK:], dst[:n_slots + MOE_BLOCK], n_blocks


def _expert_kernel(be_ref, src_ref, dst_ref, h_hbm, w1_ref, w3_ref, w2_ref, out_hbm,
                   xbuf, ybuf, w1b, w3b, w2b, gsem, ssem):
    i = pl.program_id(0)
    last = pl.num_programs(0) - 1
    cur = i % 2
    nxt = 1 - cur

    def start_gather(blk, buf):
        for r in range(MOE_BLOCK):
            pltpu.make_async_copy(h_hbm.at[pl.ds(src_ref[blk * MOE_BLOCK + r], 1)],
                                  xbuf.at[buf, pl.ds(r, 1)], gsem.at[buf]).start()

    def start_scatter(blk, buf):
        for r in range(MOE_BLOCK):
            pltpu.make_async_copy(ybuf.at[buf, pl.ds(r, 1)],
                                  out_hbm.at[pl.ds(dst_ref[(blk + 1) * MOE_BLOCK + r], 1)], ssem.at[buf]).start()

    def wait_gather(buf):
        pltpu.make_async_copy(h_hbm.at[pl.ds(0, MOE_BLOCK)], xbuf.at[buf], gsem.at[buf]).wait()

    def wait_scatter(buf):
        pltpu.make_async_copy(ybuf.at[buf], out_hbm.at[pl.ds(0, MOE_BLOCK)], ssem.at[buf]).wait()

    @pl.when(i == 0)
    def _():
        ybuf[...] = jnp.zeros_like(ybuf)
        start_gather(0, 0)

    @pl.when((i == 0) | (be_ref[i] != be_ref[jnp.maximum(i - 1, 0)]))
    def _():
        w1b[...] = w1_ref[0].astype(BF16)
        w3b[...] = w3_ref[0].astype(BF16)
        w2b[...] = w2_ref[0].astype(BF16)

    wait_gather(cur)

    @pl.when(i >= 1)
    def _():
        wait_scatter(cur)

    xb = xbuf[cur].astype(BF16)
    act = _silu(_dot(xb, w1b[...])) * _dot(xb, w3b[...])
    ybuf[cur] = _dot(act.astype(BF16), w2b[...])
    start_gather(i + 1, nxt)
    start_scatter(i - 1, nxt)

    @pl.when(i == last)
    def _():
        start_scatter(i, cur)
        wait_scatter(nxt)
        wait_scatter(cur)
        wait_gather(nxt)


def _experts(h2, plan, w1, w3, w2):
    block_expert, slot_src, slot_dst, n_blocks = plan
    nt, d = h2.shape
    ff = w1.shape[2]
    wspec = lambda shape: pl.BlockSpec((1,) + shape, lambda i, be, sr, ds: (be[i], 0, 0))
    grid_spec = pltpu.PrefetchScalarGridSpec(
        num_scalar_prefetch=3,
        grid=(n_blocks,),
        in_specs=[pl.BlockSpec(memory_space=pl.ANY), wspec((d, ff)), wspec((d, ff)), wspec((ff, d))],
        out_specs=pl.BlockSpec(memory_space=pl.ANY),
        scratch_shapes=[pltpu.VMEM((2, MOE_BLOCK, d), F32), pltpu.VMEM((2, MOE_BLOCK, d), F32),
                        pltpu.VMEM((d, ff), BF16), pltpu.VMEM((d, ff), BF16), pltpu.VMEM((ff, d), BF16),
                        pltpu.SemaphoreType.DMA((2,)), pltpu.SemaphoreType.DMA((2,))])
    return pl.pallas_call(
        _expert_kernel,
        out_shape=jax.ShapeDtypeStruct((nt * MOE_TOP_K + 2 * MOE_BLOCK, d), F32),
        grid_spec=grid_spec,
        compiler_params=_cparams(1, has_side_effects=True, disable_bounds_checks=True),
    )(block_expert, slot_src, slot_dst, h2, w1, w3, w2)


def _combine_kernel(x_ref, y0_ref, y1_ref, route_ref, gate_ref, *rest, final):
    r = route_ref[...]
    y = r[:, 2:3] * y0_ref[...] + r[:, 3:4] * y1_ref[...]
    x = x_ref[...] + gate_ref[0] * y
    if final:
        fnw_ref, o_ref = rest
        x = x * lax.rsqrt(jnp.mean(x * x, axis=-1, keepdims=True) + EPS) * fnw_ref[...]
    else:
        (o_ref,) = rest
    o_ref[...] = x


def _combine(xs, y2, route, mods, dims, final_norm_w=None):
    nt, d = xs.shape
    batch, n_ctx, seq = dims
    tpb = (n_ctx + seq) // ROW_TILE
    ctx_tiles = n_ctx // ROW_TILE
    row = lambda width: pl.BlockSpec((ROW_TILE, width), lambda i: (i, 0))
    final = final_norm_w is not None
    second = pl.BlockSpec((ROW_TILE, d), lambda i: (i + nt // ROW_TILE, 0))
    in_specs = [row(d), row(d), second, row(LANES), _mod_spec(5, d, tpb, ctx_tiles, batch)]
    args = [xs, y2, y2, route, mods]
    if final:
        in_specs.append(pl.BlockSpec((1, d), lambda i: (0, 0)))
        args.append(final_norm_w.reshape(1, d))
    return pl.pallas_call(
        functools.partial(_combine_kernel, final=final),
        out_shape=jax.ShapeDtypeStruct((nt, d), F32),
        grid=(nt // ROW_TILE,),
        in_specs=in_specs,
        out_specs=row(d),
        compiler_params=_cparams(1),
    )(*args)


def _even_col_ops():
    rh = RET_HEADS
    ops = [(True, 1.0, 1.0)] * rh + [(True, float(HEAD_DIM) ** -0.5, 1.0)] * rh + [(False, 1.0, 1.0)] * (2 * rh)
    ops += [(True, 1.0, float(HEAD_DIM) ** -0.5)] * ATT_Q_HEADS + [(True, 1.0, 1.0)] * ATT_KV_HEADS
    ops += [(False, 1.0, 1.0)] * ATT_KV_HEADS
    return tuple(ops)


def kernel(x, c, ctx, c_ctx, ada_w, ada_b, norm_w, final_norm_w, ev_w_in, ev_w_out, ret_decay_raw, att_sink,
           od_w_in, od_w_out, hg_lb_logits, hg_norm_w, moe_wg, moe_bg, moe_we, moe_be, moe_w1, moe_w3, moe_w2):
    batch, seq, d = x.shape
    n_ctx = ctx.shape[1]
    depth = ada_w.shape[0]
    assert batch + 1 <= MOD_ROWS and n_ctx % ROW_TILE == 0 and seq % ROW_TILE == 0
    assert seq % GRID_W == 0 and seq >= 3 * CHUNK
    dims = (batch, n_ctx, seq)
    tb = n_ctx + seq
    nt = batch * tb

    xs = jnp.concatenate([ctx, x], axis=1).reshape(nt, d)
    cvec = jnp.concatenate([c, c_ctx[None, :], jnp.zeros((MOD_ROWS - batch - 1, d), F32)], axis=0)
    mods_all = _ada_modulation(cvec, ada_w, ada_b)
    rope = _rope_tables(n_ctx, seq)
    even_ops = _even_col_ops()

    for l in range(depth):
        p = l // 2
        mods = mods_all[l].reshape(MOD_ROWS, 1, 6 * d)
        if l % 2 == 0:
            proj = _in_projection(xs, norm_w[l, 0], mods, ev_w_in[p].astype(BF16), dims, even_ops, rope)
            ret = _retention(proj, ret_decay_raw[p], dims)
            att = _attention(proj, att_sink[p], dims, 4 * RET_HEADS)
            parts, w_out = [ret, att], ev_w_out[p]
        else:
            proj = _in_projection(xs, norm_w[l, 0], mods, od_w_in[p].astype(BF16), dims)
            parts, w_out = [_hgrn2(proj, hg_lb_logits, hg_norm_w[p], l, dims)], od_w_out[p]
        n_logit = MOE_GROUPS + MOE_EXPERTS
        wr = jnp.concatenate([moe_wg[l], jnp.moveaxis(moe_we[l], 0, 1).reshape(d, MOE_EXPERTS),
                              jnp.zeros((d, LANES - n_logit), F32)], axis=1)
        br = jnp.concatenate([moe_bg[l], moe_be[l].reshape(-1), jnp.zeros((LANES - n_logit,), F32)])[None, :]
        xs, h2, route, counts = _out_projection_route(
            parts, w_out.astype(BF16), xs, norm_w[l, 1], mods, wr, br, dims)
        plan = _dispatch_plan(route, counts, nt)
        y2 = _experts(h2, plan, moe_w1[l], moe_w3[l], moe_w2[l])
        xs = _combine(xs, y2, route, mods, dims, final_norm_w if l == depth - 1 else None)

    return xs.reshape(batch, tb, d)[:, n_ctx:]
```

```python
import functools

import numpy as np
import jax
import jax.numpy as jnp
from jax import lax
from jax.experimental import pallas as pl
from jax.experimental.pallas import tpu as pltpu

F32 = jnp.float32
BF16 = jnp.bfloat16
HIGHEST = lax.Precision.HIGHEST

EPS = 1e-6
NEG_INF = -1e30
LANES = 128
GRID_W = 64
ROPE_THETA = 10000.0
HEAD_DIM = 128
RET_HEADS = 4
ATT_Q_HEADS = 4
ATT_KV_HEADS = 2
ATT_GROUP = ATT_Q_HEADS // ATT_KV_HEADS
WINDOW = 128
HG_HEADS = 8
MOE_GROUPS = 4
MOE_EPG = 8
MOE_EXPERTS = MOE_GROUPS * MOE_EPG
MOE_TOP_K = 2
MOE_BLOCK = 128
CHUNK = 128
ROW_TILE = 256
MOD_ROWS = 8
VMEM_LIMIT = 56 * 1024 * 1024


def _cparams(n_axes, **kw):
    return pltpu.CompilerParams(dimension_semantics=("arbitrary",) * n_axes,
                                vmem_limit_bytes=VMEM_LIMIT, **kw)


def _dot(a, b):
    return jnp.dot(a, b, preferred_element_type=F32)


def _dot_nt(a, b):
    return lax.dot_general(a, b, (((1,), (1,)), ((), ())), preferred_element_type=F32)


def _dot_tn(a, b):
    return lax.dot_general(a, b, (((0,), (0,)), ((), ())), preferred_element_type=F32)


def _silu(x):
    return x * jax.nn.sigmoid(x)


def _norm_mod(x, nw, scale, shift):
    ms = jnp.mean(x * x, axis=-1, keepdims=True)
    y = x * lax.rsqrt(ms + EPS) * nw
    return y * (1.0 + scale) + shift


def _ada_kernel(c_ref, w_ref, b_ref, o_ref):
    s = _silu(c_ref[...])
    o_ref[0] = jnp.dot(s, w_ref[0], precision=HIGHEST, preferred_element_type=F32) + b_ref[0]


def _ada_modulation(cvec, ada_w, ada_b):
    depth, d, n6 = ada_w.shape
    tn = 1536
    return pl.pallas_call(
        _ada_kernel,
        out_shape=jax.ShapeDtypeStruct((depth, MOD_ROWS, n6), F32),
        grid=(depth, n6 // tn),
        in_specs=[pl.BlockSpec((MOD_ROWS, d), lambda l, j: (0, 0)),
                  pl.BlockSpec((1, d, tn), lambda l, j: (l, 0, j)),
                  pl.BlockSpec((1, 1, tn), lambda l, j: (l, 0, j))],
        out_specs=pl.BlockSpec((1, MOD_ROWS, tn), lambda l, j: (l, 0, j)),
        compiler_params=_cparams(2),
    )(cvec, ada_w, ada_b.reshape(depth, 1, n6))


def _swap_halves(x):
    lane = lax.broadcasted_iota(jnp.int32, x.shape, 1)
    return jnp.where((lane % 64) < 32, pltpu.roll(x, 96, 1), pltpu.roll(x, 32, 1))


def _proj_kernel(x_ref, nw_ref, shift_ref, scale_ref, w_ref, *rest, col_ops, chunk):
    if col_ops is None:
        (o_ref,) = rest
    else:
        cos_ref, sin_ref, o_ref = rest
    h = _norm_mod(x_ref[...], nw_ref[...], scale_ref[0], shift_ref[0]).astype(BF16)
    nout = o_ref.shape[1]
    for c0 in range(0, nout, chunk):
        acc = _dot(h, w_ref[:, c0:c0 + chunk])
        for hd in range(chunk // LANES):
            col = c0 + hd * LANES
            a = acc[:, hd * LANES:(hd + 1) * LANES]
            if col_ops is not None:
                rope, pre, post = col_ops[col // LANES]
                if pre != 1.0:
                    a = a * pre
                if rope:
                    a = a * cos_ref[...] + _swap_halves(a) * sin_ref[...]
                if post != 1.0:
                    a = a * post
            o_ref[:, col:col + LANES] = a


def _tile_mod_row(i, tiles_per_batch, ctx_tiles, batch):
    return jnp.where(i % tiles_per_batch < ctx_tiles, batch, i // tiles_per_batch)


def _mod_spec(chunk_idx, d, tiles_per_batch, ctx_tiles, batch):
    return pl.BlockSpec(
        (1, 1, d), lambda i: (_tile_mod_row(i, tiles_per_batch, ctx_tiles, batch), 0, chunk_idx))


def _in_projection(xs, nw, mods, w_bf, dims, col_ops=None, rope=None):
    nt, d = xs.shape
    nout = w_bf.shape[1]
    batch, n_ctx, seq = dims
    tpb = (n_ctx + seq) // ROW_TILE
    ctx_tiles = n_ctx // ROW_TILE
    in_specs = [pl.BlockSpec((ROW_TILE, d), lambda i: (i, 0)),
                pl.BlockSpec((1, d), lambda i: (0, 0)),
                _mod_spec(0, d, tpb, ctx_tiles, batch),
                _mod_spec(1, d, tpb, ctx_tiles, batch),
                pl.BlockSpec((d, nout), lambda i: (0, 0))]
    args = [xs, nw.reshape(1, d), mods, mods, w_bf]
    if col_ops is not None:
        rope_spec = pl.BlockSpec((ROW_TILE, LANES), lambda i: (i % tpb, 0))
        in_specs += [rope_spec, rope_spec]
        args += [rope[0], rope[1]]
    return pl.pallas_call(
        functools.partial(_proj_kernel, col_ops=col_ops, chunk=512),
        out_shape=jax.ShapeDtypeStruct((nt, nout), F32),
        grid=(nt // ROW_TILE,),
        in_specs=in_specs,
        out_specs=pl.BlockSpec((ROW_TILE, nout), lambda i: (i, 0)),
        compiler_params=_cparams(1),
    )(*args)


def _rope_tables(n_ctx, seq):
    n_rows = seq // GRID_W
    row = jnp.repeat(jnp.arange(n_rows, dtype=F32), GRID_W)
    col = jnp.tile(jnp.arange(GRID_W, dtype=F32), n_rows)
    axis_dim = HEAD_DIM // 2
    inv_freq = ROPE_THETA ** (-jnp.arange(0, axis_dim, 2, dtype=F32) / axis_dim)
    ang_r = row[:, None] * inv_freq[None, :]
    ang_c = col[:, None] * inv_freq[None, :]
    cos = jnp.concatenate([jnp.cos(ang_r), jnp.cos(ang_r), jnp.cos(ang_c), jnp.cos(ang_c)], axis=-1)
    sin = jnp.concatenate([-jnp.sin(ang_r), jnp.sin(ang_r), -jnp.sin(ang_c), jnp.sin(ang_c)], axis=-1)
    cos = jnp.concatenate([jnp.ones((n_ctx, HEAD_DIM), F32), cos], axis=0)
    sin = jnp.concatenate([jnp.zeros((n_ctx, HEAD_DIM), F32), sin], axis=0)
    return cos, sin


def _backward_chunk(t, ctx_chunks, n_chunks):
    return jnp.where(t < ctx_chunks, ctx_chunks - 1 - t, n_chunks - 1 - (t - ctx_chunks))


def _ret_kernel(raw_ref, q_ref, k_ref, v_ref, g_ref, o_ref, of_scr, ob_scr, s_scr, dec_scr, qd_scr, kd_scr,
                *, ctx_chunks, n_chunks):
    h = pl.program_id(1)
    c = CHUNK
    ii = lax.broadcasted_iota(jnp.int32, (c, c), 0).astype(F32)
    jj = lax.broadcasted_iota(jnp.int32, (c, c), 1).astype(F32)
    lg = []
    for d in range(2):
        lgd = -jnp.exp(jnp.full((c, c), raw_ref[d, h], F32))
        lg.append(lgd)
        rel = (ii - jj) if d == 0 else (jj - ii)
        dec_scr[d] = jnp.where(rel >= 0, jnp.exp(jnp.maximum(rel, 0.0) * lgd), 0.0)
        qd_scr[d] = jnp.exp(((ii + 1.0) if d == 0 else (c - ii)) * lgd)
        kd_scr[d] = jnp.exp(((c - 1.0 - ii) if d == 0 else ii) * lgd)

    def chunk(n, d):
        rows = pl.ds(pl.multiple_of(n * c, c), c)
        q = q_ref[rows, :]
        k = k_ref[rows, :]
        vb = v_ref[rows, :].astype(BF16)
        sc = _dot_nt(q.astype(BF16), k.astype(BF16)) * dec_scr[d]
        s = s_scr[d]
        o = _dot(sc.astype(BF16), vb) + _dot((q * qd_scr[d]).astype(BF16), s.astype(BF16))
        chunk_decay = jnp.exp(float(c) * lg[d][0:1, :])
        s_scr[d] = chunk_decay * s + _dot_tn((k * kd_scr[d]).astype(BF16), vb)
        return rows, o

    s_scr[...] = jnp.zeros_like(s_scr)

    def scan(t, carry):
        rows, o = chunk(t, 0)
        of_scr[rows, :] = o
        rows, o = chunk(_backward_chunk(t, ctx_chunks, n_chunks), 1)
        ob_scr[rows, :] = o
        return carry

    lax.fori_loop(0, n_chunks, scan, 0)

    def readout(n, carry):
        rows = pl.ds(pl.multiple_of(n * c, c), c)
        o = of_scr[rows, :] + ob_scr[rows, :]
        y = o * lax.rsqrt(jnp.mean(o * o, axis=-1, keepdims=True) + EPS)
        o_ref[rows, :] = (_silu(g_ref[rows, :]) * y).astype(BF16)
        return carry

    lax.fori_loop(0, n_chunks, readout, 0)


def _retention(proj, ret_decay_raw, dims):
    batch, n_ctx, seq = dims
    tb = n_ctx + seq
    spec = lambda off: pl.BlockSpec((tb, LANES), lambda b, h: (b, off + h))
    return pl.pallas_call(
        functools.partial(_ret_kernel, ctx_chunks=n_ctx // CHUNK, n_chunks=tb // CHUNK),
        out_shape=jax.ShapeDtypeStruct((batch * tb, RET_HEADS * LANES), BF16),
        grid=(batch, RET_HEADS),
        in_specs=[pl.BlockSpec(memory_space=pltpu.SMEM),
                  spec(0), spec(RET_HEADS), spec(2 * RET_HEADS), spec(3 * RET_HEADS)],
        out_specs=pl.BlockSpec((tb, LANES), lambda b, h: (b, h)),
        scratch_shapes=[pltpu.VMEM((tb, LANES), F32), pltpu.VMEM((tb, LANES), F32),
                        pltpu.VMEM((2, LANES, LANES), F32),
                        pltpu.VMEM((2, CHUNK, CHUNK), F32), pltpu.VMEM((2, CHUNK, LANES), F32),
                        pltpu.VMEM((2, CHUNK, LANES), F32)],
        compiler_params=_cparams(2),
    )(ret_decay_raw, proj, proj, proj, proj)


def _att_kernel(sink_ref, q0_ref, q1_ref, k_ref, v_ref, o_ref, kb_scr, vb_scr, *, n_ctx, seq):
    hk = pl.program_id(1)
    c = CHUNK
    win = 3 * c
    kb_scr[...] = k_ref[...].astype(BF16)
    vb_scr[...] = v_ref[...].astype(BF16)
    kc = kb_scr[0:n_ctx, :]
    vc = vb_scr[0:n_ctx, :]
    ii = lax.broadcasted_iota(jnp.int32, (c, win), 0)
    jj = lax.broadcasted_iota(jnp.int32, (c, win), 1)

    def softmax_out(s_ctx, sink, s_loc=None, v_loc=None):
        m = jnp.maximum(jnp.max(s_ctx, axis=-1, keepdims=True), sink)
        if s_loc is not None:
            m = jnp.maximum(m, jnp.max(s_loc, axis=-1, keepdims=True))
        p_ctx = jnp.exp(s_ctx - m)
        den = jnp.sum(p_ctx, axis=-1, keepdims=True) + jnp.exp(sink - m)
        o = _dot(p_ctx.astype(BF16), vc)
        if s_loc is not None:
            p_loc = jnp.exp(s_loc - m)
            den = den + jnp.sum(p_loc, axis=-1, keepdims=True)
            o = o + _dot(p_loc.astype(BF16), v_loc)
        return o / den

    for g, q_ref in enumerate((q0_ref, q1_ref)):
        sink = jnp.full((c, 1), sink_ref[hk * ATT_GROUP + g], F32)
        cols = slice(g * LANES, (g + 1) * LANES)

        for cc in range(n_ctx // c):
            qb = q_ref[cc * c:(cc + 1) * c, :].astype(BF16)
            o_ref[cc * c:(cc + 1) * c, cols] = softmax_out(_dot_nt(qb, kc), sink).astype(BF16)

        def block(n, carry):
            q_rows = pl.ds(pl.multiple_of(n_ctx + n * c, c), c)
            qb = q_ref[q_rows, :].astype(BF16)
            start = jnp.clip(n * c - c, 0, seq - win)
            k_rows = pl.ds(pl.multiple_of(n_ctx + start, c), win)
            rel = ii - jj + (n * c - start)
            s_loc = jnp.where(jnp.abs(rel) <= WINDOW, _dot_nt(qb, kb_scr[k_rows, :]), NEG_INF)
            o = softmax_out(_dot_nt(qb, kc), sink, s_loc, vb_scr[k_rows, :])
            o_ref[q_rows, cols] = o.astype(BF16)
            return carry

        lax.fori_loop(0, seq // c, block, 0)


def _attention(proj, att_sink, dims, col0):
    batch, n_ctx, seq = dims
    tb = n_ctx + seq
    qb, kb, vb = col0, col0 + ATT_Q_HEADS, col0 + ATT_Q_HEADS + ATT_KV_HEADS
    return pl.pallas_call(
        functools.partial(_att_kernel, n_ctx=n_ctx, seq=seq),
        out_shape=jax.ShapeDtypeStruct((batch * tb, ATT_Q_HEADS * LANES), BF16),
        grid=(batch, ATT_KV_HEADS),
        in_specs=[pl.BlockSpec(memory_space=pltpu.SMEM),
                  pl.BlockSpec((tb, LANES), lambda b, h: (b, qb + ATT_GROUP * h)),
                  pl.BlockSpec((tb, LANES), lambda b, h: (b, qb + ATT_GROUP * h + 1)),
                  pl.BlockSpec((tb, LANES), lambda b, h: (b, kb + h)),
                  pl.BlockSpec((tb, LANES), lambda b, h: (b, vb + h))],
        out_specs=pl.BlockSpec((tb, ATT_GROUP * LANES), lambda b, h: (b, h)),
        scratch_shapes=[pltpu.VMEM((tb, LANES), BF16), pltpu.VMEM((tb, LANES), BF16)],
        compiler_params=_cparams(2),
    )(att_sink, proj, proj, proj, proj)


HG_LEVELS = 7
HG_MXU_LEVELS = 3


def _hgrn_constants():
    c = CHUNK
    i = np.arange(c)[:, None]
    r = np.arange(c)[None, :]
    mats = [r <= i]
    for m in range(1, HG_MXU_LEVELS + 1):
        half = 1 << (m - 1)
        beta = (i // (2 * half)) * (2 * half) + half - 1
        upper = (i % (2 * half)) >= half
        mats.append((upper & (r > beta) & (r <= i)) | ((~upper) & (r > i) & (r <= beta)))
    fwd = [m.astype(np.float32) for m in mats]
    bwd = [m[::-1, ::-1] for m in fwd]
    stack = np.stack([np.concatenate(fwd, 0), np.concatenate(bwd, 0)])
    x = i ^ r
    level = np.where(r > i, -1, np.where(r == i, 0, np.floor(np.log2(np.maximum(x, 1))) + 1)).astype(np.int32)
    levels = np.stack([level, level[::-1, ::-1]])
    return stack, levels


def _hgrn_kernel(q_ref, zf_ref, zb_ref, v_ref, g_ref, lbl_ref, nw_ref, m_ref, lv_ref, o_ref,
                 of_scr, ob_scr, st_scr, *, layer, ctx_chunks, n_chunks):
    c = CHUNK
    z = lbl_ref[...]
    e = jnp.exp(z - jnp.max(z, axis=0, keepdims=True))
    sm = e / jnp.sum(e, axis=0, keepdims=True)
    lb = jnp.sum(sm[1:layer + 1], axis=0, keepdims=True)
    row = lax.broadcasted_iota(jnp.int32, (c, LANES), 0)
    upper = [[((row if d == 0 else c - 1 - row) & (1 << (m - 1))) != 0 for m in range(1, HG_MXU_LEVELS + 1)]
             for d in range(2)]

    def level_operand(q, k, a, half, d):
        pieces = []
        for base in range(0, c, 2 * half):
            lo, hi = slice(base, base + half), slice(base + half, base + 2 * half)
            if d == 0:
                ref = a[base + half - 1:base + half]
                pieces += [k[lo] * jnp.exp(ref - a[lo]), q[hi] * jnp.exp(a[hi] - ref)]
            else:
                ref = a[base + half:base + half + 1]
                pieces += [q[lo] * jnp.exp(a[lo] - ref), k[hi] * jnp.exp(ref - a[hi])]
        return jnp.concatenate(pieces, axis=0).astype(BF16)

    def chunk(n, d, z_ref):
        rows = pl.ds(pl.multiple_of(n * c, c), c)
        q = q_ref[rows, :]
        v = v_ref[rows, :]
        f = lb + (1.0 - lb) * jax.nn.sigmoid(z_ref[rows, :])
        g = jnp.log(f)
        k = 1.0 - f
        g_hi = g.astype(BF16)
        g_lo = (g - g_hi.astype(F32)).astype(BF16)
        res = _dot(m_ref[d], jnp.concatenate([g_hi, g_lo], axis=1))
        ex = res[:, :c] + res[:, c:]
        a = ex[0:c]
        st = st_scr[d]
        o = _dot_nt((q * jnp.exp(a)).astype(BF16), st.astype(BF16))
        lv = lv_ref[d]
        sc = jnp.where(lv == 0, _dot_nt(q.astype(BF16), k.astype(BF16)), 0.0)
        for m in range(1, HG_LEVELS + 1):
            if m <= HG_MXU_LEVELS:
                zz = (jnp.where(upper[d][m - 1], q, k) * jnp.exp(ex[m * c:(m + 1) * c])).astype(BF16)
            else:
                zz = level_operand(q, k, a, 1 << (m - 1), d)
            sc = jnp.where(lv == m, _dot_nt(zz, zz), sc)
        vb = v.astype(BF16)
        o = o + _dot(sc.astype(BF16), vb)
        a_last = a[c - 1:c] if d == 0 else a[0:1]
        kd = (k * jnp.exp(a_last - a)).astype(BF16)
        st_scr[d] = st * jnp.exp(a_last) + _dot_tn(vb, kd)
        return rows, o

    st_scr[...] = jnp.zeros_like(st_scr)

    def scan(t, carry):
        rows, o = chunk(t, 0, zf_ref)
        of_scr[rows, :] = o
        rows, o = chunk(_backward_chunk(t, ctx_chunks, n_chunks), 1, zb_ref)
        ob_scr[rows, :] = o
        return carry

    lax.fori_loop(0, n_chunks, scan, 0, unroll=2)

    def readout(n, carry):
        rows = pl.ds(pl.multiple_of(n * c, c), c)
        o = of_scr[rows, :] + ob_scr[rows, :]
        y = o * lax.rsqrt(jnp.mean(o * o, axis=-1, keepdims=True) + EPS) * nw_ref[...]
        o_ref[rows, :] = (y * _silu(g_ref[rows, :])).astype(BF16)
        return carry

    lax.fori_loop(0, n_chunks, readout, 0)


def _hgrn2(proj, hg_lb_logits, hg_norm_w, layer, dims):
    batch, n_ctx, seq = dims
    tb = n_ctx + seq
    depth = hg_lb_logits.shape[0]
    stack, levels = _hgrn_constants()
    spec = lambda off: pl.BlockSpec((tb, LANES), lambda b, h: (b, off + h))
    n_mats = 1 + HG_MXU_LEVELS
    return pl.pallas_call(
        functools.partial(_hgrn_kernel, layer=layer, ctx_chunks=n_ctx // CHUNK, n_chunks=tb // CHUNK),
        out_shape=jax.ShapeDtypeStruct((batch * tb, HG_HEADS * LANES), BF16),
        grid=(batch, HG_HEADS),
        in_specs=[spec(0), spec(HG_HEADS), spec(2 * HG_HEADS), spec(3 * HG_HEADS), spec(4 * HG_HEADS),
                  pl.BlockSpec((depth, LANES), lambda b, h: (0, h)),
                  pl.BlockSpec((1, LANES), lambda b, h: (0, 0)),
                  pl.BlockSpec((2, n_mats * CHUNK, CHUNK), lambda b, h: (0, 0, 0)),
                  pl.BlockSpec((2, CHUNK, CHUNK), lambda b, h: (0, 0, 0))],
        out_specs=pl.BlockSpec((tb, LANES), lambda b, h: (b, h)),
        scratch_shapes=[pltpu.VMEM((tb, LANES), F32), pltpu.VMEM((tb, LANES), F32),
                        pltpu.VMEM((2, LANES, LANES), F32)],
        compiler_params=_cparams(2),
    )(proj, proj, proj, proj, proj, hg_lb_logits, hg_norm_w.reshape(1, LANES),
      jnp.asarray(stack, BF16), jnp.asarray(levels, jnp.int32))


SUBLANES = 8


def _store_token_tiles(ref, x):
    for s in range(x.shape[1] // LANES):
        ref[:, s, :] = x[:, s * LANES:(s + 1) * LANES]


def _load_token_tiles(ref):
    return jnp.concatenate([ref[:, s, :] for s in range(ref.shape[1])], axis=1)


def _route(logits, tri, counts):
    lane = lax.broadcasted_iota(jnp.int32, logits.shape, 1).astype(F32)
    big = float(LANES)
    first = lambda hit: jnp.min(jnp.where(hit, lane, big), axis=-1, keepdims=True)
    is_g = lane < MOE_GROUPS
    gl = jnp.where(is_g, logits, NEG_INF)
    gmax = jnp.max(gl, axis=-1, keepdims=True)
    gsum = jnp.sum(jnp.where(is_g, jnp.exp(gl - gmax), 0.0), axis=-1, keepdims=True)
    g_p = 1.0 / gsum
    lo = MOE_GROUPS + first(gl == gmax) * MOE_EPG
    el = jnp.where((lane >= lo) & (lane < lo + MOE_EPG), logits, NEG_INF)
    e1v = jnp.max(el, axis=-1, keepdims=True)
    e1 = first(el == e1v)
    el2 = jnp.where(lane == e1, NEG_INF, el)
    e2v = jnp.max(el2, axis=-1, keepdims=True)
    e2 = first(el2 == e2v)
    t = jnp.exp(e2v - e1v)
    w1 = 1.0 / (1.0 + t)
    w2 = t * w1
    id1 = e1 - MOE_GROUPS
    id2 = e2 - MOE_GROUPS
    onehot = jnp.where((lane == id1) | (lane == id2), 1.0, 0.0)
    before = counts + _dot(tri, onehot.astype(BF16))
    rank1 = jnp.sum(jnp.where(lane == id1, before, 0.0), axis=-1, keepdims=True)
    rank2 = jnp.sum(jnp.where(lane == id2, before, 0.0), axis=-1, keepdims=True)
    slab = jnp.zeros_like(logits)
    for col, val in enumerate((id1, id2, g_p * w1, g_p * w2, rank1, rank2)):
        slab = jnp.where(lane == col, val, slab)
    return slab, counts + jnp.sum(onehot, axis=0, keepdims=True)


def _out_route_kernel(*refs, n_parts):
    a_refs = refs[:n_parts]
    (w_ref, x_ref, gate_ref, nw_ref, shift_ref, scale_ref, wr_ref, br_ref, tri_ref,
     xo_ref, h2_ref, route_ref, cnt_ref) = refs[n_parts:]

    @pl.when(pl.program_id(0) == 0)
    def _():
        cnt_ref[...] = jnp.zeros_like(cnt_ref)

    k0 = 0
    y = None
    for a_ref in a_refs:
        kk = a_ref.shape[1]
        t = _dot(a_ref[...], w_ref[k0:k0 + kk, :])
        y = t if y is None else y + t
        k0 += kk
    x = x_ref[...] + gate_ref[0] * y
    xo_ref[...] = x
    h2 = _norm_mod(x, nw_ref[...], scale_ref[0], shift_ref[0])
    _store_token_tiles(h2_ref, h2)
    logits = jnp.dot(h2, wr_ref[...], precision=HIGHEST, preferred_element_type=F32) + br_ref[...]
    slab, counts = _route(logits, tri_ref[...], cnt_ref[0:1, :])
    route_ref[...] = slab
    cnt_ref[...] = jnp.broadcast_to(counts, cnt_ref.shape)


def _out_projection_route(parts, w_bf, xs, nw, mods, wr, br, dims):
    nt, d = xs.shape
    batch, n_ctx, seq = dims
    tpb = (n_ctx + seq) // ROW_TILE
    ctx_tiles = n_ctx // ROW_TILE
    row = lambda width: pl.BlockSpec((ROW_TILE, width), lambda i: (i, 0))
    full = lambda shape: pl.BlockSpec(shape, lambda i: (0,) * len(shape))
    mod = lambda chunk_idx: _mod_spec(chunk_idx, d, tpb, ctx_tiles, batch)
    tri = jnp.asarray(np.tril(np.ones((ROW_TILE, ROW_TILE), np.float32), -1), BF16)
    return pl.pallas_call(
        functools.partial(_out_route_kernel, n_parts=len(parts)),
        out_shape=(jax.ShapeDtypeStruct((nt, d), F32), jax.ShapeDtypeStruct((nt, d // LANES, LANES), F32),
                   jax.ShapeDtypeStruct((nt, LANES), F32), jax.ShapeDtypeStruct((8, LANES), F32)),
        grid=(nt // ROW_TILE,),
        in_specs=[row(p.shape[1]) for p in parts] + [
            full(w_bf.shape), row(d), mod(2), full((1, d)), mod(3), mod(4), full(wr.shape), full(br.shape),
            full(tri.shape)],
        out_specs=(row(d), pl.BlockSpec((ROW_TILE, d // LANES, LANES), lambda i: (i, 0, 0)), row(LANES),
                   full((8, LANES))),
        compiler_params=_cparams(1),
    )(*parts, w_bf, xs, mods, nw.reshape(1, d), mods, mods, wr, br, tri)


def _dispatch_plan(route, counts_f, n_tok):
    n_assign = n_tok * MOE_TOP_K
    flat_e = route[:, 0:MOE_TOP_K].astype(jnp.int32).reshape(-1)
    rank = route[:, 4:4 + MOE_TOP_K].astype(jnp.int32).reshape(-1)
    counts = counts_f[0, :MOE_EXPERTS].astype(jnp.int32)
    padded = (counts + MOE_BLOCK - 1) // MOE_BLOCK * MOE_BLOCK
    pad_end = jnp.cumsum(padded)
    pad_start = pad_end - padded
    dest = pad_start[flat_e] + rank
    n_blocks = -(-n_assign // MOE_BLOCK) + MOE_EXPERTS
    n_slots = n_blocks * MOE_BLOCK
    slot_assign = jnp.full((n_slots,), -1, jnp.int32).at[dest].set(jnp.arange(n_assign, dtype=jnp.int32))
    block_start = jnp.arange(n_blocks, dtype=jnp.int32) * MOE_BLOCK
    block_expert = jnp.minimum(jnp.sum((pad_end[None, :] <= block_start[:, None]).astype(jnp.int32), axis=1),
                               MOE_EXPERTS - 1)
    slot = jnp.arange(-MOE_BLOCK, n_slots + MOE_BLOCK, dtype=jnp.int32)
    spare = n_assign + ((slot // MOE_BLOCK) % 2) * MOE_BLOCK + slot % MOE_BLOCK
    pad = jnp.full((MOE_BLOCK,), -1, jnp.int32)
    assign = jnp.concatenate([pad, slot_assign, pad])
    src = jnp.where(assign >= 0, assign // MOE_TOP_K, 0)
    dst = jnp.where(assign >= 0, (assign % MOE_TOP_K) * n_tok + assign // MOE_TOP_K, spare)
    return block_expert, src[MOE_BLOCK:], dst[:n_slots + MOE_BLOCK], n_blocks


def _expert_kernel(be_ref, src_ref, dst_ref, h_hbm, w1_ref, w3_ref, w2_ref, out_hbm,
                   xbuf, ybuf, w1b, w3b, w2b, gsem, ssem):
    i = pl.program_id(0)
    last = pl.num_programs(0) - 1
    cur = i % 2
    nxt = 1 - cur

    def start_gather(blk, buf):
        for r in range(MOE_BLOCK):
            pltpu.make_async_copy(h_hbm.at[src_ref[blk * MOE_BLOCK + r]], xbuf.at[buf, r], gsem.at[buf]).start()

    def start_scatter(blk, buf):
        for r in range(MOE_BLOCK):
            pltpu.make_async_copy(ybuf.at[buf, r], out_hbm.at[dst_ref[(blk + 1) * MOE_BLOCK + r]],
                                  ssem.at[buf]).start()

    def wait_gather(buf):
        pltpu.make_async_copy(h_hbm.at[pl.ds(0, MOE_BLOCK)], xbuf.at[buf], gsem.at[buf]).wait()

    def wait_scatter(buf):
        pltpu.make_async_copy(ybuf.at[buf], out_hbm.at[pl.ds(0, MOE_BLOCK)], ssem.at[buf]).wait()

    @pl.when(i == 0)
    def _():
        ybuf[...] = jnp.zeros_like(ybuf)
        start_gather(0, 0)

    @pl.when((i == 0) | (be_ref[i] != be_ref[jnp.maximum(i - 1, 0)]))
    def _():
        w1b[...] = w1_ref[0, 0].astype(BF16)
        w3b[...] = w3_ref[0, 0].astype(BF16)
        w2b[...] = w2_ref[0, 0].astype(BF16)

    wait_gather(cur)

    @pl.when(i >= 1)
    def _():
        wait_scatter(cur)

    xb = _load_token_tiles(xbuf.at[cur]).astype(BF16)
    act = _silu(_dot(xb, w1b[...])) * _dot(xb, w3b[...])
    _store_token_tiles(ybuf.at[cur], _dot(act.astype(BF16), w2b[...]))
    start_gather(i + 1, nxt)
    start_scatter(i - 1, nxt)

    @pl.when(i == last)
    def _():
        start_scatter(i, cur)
        wait_scatter(nxt)
        wait_scatter(cur)
        wait_gather(nxt)


def _experts(h2, plan, w1, w3, w2, layer):
    block_expert, slot_src, slot_dst, n_blocks = plan
    nt, tiles, _ = h2.shape
    d = tiles * LANES
    ff = w1.shape[3]
    wspec = lambda shape: pl.BlockSpec((1, 1) + shape, lambda i, be, sr, ds: (layer, be[i], 0, 0))
    grid_spec = pltpu.PrefetchScalarGridSpec(
        num_scalar_prefetch=3,
        grid=(n_blocks,),
        in_specs=[pl.BlockSpec(memory_space=pl.ANY), wspec((d, ff)), wspec((d, ff)), wspec((ff, d))],
        out_specs=pl.BlockSpec(memory_space=pl.ANY),
        scratch_shapes=[pltpu.VMEM((2, MOE_BLOCK, tiles, LANES), F32), pltpu.VMEM((2, MOE_BLOCK, tiles, LANES), F32),
                        pltpu.VMEM((d, ff), BF16), pltpu.VMEM((d, ff), BF16), pltpu.VMEM((ff, d), BF16),
                        pltpu.SemaphoreType.DMA((2,)), pltpu.SemaphoreType.DMA((2,))])
    return pl.pallas_call(
        _expert_kernel,
        out_shape=jax.ShapeDtypeStruct((nt * MOE_TOP_K + 2 * MOE_BLOCK, tiles, LANES), F32),
        grid_spec=grid_spec,
        compiler_params=_cparams(1, has_side_effects=True, disable_bounds_checks=True),
    )(block_expert, slot_src, slot_dst, h2, w1, w3, w2)


def _combine_kernel(x_ref, y0_ref, y1_ref, route_ref, gate_ref, *rest, final):
    r = route_ref[...]
    y = r[:, 2:3] * _load_token_tiles(y0_ref) + r[:, 3:4] * _load_token_tiles(y1_ref)
    x = x_ref[...] + gate_ref[0] * y
    if final:
        fnw_ref, o_ref = rest
        x = x * lax.rsqrt(jnp.mean(x * x, axis=-1, keepdims=True) + EPS) * fnw_ref[...]
    else:
        (o_ref,) = rest
    o_ref[...] = x


def _combine(xs, y2, route, mods, dims, final_norm_w=None):
    nt, d = xs.shape
    batch, n_ctx, seq = dims
    tpb = (n_ctx + seq) // ROW_TILE
    ctx_tiles = n_ctx // ROW_TILE
    row = lambda width: pl.BlockSpec((ROW_TILE, width), lambda i: (i, 0))
    final = final_norm_w is not None
    first = pl.BlockSpec((ROW_TILE, d // LANES, LANES), lambda i: (i, 0, 0))
    second = pl.BlockSpec((ROW_TILE, d // LANES, LANES), lambda i: (i + nt // ROW_TILE, 0, 0))
    in_specs = [row(d), first, second, row(LANES), _mod_spec(5, d, tpb, ctx_tiles, batch)]
    args = [xs, y2, y2, route, mods]
    if final:
        in_specs.append(pl.BlockSpec((1, d), lambda i: (0, 0)))
        args.append(final_norm_w.reshape(1, d))
    return pl.pallas_call(
        functools.partial(_combine_kernel, final=final),
        out_shape=jax.ShapeDtypeStruct((nt, d), F32),
        grid=(nt // ROW_TILE,),
        in_specs=in_specs,
        out_specs=row(d),
        compiler_params=_cparams(1),
    )(*args)


def _even_col_ops():
    rh = RET_HEADS
    ops = [(True, 1.0, 1.0)] * rh + [(True, float(HEAD_DIM) ** -0.5, 1.0)] * rh + [(False, 1.0, 1.0)] * (2 * rh)
    ops += [(True, 1.0, float(HEAD_DIM) ** -0.5)] * ATT_Q_HEADS + [(True, 1.0, 1.0)] * ATT_KV_HEADS
    ops += [(False, 1.0, 1.0)] * ATT_KV_HEADS
    return tuple(ops)


def kernel(x, c, ctx, c_ctx, ada_w, ada_b, norm_w, final_norm_w, ev_w_in, ev_w_out, ret_decay_raw, att_sink,
           od_w_in, od_w_out, hg_lb_logits, hg_norm_w, moe_wg, moe_bg, moe_we, moe_be, moe_w1, moe_w3, moe_w2):
    batch, seq, d = x.shape
    n_ctx = ctx.shape[1]
    depth = ada_w.shape[0]
    assert batch + 1 <= MOD_ROWS and n_ctx % ROW_TILE == 0 and seq % ROW_TILE == 0
    assert seq % GRID_W == 0 and seq >= 3 * CHUNK
    dims = (batch, n_ctx, seq)
    tb = n_ctx + seq
    nt = batch * tb

    xs = jnp.concatenate([ctx, x], axis=1).reshape(nt, d)
    cvec = jnp.concatenate([c, c_ctx[None, :], jnp.zeros((MOD_ROWS - batch - 1, d), F32)], axis=0)
    mods_all = _ada_modulation(cvec, ada_w, ada_b)
    rope = _rope_tables(n_ctx, seq)
    even_ops = _even_col_ops()

    for l in range(depth):
        p = l // 2
        mods = mods_all[l].reshape(MOD_ROWS, 1, 6 * d)
        if l % 2 == 0:
            proj = _in_projection(xs, norm_w[l, 0], mods, ev_w_in[p].astype(BF16), dims, even_ops, rope)
            ret = _retention(proj, ret_decay_raw[p], dims)
            att = _attention(proj, att_sink[p], dims, 4 * RET_HEADS)
            parts, w_out = [ret, att], ev_w_out[p]
        else:
            proj = _in_projection(xs, norm_w[l, 0], mods, od_w_in[p].astype(BF16), dims)
            parts, w_out = [_hgrn2(proj, hg_lb_logits, hg_norm_w[p], l, dims)], od_w_out[p]
        n_logit = MOE_GROUPS + MOE_EXPERTS
        wr = jnp.concatenate([moe_wg[l], jnp.moveaxis(moe_we[l], 0, 1).reshape(d, MOE_EXPERTS),
                              jnp.zeros((d, LANES - n_logit), F32)], axis=1)
        br = jnp.concatenate([moe_bg[l], moe_be[l].reshape(-1), jnp.zeros((LANES - n_logit,), F32)])[None, :]
        xs, h2, route, counts = _out_projection_route(
            parts, w_out.astype(BF16), xs, norm_w[l, 1], mods, wr, br, dims)
        plan = _dispatch_plan(route, counts, nt)
        y2 = _experts(h2, plan, moe_w1, moe_w3, moe_w2, l)
        xs = _combine(xs, y2, route, mods, dims, final_norm_w if l == depth - 1 else None)

    return xs.reshape(batch, tb, d)[:, n_ctx:]
```

```python
import functools

import numpy as np
import jax
import jax.numpy as jnp
from jax import lax
from jax.experimental import pallas as pl
from jax.experimental.pallas import tpu as pltpu

F32 = jnp.float32
BF16 = jnp.bfloat16
HIGHEST = lax.Precision.HIGHEST

EPS = 1e-6
NEG_INF = -1e30
LANES = 128
GRID_W = 64
ROPE_THETA = 10000.0
HEAD_DIM = 128
RET_HEADS = 4
ATT_Q_HEADS = 4
ATT_KV_HEADS = 2
ATT_GROUP = ATT_Q_HEADS // ATT_KV_HEADS
WINDOW = 128
HG_HEADS = 8
MOE_GROUPS = 4
MOE_EPG = 8
MOE_EXPERTS = MOE_GROUPS * MOE_EPG
MOE_TOP_K = 2
MOE_BLOCK = 128
CHUNK = 128
ROW_TILE = 256
MOD_ROWS = 8
VMEM_LIMIT = 56 * 1024 * 1024


def _cparams(n_axes, **kw):
    return pltpu.CompilerParams(dimension_semantics=("arbitrary",) * n_axes,
                                vmem_limit_bytes=VMEM_LIMIT, **kw)


def _dot(a, b):
    return jnp.dot(a, b, preferred_element_type=F32)


def _dot_nt(a, b):
    return lax.dot_general(a, b, (((1,), (1,)), ((), ())), preferred_element_type=F32)


def _dot_tn(a, b):
    return lax.dot_general(a, b, (((0,), (0,)), ((), ())), preferred_element_type=F32)


def _silu(x):
    return x * jax.nn.sigmoid(x)


def _norm_mod(x, nw, scale, shift):
    ms = jnp.mean(x * x, axis=-1, keepdims=True)
    y = x * lax.rsqrt(ms + EPS) * nw
    return y * (1.0 + scale) + shift


def _ada_kernel(c_ref, w_ref, b_ref, o_ref):
    s = _silu(c_ref[...])
    o_ref[0] = jnp.dot(s, w_ref[0], precision=HIGHEST, preferred_element_type=F32) + b_ref[0]


def _ada_modulation(cvec, ada_w, ada_b):
    depth, d, n6 = ada_w.shape
    tn = 1536
    return pl.pallas_call(
        _ada_kernel,
        out_shape=jax.ShapeDtypeStruct((depth, MOD_ROWS, n6), F32),
        grid=(depth, n6 // tn),
        in_specs=[pl.BlockSpec((MOD_ROWS, d), lambda l, j: (0, 0)),
                  pl.BlockSpec((1, d, tn), lambda l, j: (l, 0, j)),
                  pl.BlockSpec((1, 1, tn), lambda l, j: (l, 0, j))],
        out_specs=pl.BlockSpec((1, MOD_ROWS, tn), lambda l, j: (l, 0, j)),
        compiler_params=_cparams(2),
    )(cvec, ada_w, ada_b.reshape(depth, 1, n6))


def _swap_halves(x):
    lane = lax.broadcasted_iota(jnp.int32, x.shape, 1)
    return jnp.where((lane % 64) < 32, pltpu.roll(x, 96, 1), pltpu.roll(x, 32, 1))


def _proj_kernel(x_ref, nw_ref, shift_ref, scale_ref, w_ref, *rest, col_ops, chunk):
    if col_ops is None:
        (o_ref,) = rest
    else:
        cos_ref, sin_ref, o_ref = rest
    h = _norm_mod(x_ref[...], nw_ref[...], scale_ref[0], shift_ref[0]).astype(BF16)
    nout = o_ref.shape[1]
    for c0 in range(0, nout, chunk):
        acc = _dot(h, w_ref[:, c0:c0 + chunk])
        for hd in range(chunk // LANES):
            col = c0 + hd * LANES
            a = acc[:, hd * LANES:(hd + 1) * LANES]
            if col_ops is not None:
                rope, pre, post = col_ops[col // LANES]
                if pre != 1.0:
                    a = a * pre
                if rope:
                    a = a * cos_ref[...] + _swap_halves(a) * sin_ref[...]
                if post != 1.0:
                    a = a * post
            o_ref[:, col:col + LANES] = a


def _tile_mod_row(i, tiles_per_batch, ctx_tiles, batch):
    return jnp.where(i % tiles_per_batch < ctx_tiles, batch, i // tiles_per_batch)


def _mod_spec(chunk_idx, d, tiles_per_batch, ctx_tiles, batch):
    return pl.BlockSpec(
        (1, 1, d), lambda i: (_tile_mod_row(i, tiles_per_batch, ctx_tiles, batch), 0, chunk_idx))


def _in_projection(xs, nw, mods, w_bf, dims, col_ops=None, rope=None):
    nt, d = xs.shape
    nout = w_bf.shape[1]
    batch, n_ctx, seq = dims
    tpb = (n_ctx + seq) // ROW_TILE
    ctx_tiles = n_ctx // ROW_TILE
    in_specs = [pl.BlockSpec((ROW_TILE, d), lambda i: (i, 0)),
                pl.BlockSpec((1, d), lambda i: (0, 0)),
                _mod_spec(0, d, tpb, ctx_tiles, batch),
                _mod_spec(1, d, tpb, ctx_tiles, batch),
                pl.BlockSpec((d, nout), lambda i: (0, 0))]
    args = [xs, nw.reshape(1, d), mods, mods, w_bf]
    if col_ops is not None:
        rope_spec = pl.BlockSpec((ROW_TILE, LANES), lambda i: (i % tpb, 0))
        in_specs += [rope_spec, rope_spec]
        args += [rope[0], rope[1]]
    return pl.pallas_call(
        functools.partial(_proj_kernel, col_ops=col_ops, chunk=512),
        out_shape=jax.ShapeDtypeStruct((nt, nout), F32),
        grid=(nt // ROW_TILE,),
        in_specs=in_specs,
        out_specs=pl.BlockSpec((ROW_TILE, nout), lambda i: (i, 0)),
        compiler_params=_cparams(1),
    )(*args)


def _rope_tables(n_ctx, seq):
    n_rows = seq // GRID_W
    row = jnp.repeat(jnp.arange(n_rows, dtype=F32), GRID_W)
    col = jnp.tile(jnp.arange(GRID_W, dtype=F32), n_rows)
    axis_dim = HEAD_DIM // 2
    inv_freq = ROPE_THETA ** (-jnp.arange(0, axis_dim, 2, dtype=F32) / axis_dim)
    ang_r = row[:, None] * inv_freq[None, :]
    ang_c = col[:, None] * inv_freq[None, :]
    cos = jnp.concatenate([jnp.cos(ang_r), jnp.cos(ang_r), jnp.cos(ang_c), jnp.cos(ang_c)], axis=-1)
    sin = jnp.concatenate([-jnp.sin(ang_r), jnp.sin(ang_r), -jnp.sin(ang_c), jnp.sin(ang_c)], axis=-1)
    cos = jnp.concatenate([jnp.ones((n_ctx, HEAD_DIM), F32), cos], axis=0)
    sin = jnp.concatenate([jnp.zeros((n_ctx, HEAD_DIM), F32), sin], axis=0)
    return cos, sin


def _backward_chunk(t, ctx_chunks, n_chunks):
    return jnp.where(t < ctx_chunks, ctx_chunks - 1 - t, n_chunks - 1 - (t - ctx_chunks))


def _ret_kernel(raw_ref, q_ref, k_ref, v_ref, g_ref, o_ref, of_scr, ob_scr, s_scr, dec_scr, qd_scr, kd_scr,
                *, ctx_chunks, n_chunks):
    h = pl.program_id(1)
    c = CHUNK
    ii = lax.broadcasted_iota(jnp.int32, (c, c), 0).astype(F32)
    jj = lax.broadcasted_iota(jnp.int32, (c, c), 1).astype(F32)
    lg = []
    for d in range(2):
        lgd = -jnp.exp(jnp.full((c, c), raw_ref[d, h], F32))
        lg.append(lgd)
        rel = (ii - jj) if d == 0 else (jj - ii)
        dec_scr[d] = jnp.where(rel >= 0, jnp.exp(jnp.maximum(rel, 0.0) * lgd), 0.0)
        qd_scr[d] = jnp.exp(((ii + 1.0) if d == 0 else (c - ii)) * lgd)
        kd_scr[d] = jnp.exp(((c - 1.0 - ii) if d == 0 else ii) * lgd)

    def chunk(n, d):
        rows = pl.ds(pl.multiple_of(n * c, c), c)
        q = q_ref[rows, :]
        k = k_ref[rows, :]
        vb = v_ref[rows, :].astype(BF16)
        sc = _dot_nt(q.astype(BF16), k.astype(BF16)) * dec_scr[d]
        s = s_scr[d]
        o = _dot(sc.astype(BF16), vb) + _dot((q * qd_scr[d]).astype(BF16), s.astype(BF16))
        chunk_decay = jnp.exp(float(c) * lg[d][0:1, :])
        s_scr[d] = chunk_decay * s + _dot_tn((k * kd_scr[d]).astype(BF16), vb)
        return rows, o

    s_scr[...] = jnp.zeros_like(s_scr)

    def scan(t, carry):
        rows, o = chunk(t, 0)
        of_scr[rows, :] = o
        rows, o = chunk(_backward_chunk(t, ctx_chunks, n_chunks), 1)
        ob_scr[rows, :] = o
        return carry

    lax.fori_loop(0, n_chunks, scan, 0)

    def readout(n, carry):
        rows = pl.ds(pl.multiple_of(n * c, c), c)
        o = of_scr[rows, :] + ob_scr[rows, :]
        y = o * lax.rsqrt(jnp.mean(o * o, axis=-1, keepdims=True) + EPS)
        o_ref[rows, :] = (_silu(g_ref[rows, :]) * y).astype(BF16)
        return carry

    lax.fori_loop(0, n_chunks, readout, 0)


def _retention(proj, ret_decay_raw, dims):
    batch, n_ctx, seq = dims
    tb = n_ctx + seq
    spec = lambda off: pl.BlockSpec((tb, LANES), lambda b, h: (b, off + h))
    return pl.pallas_call(
        functools.partial(_ret_kernel, ctx_chunks=n_ctx // CHUNK, n_chunks=tb // CHUNK),
        out_shape=jax.ShapeDtypeStruct((batch * tb, RET_HEADS * LANES), BF16),
        grid=(batch, RET_HEADS),
        in_specs=[pl.BlockSpec(memory_space=pltpu.SMEM),
                  spec(0), spec(RET_HEADS), spec(2 * RET_HEADS), spec(3 * RET_HEADS)],
        out_specs=pl.BlockSpec((tb, LANES), lambda b, h: (b, h)),
        scratch_shapes=[pltpu.VMEM((tb, LANES), F32), pltpu.VMEM((tb, LANES), F32),
                        pltpu.VMEM((2, LANES, LANES), F32),
                        pltpu.VMEM((2, CHUNK, CHUNK), F32), pltpu.VMEM((2, CHUNK, LANES), F32),
                        pltpu.VMEM((2, CHUNK, LANES), F32)],
        compiler_params=_cparams(2),
    )(ret_decay_raw, proj, proj, proj, proj)


def _att_kernel(sink_ref, q0_ref, q1_ref, k_ref, v_ref, o_ref, kb_scr, vb_scr, *, n_ctx, seq):
    hk = pl.program_id(1)
    c = CHUNK
    win = 3 * c
    kb_scr[...] = k_ref[...].astype(BF16)
    vb_scr[...] = v_ref[...].astype(BF16)
    kc = kb_scr[0:n_ctx, :]
    vc = vb_scr[0:n_ctx, :]
    ii = lax.broadcasted_iota(jnp.int32, (c, win), 0)
    jj = lax.broadcasted_iota(jnp.int32, (c, win), 1)

    def softmax_out(s_ctx, sink, s_loc=None, v_loc=None):
        m = jnp.maximum(jnp.max(s_ctx, axis=-1, keepdims=True), sink)
        if s_loc is not None:
            m = jnp.maximum(m, jnp.max(s_loc, axis=-1, keepdims=True))
        p_ctx = jnp.exp(s_ctx - m)
        den = jnp.sum(p_ctx, axis=-1, keepdims=True) + jnp.exp(sink - m)
        o = _dot(p_ctx.astype(BF16), vc)
        if s_loc is not None:
            p_loc = jnp.exp(s_loc - m)
            den = den + jnp.sum(p_loc, axis=-1, keepdims=True)
            o = o + _dot(p_loc.astype(BF16), v_loc)
        return o / den

    for g, q_ref in enumerate((q0_ref, q1_ref)):
        sink = jnp.full((c, 1), sink_ref[hk * ATT_GROUP + g], F32)
        cols = slice(g * LANES, (g + 1) * LANES)

        for cc in range(n_ctx // c):
            qb = q_ref[cc * c:(cc + 1) * c, :].astype(BF16)
            o_ref[cc * c:(cc + 1) * c, cols] = softmax_out(_dot_nt(qb, kc), sink).astype(BF16)

        def block(n, carry):
            q_rows = pl.ds(pl.multiple_of(n_ctx + n * c, c), c)
            qb = q_ref[q_rows, :].astype(BF16)
            start = jnp.clip(n * c - c, 0, seq - win)
            k_rows = pl.ds(pl.multiple_of(n_ctx + start, c), win)
            rel = ii - jj + (n * c - start)
            s_loc = jnp.where(jnp.abs(rel) <= WINDOW, _dot_nt(qb, kb_scr[k_rows, :]), NEG_INF)
            o = softmax_out(_dot_nt(qb, kc), sink, s_loc, vb_scr[k_rows, :])
            o_ref[q_rows, cols] = o.astype(BF16)
            return carry

        lax.fori_loop(0, seq // c, block, 0)


def _attention(proj, att_sink, dims, col0):
    batch, n_ctx, seq = dims
    tb = n_ctx + seq
    qb, kb, vb = col0, col0 + ATT_Q_HEADS, col0 + ATT_Q_HEADS + ATT_KV_HEADS
    return pl.pallas_call(
        functools.partial(_att_kernel, n_ctx=n_ctx, seq=seq),
        out_shape=jax.ShapeDtypeStruct((batch * tb, ATT_Q_HEADS * LANES), BF16),
        grid=(batch, ATT_KV_HEADS),
        in_specs=[pl.BlockSpec(memory_space=pltpu.SMEM),
                  pl.BlockSpec((tb, LANES), lambda b, h: (b, qb + ATT_GROUP * h)),
                  pl.BlockSpec((tb, LANES), lambda b, h: (b, qb + ATT_GROUP * h + 1)),
                  pl.BlockSpec((tb, LANES), lambda b, h: (b, kb + h)),
                  pl.BlockSpec((tb, LANES), lambda b, h: (b, vb + h))],
        out_specs=pl.BlockSpec((tb, ATT_GROUP * LANES), lambda b, h: (b, h)),
        scratch_shapes=[pltpu.VMEM((tb, LANES), BF16), pltpu.VMEM((tb, LANES), BF16)],
        compiler_params=_cparams(2),
    )(att_sink, proj, proj, proj, proj)


HG_LEVELS = 7
HG_MXU_LEVELS = 3


def _hgrn_constants():
    c = CHUNK
    i = np.arange(c)[:, None]
    r = np.arange(c)[None, :]
    mats = [r <= i]
    for m in range(1, HG_MXU_LEVELS + 1):
        half = 1 << (m - 1)
        beta = (i // (2 * half)) * (2 * half) + half - 1
        upper = (i % (2 * half)) >= half
        mats.append((upper & (r > beta) & (r <= i)) | ((~upper) & (r > i) & (r <= beta)))
    fwd = [m.astype(np.float32) for m in mats]
    bwd = [m[::-1, ::-1] for m in fwd]
    stack = np.stack([np.concatenate(fwd, 0), np.concatenate(bwd, 0)])
    x = i ^ r
    level = np.where(r > i, -1, np.where(r == i, 0, np.floor(np.log2(np.maximum(x, 1))) + 1)).astype(np.int32)
    levels = np.stack([level, level[::-1, ::-1]])
    return stack, levels


def _hgrn_kernel(q_ref, zf_ref, zb_ref, v_ref, g_ref, lbl_ref, nw_ref, m_ref, lv_ref, o_ref,
                 of_scr, ob_scr, st_scr, *, layer, ctx_chunks, n_chunks):
    c = CHUNK
    z = lbl_ref[...]
    e = jnp.exp(z - jnp.max(z, axis=0, keepdims=True))
    sm = e / jnp.sum(e, axis=0, keepdims=True)
    lb = jnp.sum(sm[1:layer + 1], axis=0, keepdims=True)
    row = lax.broadcasted_iota(jnp.int32, (c, LANES), 0)
    upper = [[((row if d == 0 else c - 1 - row) & (1 << (m - 1))) != 0 for m in range(1, HG_MXU_LEVELS + 1)]
             for d in range(2)]

    def level_operand(q, k, a, half, d):
        pieces = []
        for base in range(0, c, 2 * half):
            lo, hi = slice(base, base + half), slice(base + half, base + 2 * half)
            if d == 0:
                ref = a[base + half - 1:base + half]
                pieces += [k[lo] * jnp.exp(ref - a[lo]), q[hi] * jnp.exp(a[hi] - ref)]
            else:
                ref = a[base + half:base + half + 1]
                pieces += [q[lo] * jnp.exp(a[lo] - ref), k[hi] * jnp.exp(ref - a[hi])]
        return jnp.concatenate(pieces, axis=0).astype(BF16)

    def chunk(n, d, z_ref):
        rows = pl.ds(pl.multiple_of(n * c, c), c)
        q = q_ref[rows, :]
        v = v_ref[rows, :]
        f = lb + (1.0 - lb) * jax.nn.sigmoid(z_ref[rows, :])
        g = jnp.log(f)
        k = 1.0 - f
        g_hi = g.astype(BF16)
        g_lo = (g - g_hi.astype(F32)).astype(BF16)
        res = _dot(m_ref[d], jnp.concatenate([g_hi, g_lo], axis=1))
        ex = res[:, :c] + res[:, c:]
        a = ex[0:c]
        st = st_scr[d]
        o = _dot_nt((q * jnp.exp(a)).astype(BF16), st.astype(BF16))
        lv = lv_ref[d]
        sc = jnp.where(lv == 0, _dot_nt(q.astype(BF16), k.astype(BF16)), 0.0)
        for m in range(1, HG_LEVELS + 1):
            if m <= HG_MXU_LEVELS:
                zz = (jnp.where(upper[d][m - 1], q, k) * jnp.exp(ex[m * c:(m + 1) * c])).astype(BF16)
            else:
                zz = level_operand(q, k, a, 1 << (m - 1), d)
            sc = jnp.where(lv == m, _dot_nt(zz, zz), sc)
        vb = v.astype(BF16)
        o = o + _dot(sc.astype(BF16), vb)
        a_last = a[c - 1:c] if d == 0 else a[0:1]
        kd = (k * jnp.exp(a_last - a)).astype(BF16)
        st_scr[d] = st * jnp.exp(a_last) + _dot_tn(vb, kd)
        return rows, o

    st_scr[...] = jnp.zeros_like(st_scr)

    def scan(t, carry):
        rows, o = chunk(t, 0, zf_ref)
        of_scr[rows, :] = o
        rows, o = chunk(_backward_chunk(t, ctx_chunks, n_chunks), 1, zb_ref)
        ob_scr[rows, :] = o
        return carry

    lax.fori_loop(0, n_chunks, scan, 0, unroll=2)

    def readout(n, carry):
        rows = pl.ds(pl.multiple_of(n * c, c), c)
        o = of_scr[rows, :] + ob_scr[rows, :]
        y = o * lax.rsqrt(jnp.mean(o * o, axis=-1, keepdims=True) + EPS) * nw_ref[...]
        o_ref[rows, :] = (y * _silu(g_ref[rows, :])).astype(BF16)
        return carry

    lax.fori_loop(0, n_chunks, readout, 0)


def _hgrn2(proj, hg_lb_logits, hg_norm_w, layer, dims):
    batch, n_ctx, seq = dims
    tb = n_ctx + seq
    depth = hg_lb_logits.shape[0]
    stack, levels = _hgrn_constants()
    spec = lambda off: pl.BlockSpec((tb, LANES), lambda b, h: (b, off + h))
    n_mats = 1 + HG_MXU_LEVELS
    return pl.pallas_call(
        functools.partial(_hgrn_kernel, layer=layer, ctx_chunks=n_ctx // CHUNK, n_chunks=tb // CHUNK),
        out_shape=jax.ShapeDtypeStruct((batch * tb, HG_HEADS * LANES), BF16),
        grid=(batch, HG_HEADS),
        in_specs=[spec(0), spec(HG_HEADS), spec(2 * HG_HEADS), spec(3 * HG_HEADS), spec(4 * HG_HEADS),
                  pl.BlockSpec((depth, LANES), lambda b, h: (0, h)),
                  pl.BlockSpec((1, LANES), lambda b, h: (0, 0)),
                  pl.BlockSpec((2, n_mats * CHUNK, CHUNK), lambda b, h: (0, 0, 0)),
                  pl.BlockSpec((2, CHUNK, CHUNK), lambda b, h: (0, 0, 0))],
        out_specs=pl.BlockSpec((tb, LANES), lambda b, h: (b, h)),
        scratch_shapes=[pltpu.VMEM((tb, LANES), F32), pltpu.VMEM((tb, LANES), F32),
                        pltpu.VMEM((2, LANES, LANES), F32)],
        compiler_params=_cparams(2),
    )(proj, proj, proj, proj, proj, hg_lb_logits, hg_norm_w.reshape(1, LANES),
      jnp.asarray(stack, BF16), jnp.asarray(levels, jnp.int32))


SUBLANES = 8


def _store_token_tiles(ref, x):
    rows, tiles = x.shape[0], x.shape[1] // LANES
    for s in range(tiles):
        ref[pl.ds(s, rows, stride=tiles), :] = x[:, s * LANES:(s + 1) * LANES]


def _load_token_tiles(ref, tiles):
    rows = ref.shape[0] // tiles
    return jnp.concatenate([ref[pl.ds(s, rows, stride=tiles), :] for s in range(tiles)], axis=1)


def _route(logits, tri, counts):
    lane = lax.broadcasted_iota(jnp.int32, logits.shape, 1).astype(F32)
    big = float(LANES)
    first = lambda hit: jnp.min(jnp.where(hit, lane, big), axis=-1, keepdims=True)
    is_g = lane < MOE_GROUPS
    gl = jnp.where(is_g, logits, NEG_INF)
    gmax = jnp.max(gl, axis=-1, keepdims=True)
    gsum = jnp.sum(jnp.where(is_g, jnp.exp(gl - gmax), 0.0), axis=-1, keepdims=True)
    g_p = 1.0 / gsum
    lo = MOE_GROUPS + first(gl == gmax) * MOE_EPG
    el = jnp.where((lane >= lo) & (lane < lo + MOE_EPG), logits, NEG_INF)
    e1v = jnp.max(el, axis=-1, keepdims=True)
    e1 = first(el == e1v)
    el2 = jnp.where(lane == e1, NEG_INF, el)
    e2v = jnp.max(el2, axis=-1, keepdims=True)
    e2 = first(el2 == e2v)
    t = jnp.exp(e2v - e1v)
    w1 = 1.0 / (1.0 + t)
    w2 = t * w1
    id1 = e1 - MOE_GROUPS
    id2 = e2 - MOE_GROUPS
    onehot = jnp.where((lane == id1) | (lane == id2), 1.0, 0.0)
    before = counts + _dot(tri, onehot.astype(BF16))
    rank1 = jnp.sum(jnp.where(lane == id1, before, 0.0), axis=-1, keepdims=True)
    rank2 = jnp.sum(jnp.where(lane == id2, before, 0.0), axis=-1, keepdims=True)
    slab = jnp.zeros_like(logits)
    for col, val in enumerate((id1, id2, g_p * w1, g_p * w2, rank1, rank2)):
        slab = jnp.where(lane == col, val, slab)
    return slab, counts + jnp.sum(onehot, axis=0, keepdims=True)


def _out_route_kernel(*refs, n_parts):
    a_refs = refs[:n_parts]
    (w_ref, x_ref, gate_ref, nw_ref, shift_ref, scale_ref, wr_ref, br_ref, tri_ref,
     xo_ref, h2_ref, route_ref, cnt_ref) = refs[n_parts:]

    @pl.when(pl.program_id(0) == 0)
    def _():
        cnt_ref[...] = jnp.zeros_like(cnt_ref)

    k0 = 0
    y = None
    for a_ref in a_refs:
        kk = a_ref.shape[1]
        t = _dot(a_ref[...], w_ref[k0:k0 + kk, :])
        y = t if y is None else y + t
        k0 += kk
    x = x_ref[...] + gate_ref[0] * y
    xo_ref[...] = x
    h2 = _norm_mod(x, nw_ref[...], scale_ref[0], shift_ref[0])
    _store_token_tiles(h2_ref, h2)
    h_hi = h2.astype(BF16)
    h_lo = (h2 - h_hi.astype(F32)).astype(BF16)
    both = _dot(h_hi, wr_ref[...])
    logits = both[:, :LANES] + both[:, LANES:] + _dot(h_lo, wr_ref[:, :LANES]) + br_ref[...]
    slab, counts = _route(logits, tri_ref[...], cnt_ref[0:1, :])
    route_ref[...] = slab
    cnt_ref[...] = jnp.broadcast_to(counts, cnt_ref.shape)


def _out_projection_route(parts, w_bf, xs, nw, mods, wr, br, dims):
    nt, d = xs.shape
    batch, n_ctx, seq = dims
    tpb = (n_ctx + seq) // ROW_TILE
    ctx_tiles = n_ctx // ROW_TILE
    row = lambda width: pl.BlockSpec((ROW_TILE, width), lambda i: (i, 0))
    full = lambda shape: pl.BlockSpec(shape, lambda i: (0,) * len(shape))
    mod = lambda chunk_idx: _mod_spec(chunk_idx, d, tpb, ctx_tiles, batch)
    tri = jnp.asarray(np.tril(np.ones((ROW_TILE, ROW_TILE), np.float32), -1), BF16)
    return pl.pallas_call(
        functools.partial(_out_route_kernel, n_parts=len(parts)),
        out_shape=(jax.ShapeDtypeStruct((nt, d), F32), jax.ShapeDtypeStruct((nt * d // LANES, LANES), F32),
                   jax.ShapeDtypeStruct((nt, LANES), F32), jax.ShapeDtypeStruct((8, LANES), F32)),
        grid=(nt // ROW_TILE,),
        in_specs=[row(p.shape[1]) for p in parts] + [
            full(w_bf.shape), row(d), mod(2), full((1, d)), mod(3), mod(4), full(wr.shape), full(br.shape),
            full(tri.shape)],
        out_specs=(row(d), pl.BlockSpec((ROW_TILE * d // LANES, LANES), lambda i: (i, 0)), row(LANES),
                   full((8, LANES))),
        compiler_params=_cparams(1),
    )(*parts, w_bf, xs, mods, nw.reshape(1, d), mods, mods, wr, br, tri)


def _dispatch_plan(route, counts_f, n_tok):
    n_assign = n_tok * MOE_TOP_K
    flat_e = route[:, 0:MOE_TOP_K].astype(jnp.int32).reshape(-1)
    rank = route[:, 4:4 + MOE_TOP_K].astype(jnp.int32).reshape(-1)
    counts = counts_f[0, :MOE_EXPERTS].astype(jnp.int32)
    padded = (counts + MOE_BLOCK - 1) // MOE_BLOCK * MOE_BLOCK
    pad_end = jnp.cumsum(padded)
    pad_start = pad_end - padded
    dest = pad_start[flat_e] + rank
    n_blocks = -(-n_assign // MOE_BLOCK) + MOE_EXPERTS
    n_slots = n_blocks * MOE_BLOCK
    slot_assign = jnp.full((n_slots,), -1, jnp.int32).at[dest].set(jnp.arange(n_assign, dtype=jnp.int32))
    block_start = jnp.arange(n_blocks, dtype=jnp.int32) * MOE_BLOCK
    block_expert = jnp.minimum(jnp.sum((pad_end[None, :] <= block_start[:, None]).astype(jnp.int32), axis=1),
                               MOE_EXPERTS - 1)
    slot = jnp.arange(-MOE_BLOCK, n_slots + MOE_BLOCK, dtype=jnp.int32)
    spare = n_assign + ((slot // MOE_BLOCK) % 2) * MOE_BLOCK + slot % MOE_BLOCK
    pad = jnp.full((MOE_BLOCK,), -1, jnp.int32)
    assign = jnp.concatenate([pad, slot_assign, pad])
    src = jnp.where(assign >= 0, assign // MOE_TOP_K, 0)
    dst = jnp.where(assign >= 0, (assign % MOE_TOP_K) * n_tok + assign // MOE_TOP_K, spare)
    return block_expert, src[MOE_BLOCK:], dst[:n_slots + MOE_BLOCK], n_blocks


def _expert_kernel(be_ref, src_ref, dst_ref, h_hbm, w1_ref, w3_ref, w2_ref, out_hbm,
                   xbuf, ybuf, w1b, w3b, w2b, gsem, ssem):
    i = pl.program_id(0)
    last = pl.num_programs(0) - 1
    cur = i % 2
    nxt = 1 - cur

    tiles = xbuf.shape[1] // MOE_BLOCK

    def token(idx):
        return pl.ds(pl.multiple_of(idx, tiles), tiles)

    def start_gather(blk, buf):
        for r in range(MOE_BLOCK):
            pltpu.make_async_copy(h_hbm.at[token(src_ref[blk * MOE_BLOCK + r])],
                                  xbuf.at[buf, pl.ds(r * tiles, tiles)], gsem.at[buf]).start()

    def start_scatter(blk, buf):
        for r in range(MOE_BLOCK):
            pltpu.make_async_copy(ybuf.at[buf, pl.ds(r * tiles, tiles)],
                                  out_hbm.at[token(dst_ref[(blk + 1) * MOE_BLOCK + r])], ssem.at[buf]).start()

    def wait_gather(buf):
        pltpu.make_async_copy(h_hbm.at[pl.ds(0, MOE_BLOCK * tiles)], xbuf.at[buf], gsem.at[buf]).wait()

    def wait_scatter(buf):
        pltpu.make_async_copy(ybuf.at[buf], out_hbm.at[pl.ds(0, MOE_BLOCK * tiles)], ssem.at[buf]).wait()

    @pl.when(i == 0)
    def _():
        ybuf[...] = jnp.zeros_like(ybuf)
        start_gather(0, 0)

    @pl.when((i == 0) | (be_ref[i] != be_ref[jnp.maximum(i - 1, 0)]))
    def _():
        w1b[...] = w1_ref[0, 0].astype(BF16)
        w3b[...] = w3_ref[0, 0].astype(BF16)
        w2b[...] = w2_ref[0, 0].astype(BF16)

    wait_gather(cur)

    @pl.when(i >= 1)
    def _():
        wait_scatter(cur)

    start_gather(i + 1, nxt)
    start_scatter(i - 1, nxt)
    xb = _load_token_tiles(xbuf.at[cur], tiles).astype(BF16)
    act = _silu(_dot(xb, w1b[...])) * _dot(xb, w3b[...])
    _store_token_tiles(ybuf.at[cur], _dot(act.astype(BF16), w2b[...]))

    @pl.when(i == last)
    def _():
        start_scatter(i, cur)
        wait_scatter(nxt)
        wait_scatter(cur)
        wait_gather(nxt)


def _experts(h2, plan, w1, w3, w2, layer):
    block_expert, slot_src, slot_dst, n_blocks = plan
    d, ff = w1.shape[2], w1.shape[3]
    tiles = d // LANES
    nt = h2.shape[0] // tiles
    wspec = lambda shape: pl.BlockSpec((1, 1) + shape, lambda i, be, sr, ds: (layer, be[i], 0, 0))
    grid_spec = pltpu.PrefetchScalarGridSpec(
        num_scalar_prefetch=3,
        grid=(n_blocks,),
        in_specs=[pl.BlockSpec(memory_space=pl.ANY), wspec((d, ff)), wspec((d, ff)), wspec((ff, d))],
        out_specs=pl.BlockSpec(memory_space=pl.ANY),
        scratch_shapes=[pltpu.VMEM((2, MOE_BLOCK * tiles, LANES), F32), pltpu.VMEM((2, MOE_BLOCK * tiles, LANES), F32),
                        pltpu.VMEM((d, ff), BF16), pltpu.VMEM((d, ff), BF16), pltpu.VMEM((ff, d), BF16),
                        pltpu.SemaphoreType.DMA((2,)), pltpu.SemaphoreType.DMA((2,))])
    return pl.pallas_call(
        _expert_kernel,
        out_shape=jax.ShapeDtypeStruct(((nt * MOE_TOP_K + 2 * MOE_BLOCK) * tiles, LANES), F32),
        grid_spec=grid_spec,
        compiler_params=_cparams(1, has_side_effects=True, disable_bounds_checks=True),
    )(block_expert, slot_src * tiles, slot_dst * tiles, h2, w1, w3, w2)


def _combine_kernel(x_ref, y0_ref, y1_ref, route_ref, gate_ref, *rest, final):
    r = route_ref[...]
    tiles = x_ref.shape[1] // LANES
    y = r[:, 2:3] * _load_token_tiles(y0_ref, tiles) + r[:, 3:4] * _load_token_tiles(y1_ref, tiles)
    x = x_ref[...] + gate_ref[0] * y
    if final:
        fnw_ref, o_ref = rest
        x = x * lax.rsqrt(jnp.mean(x * x, axis=-1, keepdims=True) + EPS) * fnw_ref[...]
    else:
        (o_ref,) = rest
    o_ref[...] = x


def _combine(xs, y2, route, mods, dims, final_norm_w=None):
    nt, d = xs.shape
    batch, n_ctx, seq = dims
    tpb = (n_ctx + seq) // ROW_TILE
    ctx_tiles = n_ctx // ROW_TILE
    row = lambda width: pl.BlockSpec((ROW_TILE, width), lambda i: (i, 0))
    final = final_norm_w is not None
    first = pl.BlockSpec((ROW_TILE * d // LANES, LANES), lambda i: (i, 0))
    second = pl.BlockSpec((ROW_TILE * d // LANES, LANES), lambda i: (i + nt // ROW_TILE, 0))
    in_specs = [row(d), first, second, row(LANES), _mod_spec(5, d, tpb, ctx_tiles, batch)]
    args = [xs, y2, y2, route, mods]
    if final:
        in_specs.append(pl.BlockSpec((1, d), lambda i: (0, 0)))
        args.append(final_norm_w.reshape(1, d))
    return pl.pallas_call(
        functools.partial(_combine_kernel, final=final),
        out_shape=jax.ShapeDtypeStruct((nt, d), F32),
        grid=(nt // ROW_TILE,),
        in_specs=in_specs,
        out_specs=row(d),
        compiler_params=_cparams(1),
    )(*args)


def _even_col_ops():
    rh = RET_HEADS
    ops = [(True, 1.0, 1.0)] * rh + [(True, float(HEAD_DIM) ** -0.5, 1.0)] * rh + [(False, 1.0, 1.0)] * (2 * rh)
    ops += [(True, 1.0, float(HEAD_DIM) ** -0.5)] * ATT_Q_HEADS + [(True, 1.0, 1.0)] * ATT_KV_HEADS
    ops += [(False, 1.0, 1.0)] * ATT_KV_HEADS
    return tuple(ops)


def kernel(x, c, ctx, c_ctx, ada_w, ada_b, norm_w, final_norm_w, ev_w_in, ev_w_out, ret_decay_raw, att_sink,
           od_w_in, od_w_out, hg_lb_logits, hg_norm_w, moe_wg, moe_bg, moe_we, moe_be, moe_w1, moe_w3, moe_w2):
    batch, seq, d = x.shape
    n_ctx = ctx.shape[1]
    depth = ada_w.shape[0]
    assert batch + 1 <= MOD_ROWS and n_ctx % ROW_TILE == 0 and seq % ROW_TILE == 0
    assert seq % GRID_W == 0 and seq >= 3 * CHUNK
    dims = (batch, n_ctx, seq)
    tb = n_ctx + seq
    nt = batch * tb

    xs = jnp.concatenate([ctx, x], axis=1).reshape(nt, d)
    cvec = jnp.concatenate([c, c_ctx[None, :], jnp.zeros((MOD_ROWS - batch - 1, d), F32)], axis=0)
    mods_all = _ada_modulation(cvec, ada_w, ada_b)
    rope = _rope_tables(n_ctx, seq)
    even_ops = _even_col_ops()

    for l in range(depth):
        p = l // 2
        mods = mods_all[l].reshape(MOD_ROWS, 1, 6 * d)
        if l % 2 == 0:
            proj = _in_projection(xs, norm_w[l, 0], mods, ev_w_in[p].astype(BF16), dims, even_ops, rope)
            ret = _retention(proj, ret_decay_raw[p], dims)
            att = _attention(proj, att_sink[p], dims, 4 * RET_HEADS)
            parts, w_out = [ret, att], ev_w_out[p]
        else:
            proj = _in_projection(xs, norm_w[l, 0], mods, od_w_in[p].astype(BF16), dims)
            parts, w_out = [_hgrn2(proj, hg_lb_logits, hg_norm_w[p], l, dims)], od_w_out[p]
        n_logit = MOE_GROUPS + MOE_EXPERTS
        wr = jnp.concatenate([moe_wg[l], jnp.moveaxis(moe_we[l], 0, 1).reshape(d, MOE_EXPERTS),
                              jnp.zeros((d, LANES - n_logit), F32)], axis=1)
        wr_hi = wr.astype(BF16)
        wr = jnp.concatenate([wr_hi, (wr - wr_hi.astype(F32)).astype(BF16)], axis=1)
        br = jnp.concatenate([moe_bg[l], moe_be[l].reshape(-1), jnp.zeros((LANES - n_logit,), F32)])[None, :]
        xs, h2, route, counts = _out_projection_route(
            parts, w_out.astype(BF16), xs, norm_w[l, 1], mods, wr, br, dims)
        plan = _dispatch_plan(route, counts, nt)
        y2 = _experts(h2, plan, moe_w1, moe_w3, moe_w2, l)
        xs = _combine(xs, y2, route, mods, dims, final_norm_w if l == depth - 1 else None)

    return xs.reshape(batch, tb, d)[:, n_ctx:]
```

```python
import functools

import numpy as np
import jax
import jax.numpy as jnp
from jax import lax
from jax.experimental import pallas as pl
from jax.experimental.pallas import tpu as pltpu

F32 = jnp.float32
BF16 = jnp.bfloat16
HIGHEST = lax.Precision.HIGHEST

EPS = 1e-6
NEG_INF = -1e30
LANES = 128
GRID_W = 64
ROPE_THETA = 10000.0
HEAD_DIM = 128
RET_HEADS = 4
ATT_Q_HEADS = 4
ATT_KV_HEADS = 2
ATT_GROUP = ATT_Q_HEADS // ATT_KV_HEADS
WINDOW = 128
HG_HEADS = 8
MOE_GROUPS = 4
MOE_EPG = 8
MOE_EXPERTS = MOE_GROUPS * MOE_EPG
MOE_TOP_K = 2
MOE_BLOCK = 128
CHUNK = 128
ROW_TILE = 256
MOD_ROWS = 8
VMEM_LIMIT = 56 * 1024 * 1024


def _cparams(n_axes, **kw):
    return pltpu.CompilerParams(dimension_semantics=("arbitrary",) * n_axes,
                                vmem_limit_bytes=VMEM_LIMIT, **kw)


def _dot(a, b):
    return jnp.dot(a, b, preferred_element_type=F32)


def _dot_nt(a, b):
    return lax.dot_general(a, b, (((1,), (1,)), ((), ())), preferred_element_type=F32)


def _dot_tn(a, b):
    return lax.dot_general(a, b, (((0,), (0,)), ((), ())), preferred_element_type=F32)


def _silu(x):
    return x * jax.nn.sigmoid(x)


def _norm_mod(x, nw, scale, shift):
    ms = jnp.mean(x * x, axis=-1, keepdims=True)
    y = x * lax.rsqrt(ms + EPS) * nw
    return y * (1.0 + scale) + shift


def _ada_kernel(c_ref, w_ref, b_ref, o_ref):
    s = _silu(c_ref[...])
    o_ref[0] = jnp.dot(s, w_ref[0], precision=HIGHEST, preferred_element_type=F32) + b_ref[0]


def _ada_modulation(cvec, ada_w, ada_b):
    depth, d, n6 = ada_w.shape
    tn = 1536
    return pl.pallas_call(
        _ada_kernel,
        out_shape=jax.ShapeDtypeStruct((depth, MOD_ROWS, n6), F32),
        grid=(depth, n6 // tn),
        in_specs=[pl.BlockSpec((MOD_ROWS, d), lambda l, j: (0, 0)),
                  pl.BlockSpec((1, d, tn), lambda l, j: (l, 0, j)),
                  pl.BlockSpec((1, 1, tn), lambda l, j: (l, 0, j))],
        out_specs=pl.BlockSpec((1, MOD_ROWS, tn), lambda l, j: (l, 0, j)),
        compiler_params=_cparams(2),
    )(cvec, ada_w, ada_b.reshape(depth, 1, n6))


def _swap_halves(x):
    lane = lax.broadcasted_iota(jnp.int32, x.shape, 1)
    return jnp.where((lane % 64) < 32, pltpu.roll(x, 96, 1), pltpu.roll(x, 32, 1))


def _proj_kernel(x_ref, nw_ref, shift_ref, scale_ref, w_ref, *rest, col_ops, chunk):
    if col_ops is None:
        (o_ref,) = rest
    else:
        cos_ref, sin_ref, o_ref = rest
    h = _norm_mod(x_ref[...], nw_ref[...], scale_ref[0], shift_ref[0]).astype(BF16)
    nout = o_ref.shape[1]
    for c0 in range(0, nout, chunk):
        acc = _dot(h, w_ref[:, c0:c0 + chunk])
        for hd in range(chunk // LANES):
            col = c0 + hd * LANES
            a = acc[:, hd * LANES:(hd + 1) * LANES]
            if col_ops is not None:
                rope, pre, post = col_ops[col // LANES]
                if pre != 1.0:
                    a = a * pre
                if rope:
                    a = a * cos_ref[...] + _swap_halves(a) * sin_ref[...]
                if post != 1.0:
                    a = a * post
            o_ref[:, col:col + LANES] = a


def _tile_mod_row(i, tiles_per_batch, ctx_tiles, batch):
    return jnp.where(i % tiles_per_batch < ctx_tiles, batch, i // tiles_per_batch)


def _mod_spec(chunk_idx, d, tiles_per_batch, ctx_tiles, batch):
    return pl.BlockSpec(
        (1, 1, d), lambda i: (_tile_mod_row(i, tiles_per_batch, ctx_tiles, batch), 0, chunk_idx))


def _in_projection(xs, nw, mods, w_bf, dims, col_ops=None, rope=None):
    nt, d = xs.shape
    nout = w_bf.shape[1]
    batch, n_ctx, seq = dims
    tpb = (n_ctx + seq) // ROW_TILE
    ctx_tiles = n_ctx // ROW_TILE
    in_specs = [pl.BlockSpec((ROW_TILE, d), lambda i: (i, 0)),
                pl.BlockSpec((1, d), lambda i: (0, 0)),
                _mod_spec(0, d, tpb, ctx_tiles, batch),
                _mod_spec(1, d, tpb, ctx_tiles, batch),
                pl.BlockSpec((d, nout), lambda i: (0, 0))]
    args = [xs, nw.reshape(1, d), mods, mods, w_bf]
    if col_ops is not None:
        rope_spec = pl.BlockSpec((ROW_TILE, LANES), lambda i: (i % tpb, 0))
        in_specs += [rope_spec, rope_spec]
        args += [rope[0], rope[1]]
    return pl.pallas_call(
        functools.partial(_proj_kernel, col_ops=col_ops, chunk=512),
        out_shape=jax.ShapeDtypeStruct((nt, nout), F32),
        grid=(nt // ROW_TILE,),
        in_specs=in_specs,
        out_specs=pl.BlockSpec((ROW_TILE, nout), lambda i: (i, 0)),
        compiler_params=_cparams(1),
    )(*args)


def _rope_tables(n_ctx, seq):
    n_rows = seq // GRID_W
    row = jnp.repeat(jnp.arange(n_rows, dtype=F32), GRID_W)
    col = jnp.tile(jnp.arange(GRID_W, dtype=F32), n_rows)
    axis_dim = HEAD_DIM // 2
    inv_freq = ROPE_THETA ** (-jnp.arange(0, axis_dim, 2, dtype=F32) / axis_dim)
    ang_r = row[:, None] * inv_freq[None, :]
    ang_c = col[:, None] * inv_freq[None, :]
    cos = jnp.concatenate([jnp.cos(ang_r), jnp.cos(ang_r), jnp.cos(ang_c), jnp.cos(ang_c)], axis=-1)
    sin = jnp.concatenate([-jnp.sin(ang_r), jnp.sin(ang_r), -jnp.sin(ang_c), jnp.sin(ang_c)], axis=-1)
    cos = jnp.concatenate([jnp.ones((n_ctx, HEAD_DIM), F32), cos], axis=0)
    sin = jnp.concatenate([jnp.zeros((n_ctx, HEAD_DIM), F32), sin], axis=0)
    return cos, sin


def _backward_chunk(t, ctx_chunks, n_chunks):
    return jnp.where(t < ctx_chunks, ctx_chunks - 1 - t, n_chunks - 1 - (t - ctx_chunks))


def _ret_kernel(raw_ref, q_ref, k_ref, v_ref, g_ref, o_ref, of_scr, ob_scr, s_scr, dec_scr, qd_scr, kd_scr,
                *, ctx_chunks, n_chunks):
    h = pl.program_id(1)
    c = CHUNK
    ii = lax.broadcasted_iota(jnp.int32, (c, c), 0).astype(F32)
    jj = lax.broadcasted_iota(jnp.int32, (c, c), 1).astype(F32)
    lg = []
    for d in range(2):
        lgd = -jnp.exp(jnp.full((c, c), raw_ref[d, h], F32))
        lg.append(lgd)
        rel = (ii - jj) if d == 0 else (jj - ii)
        dec_scr[d] = jnp.where(rel >= 0, jnp.exp(jnp.maximum(rel, 0.0) * lgd), 0.0)
        qd_scr[d] = jnp.exp(((ii + 1.0) if d == 0 else (c - ii)) * lgd)
        kd_scr[d] = jnp.exp(((c - 1.0 - ii) if d == 0 else ii) * lgd)

    def chunk(n, d):
        rows = pl.ds(pl.multiple_of(n * c, c), c)
        q = q_ref[rows, :]
        k = k_ref[rows, :]
        vb = v_ref[rows, :].astype(BF16)
        sc = _dot_nt(q.astype(BF16), k.astype(BF16)) * dec_scr[d]
        s = s_scr[d]
        o = _dot(sc.astype(BF16), vb) + _dot((q * qd_scr[d]).astype(BF16), s.astype(BF16))
        chunk_decay = jnp.exp(float(c) * lg[d][0:1, :])
        s_scr[d] = chunk_decay * s + _dot_tn((k * kd_scr[d]).astype(BF16), vb)
        return rows, o

    s_scr[...] = jnp.zeros_like(s_scr)

    def scan(t, carry):
        rows, o = chunk(t, 0)
        of_scr[rows, :] = o
        rows, o = chunk(_backward_chunk(t, ctx_chunks, n_chunks), 1)
        ob_scr[rows, :] = o
        return carry

    lax.fori_loop(0, n_chunks, scan, 0, unroll=2)

    def readout(n, carry):
        rows = pl.ds(pl.multiple_of(n * c, c), c)
        o = of_scr[rows, :] + ob_scr[rows, :]
        y = o * lax.rsqrt(jnp.mean(o * o, axis=-1, keepdims=True) + EPS)
        o_ref[rows, :] = (_silu(g_ref[rows, :]) * y).astype(BF16)
        return carry

    lax.fori_loop(0, n_chunks, readout, 0)


def _retention(proj, ret_decay_raw, dims):
    batch, n_ctx, seq = dims
    tb = n_ctx + seq
    spec = lambda off: pl.BlockSpec((tb, LANES), lambda b, h: (b, off + h))
    return pl.pallas_call(
        functools.partial(_ret_kernel, ctx_chunks=n_ctx // CHUNK, n_chunks=tb // CHUNK),
        out_shape=jax.ShapeDtypeStruct((batch * tb, RET_HEADS * LANES), BF16),
        grid=(batch, RET_HEADS),
        in_specs=[pl.BlockSpec(memory_space=pltpu.SMEM),
                  spec(0), spec(RET_HEADS), spec(2 * RET_HEADS), spec(3 * RET_HEADS)],
        out_specs=pl.BlockSpec((tb, LANES), lambda b, h: (b, h)),
        scratch_shapes=[pltpu.VMEM((tb, LANES), F32), pltpu.VMEM((tb, LANES), F32),
                        pltpu.VMEM((2, LANES, LANES), F32),
                        pltpu.VMEM((2, CHUNK, CHUNK), F32), pltpu.VMEM((2, CHUNK, LANES), F32),
                        pltpu.VMEM((2, CHUNK, LANES), F32)],
        compiler_params=_cparams(2),
    )(ret_decay_raw, proj, proj, proj, proj)


def _att_kernel(sink_ref, q0_ref, q1_ref, k_ref, v_ref, o_ref, kb_scr, vb_scr, bias_scr, *, n_ctx, seq):
    hk = pl.program_id(1)
    c = CHUNK
    win = 3 * c
    kb_scr[...] = k_ref[...].astype(BF16)
    vb_scr[...] = v_ref[...].astype(BF16)
    kc = kb_scr[0:n_ctx, :]
    vc = vb_scr[0:n_ctx, :]
    ii = lax.broadcasted_iota(jnp.int32, (c, win), 0)
    jj = lax.broadcasted_iota(jnp.int32, (c, win), 1)
    for off in range(3):
        bias_scr[off] = jnp.where(jnp.abs(ii - jj + off * c) <= WINDOW, 0.0, NEG_INF)

    def softmax_out(s_ctx, sink, s_loc=None, v_loc=None):
        m = jnp.maximum(jnp.max(s_ctx, axis=-1, keepdims=True), sink)
        if s_loc is not None:
            m = jnp.maximum(m, jnp.max(s_loc, axis=-1, keepdims=True))
        p_ctx = jnp.exp(s_ctx - m)
        den = jnp.sum(p_ctx, axis=-1, keepdims=True) + jnp.exp(sink - m)
        o = _dot(p_ctx.astype(BF16), vc)
        if s_loc is not None:
            p_loc = jnp.exp(s_loc - m)
            den = den + jnp.sum(p_loc, axis=-1, keepdims=True)
            o = o + _dot(p_loc.astype(BF16), v_loc)
        return o / den

    heads = []
    for g, q_ref in enumerate((q0_ref, q1_ref)):
        sink = jnp.full((c, 1), sink_ref[hk * ATT_GROUP + g], F32)
        heads.append((q_ref, sink, slice(g * LANES, (g + 1) * LANES)))

    for q_ref, sink, cols in heads:
        for cc in range(n_ctx // c):
            qb = q_ref[cc * c:(cc + 1) * c, :].astype(BF16)
            o_ref[cc * c:(cc + 1) * c, cols] = softmax_out(_dot_nt(qb, kc), sink).astype(BF16)

    def block(n, carry):
        q_rows = pl.ds(pl.multiple_of(n_ctx + n * c, c), c)
        start = jnp.clip(n * c - c, 0, seq - win)
        k_rows = pl.ds(pl.multiple_of(n_ctx + start, c), win)
        k_loc = kb_scr[k_rows, :]
        v_loc = vb_scr[k_rows, :]
        bias = bias_scr[(n * c - start) // c]
        for q_ref, sink, cols in heads:
            qb = q_ref[q_rows, :].astype(BF16)
            o = softmax_out(_dot_nt(qb, kc), sink, _dot_nt(qb, k_loc) + bias, v_loc)
            o_ref[q_rows, cols] = o.astype(BF16)
        return carry

    lax.fori_loop(0, seq // c, block, 0)


def _attention(proj, att_sink, dims, col0):
    batch, n_ctx, seq = dims
    tb = n_ctx + seq
    qb, kb, vb = col0, col0 + ATT_Q_HEADS, col0 + ATT_Q_HEADS + ATT_KV_HEADS
    return pl.pallas_call(
        functools.partial(_att_kernel, n_ctx=n_ctx, seq=seq),
        out_shape=jax.ShapeDtypeStruct((batch * tb, ATT_Q_HEADS * LANES), BF16),
        grid=(batch, ATT_KV_HEADS),
        in_specs=[pl.BlockSpec(memory_space=pltpu.SMEM),
                  pl.BlockSpec((tb, LANES), lambda b, h: (b, qb + ATT_GROUP * h)),
                  pl.BlockSpec((tb, LANES), lambda b, h: (b, qb + ATT_GROUP * h + 1)),
                  pl.BlockSpec((tb, LANES), lambda b, h: (b, kb + h)),
                  pl.BlockSpec((tb, LANES), lambda b, h: (b, vb + h))],
        out_specs=pl.BlockSpec((tb, ATT_GROUP * LANES), lambda b, h: (b, h)),
        scratch_shapes=[pltpu.VMEM((tb, LANES), BF16), pltpu.VMEM((tb, LANES), BF16),
                        pltpu.VMEM((3, CHUNK, 3 * CHUNK), F32)],
        compiler_params=_cparams(2),
    )(att_sink, proj, proj, proj, proj)


HG_LEVELS = 7
HG_MXU_LEVELS = 3


def _hgrn_constants():
    c = CHUNK
    i = np.arange(c)[:, None]
    r = np.arange(c)[None, :]
    mats = [r <= i]
    for m in range(1, HG_MXU_LEVELS + 1):
        half = 1 << (m - 1)
        beta = (i // (2 * half)) * (2 * half) + half - 1
        upper = (i % (2 * half)) >= half
        mats.append((upper & (r > beta) & (r <= i)) | ((~upper) & (r > i) & (r <= beta)))
    fwd = [m.astype(np.float32) for m in mats]
    bwd = [m[::-1, ::-1] for m in fwd]
    stack = np.stack([np.concatenate(fwd, 0), np.concatenate(bwd, 0)])
    x = i ^ r
    level = np.where(r > i, -1, np.where(r == i, 0, np.floor(np.log2(np.maximum(x, 1))) + 1)).astype(np.int32)
    levels = np.stack([level, level[::-1, ::-1]])
    return stack, levels


def _hgrn_kernel(q_ref, zf_ref, zb_ref, v_ref, g_ref, lbl_ref, nw_ref, m_ref, lv_ref, o_ref,
                 of_scr, ob_scr, st_scr, *, layer, ctx_chunks, n_chunks):
    c = CHUNK
    z = lbl_ref[...]
    e = jnp.exp(z - jnp.max(z, axis=0, keepdims=True))
    sm = e / jnp.sum(e, axis=0, keepdims=True)
    lb = jnp.sum(sm[1:layer + 1], axis=0, keepdims=True)
    row = lax.broadcasted_iota(jnp.int32, (c, LANES), 0)
    upper = [[((row if d == 0 else c - 1 - row) & (1 << (m - 1))) != 0 for m in range(1, HG_MXU_LEVELS + 1)]
             for d in range(2)]

    def level_operand(q, k, a, half, d):
        pieces = []
        for base in range(0, c, 2 * half):
            lo, hi = slice(base, base + half), slice(base + half, base + 2 * half)
            if d == 0:
                ref = a[base + half - 1:base + half]
                pieces += [k[lo] * jnp.exp(ref - a[lo]), q[hi] * jnp.exp(a[hi] - ref)]
            else:
                ref = a[base + half:base + half + 1]
                pieces += [q[lo] * jnp.exp(a[lo] - ref), k[hi] * jnp.exp(ref - a[hi])]
        return jnp.concatenate(pieces, axis=0).astype(BF16)

    def chunk(n, d, z_ref):
        rows = pl.ds(pl.multiple_of(n * c, c), c)
        q = q_ref[rows, :]
        v = v_ref[rows, :]
        f = lb + (1.0 - lb) * jax.nn.sigmoid(z_ref[rows, :])
        g = jnp.log(f)
        k = 1.0 - f
        g_hi = g.astype(BF16)
        g_lo = (g - g_hi.astype(F32)).astype(BF16)
        res = _dot(m_ref[d], jnp.concatenate([g_hi, g_lo], axis=1))
        ex = res[:, :c] + res[:, c:]
        a = ex[0:c]
        st = st_scr[d]
        o = _dot_nt((q * jnp.exp(a)).astype(BF16), st.astype(BF16))
        lv = lv_ref[d]
        sc = jnp.where(lv == 0, _dot_nt(q.astype(BF16), k.astype(BF16)), 0.0)
        for m in range(1, HG_LEVELS + 1):
            if m <= HG_MXU_LEVELS:
                zz = (jnp.where(upper[d][m - 1], q, k) * jnp.exp(ex[m * c:(m + 1) * c])).astype(BF16)
            else:
                zz = level_operand(q, k, a, 1 << (m - 1), d)
            sc = jnp.where(lv == m, _dot_nt(zz, zz), sc)
        vb = v.astype(BF16)
        o = o + _dot(sc.astype(BF16), vb)
        a_last = a[c - 1:c] if d == 0 else a[0:1]
        kd = (k * jnp.exp(a_last - a)).astype(BF16)
        st_scr[d] = st * jnp.exp(a_last) + _dot_tn(vb, kd)
        return rows, o

    st_scr[...] = jnp.zeros_like(st_scr)

    def scan(t, carry):
        rows, o = chunk(t, 0, zf_ref)
        of_scr[rows, :] = o
        rows, o = chunk(_backward_chunk(t, ctx_chunks, n_chunks), 1, zb_ref)
        ob_scr[rows, :] = o
        return carry

    lax.fori_loop(0, n_chunks, scan, 0, unroll=2)

    def readout(n, carry):
        rows = pl.ds(pl.multiple_of(n * c, c), c)
        o = of_scr[rows, :] + ob_scr[rows, :]
        y = o * lax.rsqrt(jnp.mean(o * o, axis=-1, keepdims=True) + EPS) * nw_ref[...]
        o_ref[rows, :] = (y * _silu(g_ref[rows, :])).astype(BF16)
        return carry

    lax.fori_loop(0, n_chunks, readout, 0)


def _hgrn2(proj, hg_lb_logits, hg_norm_w, layer, dims):
    batch, n_ctx, seq = dims
    tb = n_ctx + seq
    depth = hg_lb_logits.shape[0]
    stack, levels = _hgrn_constants()
    spec = lambda off: pl.BlockSpec((tb, LANES), lambda b, h: (b, off + h))
    n_mats = 1 + HG_MXU_LEVELS
    return pl.pallas_call(
        functools.partial(_hgrn_kernel, layer=layer, ctx_chunks=n_ctx // CHUNK, n_chunks=tb // CHUNK),
        out_shape=jax.ShapeDtypeStruct((batch * tb, HG_HEADS * LANES), BF16),
        grid=(batch, HG_HEADS),
        in_specs=[spec(0), spec(HG_HEADS), spec(2 * HG_HEADS), spec(3 * HG_HEADS), spec(4 * HG_HEADS),
                  pl.BlockSpec((depth, LANES), lambda b, h: (0, h)),
                  pl.BlockSpec((1, LANES), lambda b, h: (0, 0)),
                  pl.BlockSpec((2, n_mats * CHUNK, CHUNK), lambda b, h: (0, 0, 0)),
                  pl.BlockSpec((2, CHUNK, CHUNK), lambda b, h: (0, 0, 0))],
        out_specs=pl.BlockSpec((tb, LANES), lambda b, h: (b, h)),
        scratch_shapes=[pltpu.VMEM((tb, LANES), F32), pltpu.VMEM((tb, LANES), F32),
                        pltpu.VMEM((2, LANES, LANES), F32)],
        compiler_params=_cparams(2),
    )(proj, proj, proj, proj, proj, hg_lb_logits, hg_norm_w.reshape(1, LANES),
      jnp.asarray(stack, BF16), jnp.asarray(levels, jnp.int32))


SUBLANES = 8


def _store_token_tiles(ref, x):
    rows, tiles = x.shape[0], x.shape[1] // LANES
    for s in range(tiles):
        ref[pl.ds(s, rows, stride=tiles), :] = x[:, s * LANES:(s + 1) * LANES]


def _load_token_tiles(ref, tiles):
    rows = ref.shape[0] // tiles
    return jnp.concatenate([ref[pl.ds(s, rows, stride=tiles), :] for s in range(tiles)], axis=1)


def _route(logits, tri, counts):
    lane = lax.broadcasted_iota(jnp.int32, logits.shape, 1).astype(F32)
    big = float(LANES)
    first = lambda hit: jnp.min(jnp.where(hit, lane, big), axis=-1, keepdims=True)
    is_g = lane < MOE_GROUPS
    gl = jnp.where(is_g, logits, NEG_INF)
    gmax = jnp.max(gl, axis=-1, keepdims=True)
    gsum = jnp.sum(jnp.where(is_g, jnp.exp(gl - gmax), 0.0), axis=-1, keepdims=True)
    g_p = 1.0 / gsum
    lo = MOE_GROUPS + first(gl == gmax) * MOE_EPG
    el = jnp.where((lane >= lo) & (lane < lo + MOE_EPG), logits, NEG_INF)
    e1v = jnp.max(el, axis=-1, keepdims=True)
    e1 = first(el == e1v)
    el2 = jnp.where(lane == e1, NEG_INF, el)
    e2v = jnp.max(el2, axis=-1, keepdims=True)
    e2 = first(el2 == e2v)
    t = jnp.exp(e2v - e1v)
    w1 = 1.0 / (1.0 + t)
    w2 = t * w1
    id1 = e1 - MOE_GROUPS
    id2 = e2 - MOE_GROUPS
    onehot = jnp.where((lane == id1) | (lane == id2), 1.0, 0.0)
    before = counts + _dot(tri, onehot.astype(BF16))
    rank1 = jnp.sum(jnp.where(lane == id1, before, 0.0), axis=-1, keepdims=True)
    rank2 = jnp.sum(jnp.where(lane == id2, before, 0.0), axis=-1, keepdims=True)
    slab = jnp.zeros_like(logits)
    for col, val in enumerate((id1, id2, g_p * w1, g_p * w2, rank1, rank2)):
        slab = jnp.where(lane == col, val, slab)
    return slab, counts + jnp.sum(onehot, axis=0, keepdims=True)


def _out_route_kernel(*refs, n_parts):
    a_refs = refs[:n_parts]
    (w_ref, x_ref, gate_ref, nw_ref, shift_ref, scale_ref, wr_ref, br_ref, tri_ref,
     xo_ref, h2_ref, route_ref, cnt_ref) = refs[n_parts:]

    @pl.when(pl.program_id(0) == 0)
    def _():
        cnt_ref[...] = jnp.zeros_like(cnt_ref)

    k0 = 0
    y = None
    for a_ref in a_refs:
        kk = a_ref.shape[1]
        t = _dot(a_ref[...], w_ref[k0:k0 + kk, :])
        y = t if y is None else y + t
        k0 += kk
    x = x_ref[...] + gate_ref[0] * y
    xo_ref[...] = x
    h2 = _norm_mod(x, nw_ref[...], scale_ref[0], shift_ref[0])
    _store_token_tiles(h2_ref, h2)
    h_hi = h2.astype(BF16)
    h_lo = (h2 - h_hi.astype(F32)).astype(BF16)
    both = _dot(h_hi, wr_ref[...])
    logits = both[:, :LANES] + both[:, LANES:] + _dot(h_lo, wr_ref[:, :LANES]) + br_ref[...]
    slab, counts = _route(logits, tri_ref[...], cnt_ref[0:1, :])
    route_ref[...] = slab
    cnt_ref[...] = jnp.broadcast_to(counts, cnt_ref.shape)


def _out_projection_route(parts, w_bf, xs, nw, mods, wr, br, dims):
    nt, d = xs.shape
    batch, n_ctx, seq = dims
    tpb = (n_ctx + seq) // ROW_TILE
    ctx_tiles = n_ctx // ROW_TILE
    row = lambda width: pl.BlockSpec((ROW_TILE, width), lambda i: (i, 0))
    full = lambda shape: pl.BlockSpec(shape, lambda i: (0,) * len(shape))
    mod = lambda chunk_idx: _mod_spec(chunk_idx, d, tpb, ctx_tiles, batch)
    tri = jnp.asarray(np.tril(np.ones((ROW_TILE, ROW_TILE), np.float32), -1), BF16)
    return pl.pallas_call(
        functools.partial(_out_route_kernel, n_parts=len(parts)),
        out_shape=(jax.ShapeDtypeStruct((nt, d), F32), jax.ShapeDtypeStruct((nt * d // LANES, LANES), F32),
                   jax.ShapeDtypeStruct((nt, LANES), F32), jax.ShapeDtypeStruct((8, LANES), F32)),
        grid=(nt // ROW_TILE,),
        in_specs=[row(p.shape[1]) for p in parts] + [
            full(w_bf.shape), row(d), mod(2), full((1, d)), mod(3), mod(4), full(wr.shape), full(br.shape),
            full(tri.shape)],
        out_specs=(row(d), pl.BlockSpec((ROW_TILE * d // LANES, LANES), lambda i: (i, 0)), row(LANES),
                   full((8, LANES))),
        compiler_params=_cparams(1),
    )(*parts, w_bf, xs, mods, nw.reshape(1, d), mods, mods, wr, br, tri)


SCATTER_DMA_PRIORITY = 1


def _dispatch_plan(route, counts_f, n_tok, tiles):
    n_assign = n_tok * MOE_TOP_K
    flat_e = route[:, 0:MOE_TOP_K].astype(jnp.int32).reshape(-1)
    rank = route[:, 4:4 + MOE_TOP_K].astype(jnp.int32).reshape(-1)
    counts = counts_f[0, :MOE_EXPERTS].astype(jnp.int32)
    padded = (counts + MOE_BLOCK - 1) // MOE_BLOCK * MOE_BLOCK
    pad_end = jnp.cumsum(padded)
    pad_start = pad_end - padded
    dest = pad_start[flat_e] + rank
    n_blocks = -(-n_assign // MOE_BLOCK) + MOE_EXPERTS
    n_slots = n_blocks * MOE_BLOCK
    block_start = jnp.arange(n_blocks, dtype=jnp.int32) * MOE_BLOCK
    block_expert = jnp.minimum(jnp.sum((pad_end[None, :] <= block_start[:, None]).astype(jnp.int32), axis=1),
                               MOE_EXPERTS - 1)
    slot_assign = jnp.full((n_slots,), -1, jnp.int32).at[dest].set(jnp.arange(n_assign, dtype=jnp.int32))
    slot = jnp.arange(-MOE_BLOCK, n_slots + MOE_BLOCK, dtype=jnp.int32)
    spare = n_assign + ((slot // MOE_BLOCK) % 2) * MOE_BLOCK + slot % MOE_BLOCK
    pad = jnp.full((MOE_BLOCK,), -1, jnp.int32)
    assign = jnp.concatenate([pad, slot_assign, pad])
    src = jnp.where(assign >= 0, assign // MOE_TOP_K, 0) * tiles
    dst = jnp.where(assign >= 0, (assign % MOE_TOP_K) * n_tok + assign // MOE_TOP_K, spare) * tiles
    return block_expert, src[MOE_BLOCK:], dst[:n_slots + MOE_BLOCK], n_blocks


def _expert_kernel(be_ref, src_ref, dst_ref, h_hbm, w1_ref, w3_ref, w2_ref, out_hbm,
                   xbuf, ybuf, w1b, w3b, w2b, gsem, ssem):
    i = pl.program_id(0)
    last = pl.num_programs(0) - 1
    cur = i % 2
    nxt = 1 - cur

    tiles = xbuf.shape[1] // MOE_BLOCK

    def token(idx):
        return pl.ds(pl.multiple_of(idx, tiles), tiles)

    def start_gather(blk, buf):
        for r in range(MOE_BLOCK):
            pltpu.make_async_copy(h_hbm.at[token(src_ref[blk * MOE_BLOCK + r])],
                                  xbuf.at[buf, pl.ds(r * tiles, tiles)], gsem.at[buf]).start()

    def start_scatter(blk, buf):
        for r in range(MOE_BLOCK):
            pltpu.make_async_copy(ybuf.at[buf, pl.ds(r * tiles, tiles)],
                                  out_hbm.at[token(dst_ref[(blk + 1) * MOE_BLOCK + r])], ssem.at[buf]).start(
                                      priority=SCATTER_DMA_PRIORITY)

    def wait_gather(buf):
        pltpu.make_async_copy(h_hbm.at[pl.ds(0, MOE_BLOCK * tiles)], xbuf.at[buf], gsem.at[buf]).wait()

    def wait_scatter(buf):
        pltpu.make_async_copy(ybuf.at[buf], out_hbm.at[pl.ds(0, MOE_BLOCK * tiles)], ssem.at[buf]).wait()

    @pl.when(i == 0)
    def _():
        ybuf[...] = jnp.zeros_like(ybuf)
        start_gather(0, 0)

    @pl.when((i == 0) | (be_ref[i] != be_ref[jnp.maximum(i - 1, 0)]))
    def _():
        w1b[...] = w1_ref[0, 0].astype(BF16)
        w3b[...] = w3_ref[0, 0].astype(BF16)
        w2b[...] = w2_ref[0, 0].astype(BF16)

    wait_gather(cur)

    @pl.when(i >= 1)
    def _():
        wait_scatter(cur)

    start_gather(i + 1, nxt)
    start_scatter(i - 1, nxt)
    xb = _load_token_tiles(xbuf.at[cur], tiles).astype(BF16)
    act = _silu(_dot(xb, w1b[...])) * _dot(xb, w3b[...])
    _store_token_tiles(ybuf.at[cur], _dot(act.astype(BF16), w2b[...]))

    @pl.when(i == last)
    def _():
        start_scatter(i, cur)
        wait_scatter(nxt)
        wait_scatter(cur)
        wait_gather(nxt)


def _experts(h2, plan, w1, w3, w2, layer):
    block_expert, slot_src, slot_dst, n_blocks = plan
    d, ff = w1.shape[2], w1.shape[3]
    tiles = d // LANES
    nt = h2.shape[0] // tiles
    wspec = lambda shape: pl.BlockSpec((1, 1) + shape, lambda i, be, sr, ds: (layer, be[i], 0, 0))
    grid_spec = pltpu.PrefetchScalarGridSpec(
        num_scalar_prefetch=3,
        grid=(n_blocks,),
        in_specs=[pl.BlockSpec(memory_space=pl.ANY), wspec((d, ff)), wspec((d, ff)), wspec((ff, d))],
        out_specs=pl.BlockSpec(memory_space=pl.ANY),
        scratch_shapes=[pltpu.VMEM((2, MOE_BLOCK * tiles, LANES), F32), pltpu.VMEM((2, MOE_BLOCK * tiles, LANES), F32),
                        pltpu.VMEM((d, ff), BF16), pltpu.VMEM((d, ff), BF16), pltpu.VMEM((ff, d), BF16),
                        pltpu.SemaphoreType.DMA((2,)), pltpu.SemaphoreType.DMA((2,))])
    return pl.pallas_call(
        _expert_kernel,
        out_shape=jax.ShapeDtypeStruct(((nt * MOE_TOP_K + 2 * MOE_BLOCK) * tiles, LANES), F32),
        grid_spec=grid_spec,
        compiler_params=_cparams(1, has_side_effects=True, disable_bounds_checks=True),
    )(block_expert, slot_src, slot_dst, h2, w1, w3, w2)


def _combine_kernel(x_ref, y0_ref, y1_ref, route_ref, gate_ref, *rest, final):
    r = route_ref[...]
    tiles = x_ref.shape[1] // LANES
    y = r[:, 2:3] * _load_token_tiles(y0_ref, tiles) + r[:, 3:4] * _load_token_tiles(y1_ref, tiles)
    x = x_ref[...] + gate_ref[0] * y
    if final:
        fnw_ref, o_ref = rest
        x = x * lax.rsqrt(jnp.mean(x * x, axis=-1, keepdims=True) + EPS) * fnw_ref[...]
    else:
        (o_ref,) = rest
    o_ref[...] = x


def _combine(xs, y2, route, mods, dims, final_norm_w=None):
    nt, d = xs.shape
    batch, n_ctx, seq = dims
    tpb = (n_ctx + seq) // ROW_TILE
    ctx_tiles = n_ctx // ROW_TILE
    row = lambda width: pl.BlockSpec((ROW_TILE, width), lambda i: (i, 0))
    final = final_norm_w is not None
    first = pl.BlockSpec((ROW_TILE * d // LANES, LANES), lambda i: (i, 0))
    second = pl.BlockSpec((ROW_TILE * d // LANES, LANES), lambda i: (i + nt // ROW_TILE, 0))
    in_specs = [row(d), first, second, row(LANES), _mod_spec(5, d, tpb, ctx_tiles, batch)]
    args = [xs, y2, y2, route, mods]
    if final:
        in_specs.append(pl.BlockSpec((1, d), lambda i: (0, 0)))
        args.append(final_norm_w.reshape(1, d))
    return pl.pallas_call(
        functools.partial(_combine_kernel, final=final),
        out_shape=jax.ShapeDtypeStruct((nt, d), F32),
        grid=(nt // ROW_TILE,),
        in_specs=in_specs,
        out_specs=row(d),
        compiler_params=_cparams(1),
    )(*args)


def _even_col_ops():
    rh = RET_HEADS
    ops = [(True, 1.0, 1.0)] * rh + [(True, float(HEAD_DIM) ** -0.5, 1.0)] * rh + [(False, 1.0, 1.0)] * (2 * rh)
    ops += [(True, 1.0, float(HEAD_DIM) ** -0.5)] * ATT_Q_HEADS + [(True, 1.0, 1.0)] * ATT_KV_HEADS
    ops += [(False, 1.0, 1.0)] * ATT_KV_HEADS
    return tuple(ops)


def kernel(x, c, ctx, c_ctx, ada_w, ada_b, norm_w, final_norm_w, ev_w_in, ev_w_out, ret_decay_raw, att_sink,
           od_w_in, od_w_out, hg_lb_logits, hg_norm_w, moe_wg, moe_bg, moe_we, moe_be, moe_w1, moe_w3, moe_w2):
    batch, seq, d = x.shape
    n_ctx = ctx.shape[1]
    depth = ada_w.shape[0]
    assert batch + 1 <= MOD_ROWS and n_ctx % ROW_TILE == 0 and seq % ROW_TILE == 0
    assert seq % GRID_W == 0 and seq >= 3 * CHUNK
    dims = (batch, n_ctx, seq)
    tb = n_ctx + seq
    nt = batch * tb

    xs = jnp.concatenate([ctx, x], axis=1).reshape(nt, d)
    cvec = jnp.concatenate([c, c_ctx[None, :], jnp.zeros((MOD_ROWS - batch - 1, d), F32)], axis=0)
    mods_all = _ada_modulation(cvec, ada_w, ada_b)
    rope = _rope_tables(n_ctx, seq)
    even_ops = _even_col_ops()

    for l in range(depth):
        p = l // 2
        mods = mods_all[l].reshape(MOD_ROWS, 1, 6 * d)
        if l % 2 == 0:
            proj = _in_projection(xs, norm_w[l, 0], mods, ev_w_in[p].astype(BF16), dims, even_ops, rope)
            ret = _retention(proj, ret_decay_raw[p], dims)
            att = _attention(proj, att_sink[p], dims, 4 * RET_HEADS)
            parts, w_out = [ret, att], ev_w_out[p]
        else:
            proj = _in_projection(xs, norm_w[l, 0], mods, od_w_in[p].astype(BF16), dims)
            parts, w_out = [_hgrn2(proj, hg_lb_logits, hg_norm_w[p], l, dims)], od_w_out[p]
        n_logit = MOE_GROUPS + MOE_EXPERTS
        wr = jnp.concatenate([moe_wg[l], jnp.moveaxis(moe_we[l], 0, 1).reshape(d, MOE_EXPERTS),
                              jnp.zeros((d, LANES - n_logit), F32)], axis=1)
        wr_hi = wr.astype(BF16)
        wr = jnp.concatenate([wr_hi, (wr - wr_hi.astype(F32)).astype(BF16)], axis=1)
        br = jnp.concatenate([moe_bg[l], moe_be[l].reshape(-1), jnp.zeros((LANES - n_logit,), F32)])[None, :]
        xs, h2, route, counts = _out_projection_route(
            parts, w_out.astype(BF16), xs, norm_w[l, 1], mods, wr, br, dims)
        plan = _dispatch_plan(route, counts, nt, d // LANES)
        y2 = _experts(h2, plan, moe_w1, moe_w3, moe_w2, l)
        xs = _combine(xs, y2, route, mods, dims, final_norm_w if l == depth - 1 else None)

    return xs.reshape(batch, tb, d)[:, n_ctx:]
```

```python
import functools

import numpy as np
import jax
import jax.numpy as jnp
from jax import lax
from jax.experimental import pallas as pl
from jax.experimental.pallas import tpu as pltpu

F32 = jnp.float32
BF16 = jnp.bfloat16
HIGHEST = lax.Precision.HIGHEST

EPS = 1e-6
NEG_INF = -1e30
LANES = 128
GRID_W = 64
ROPE_THETA = 10000.0
HEAD_DIM = 128
RET_HEADS = 4
ATT_Q_HEADS = 4
ATT_KV_HEADS = 2
ATT_GROUP = ATT_Q_HEADS // ATT_KV_HEADS
WINDOW = 128
HG_HEADS = 8
MOE_GROUPS = 4
MOE_EPG = 8
MOE_EXPERTS = MOE_GROUPS * MOE_EPG
MOE_TOP_K = 2
MOE_BLOCK = 128
CHUNK = 128
ROW_TILE = 256
MOD_ROWS = 8
VMEM_LIMIT = 56 * 1024 * 1024


def _cparams(n_axes, **kw):
    return pltpu.CompilerParams(dimension_semantics=("arbitrary",) * n_axes,
                                vmem_limit_bytes=VMEM_LIMIT, **kw)


def _dot(a, b):
    return jnp.dot(a, b, preferred_element_type=F32)


def _dot_nt(a, b):
    return lax.dot_general(a, b, (((1,), (1,)), ((), ())), preferred_element_type=F32)


def _dot_tn(a, b):
    return lax.dot_general(a, b, (((0,), (0,)), ((), ())), preferred_element_type=F32)


def _silu(x):
    return x * jax.nn.sigmoid(x)


def _norm_mod(x, nw, scale, shift):
    ms = jnp.mean(x * x, axis=-1, keepdims=True)
    y = x * lax.rsqrt(ms + EPS) * nw
    return y * (1.0 + scale) + shift


def _ada_kernel(c_ref, w_ref, b_ref, o_ref):
    s = _silu(c_ref[...])
    o_ref[0] = jnp.dot(s, w_ref[0], precision=HIGHEST, preferred_element_type=F32) + b_ref[0]


def _ada_modulation(cvec, ada_w, ada_b):
    depth, d, n6 = ada_w.shape
    tn = 1536
    return pl.pallas_call(
        _ada_kernel,
        out_shape=jax.ShapeDtypeStruct((depth, MOD_ROWS, n6), F32),
        grid=(depth, n6 // tn),
        in_specs=[pl.BlockSpec((MOD_ROWS, d), lambda l, j: (0, 0)),
                  pl.BlockSpec((1, d, tn), lambda l, j: (l, 0, j)),
                  pl.BlockSpec((1, 1, tn), lambda l, j: (l, 0, j))],
        out_specs=pl.BlockSpec((1, MOD_ROWS, tn), lambda l, j: (l, 0, j)),
        compiler_params=_cparams(2),
    )(cvec, ada_w, ada_b.reshape(depth, 1, n6))


def _swap_halves(x):
    lane = lax.broadcasted_iota(jnp.int32, x.shape, 1)
    return jnp.where((lane % 64) < 32, pltpu.roll(x, 96, 1), pltpu.roll(x, 32, 1))


def _proj_kernel(x_ref, nw_ref, shift_ref, scale_ref, w_ref, *rest, col_ops, chunk):
    if col_ops is None:
        (o_ref,) = rest
    else:
        cos_ref, sin_ref, o_ref = rest
    h = _norm_mod(x_ref[...], nw_ref[...], scale_ref[0], shift_ref[0]).astype(BF16)
    nout = o_ref.shape[0] * LANES
    for c0 in range(0, nout, chunk):
        acc = _dot(h, w_ref[:, c0:c0 + chunk])
        for hd in range(chunk // LANES):
            col = c0 + hd * LANES
            a = acc[:, hd * LANES:(hd + 1) * LANES]
            if col_ops is not None:
                rope, pre, post = col_ops[col // LANES]
                if pre != 1.0:
                    a = a * pre
                if rope:
                    a = a * cos_ref[...] + _swap_halves(a) * sin_ref[...]
                if post != 1.0:
                    a = a * post
            o_ref[col // LANES] = a


def _tile_mod_row(i, tiles_per_batch, ctx_tiles, batch):
    return jnp.where(i % tiles_per_batch < ctx_tiles, batch, i // tiles_per_batch)


def _mod_spec(chunk_idx, d, tiles_per_batch, ctx_tiles, batch):
    return pl.BlockSpec(
        (1, 1, d), lambda i: (_tile_mod_row(i, tiles_per_batch, ctx_tiles, batch), 0, chunk_idx))


def _in_projection(xs, nw, mods, w_bf, dims, col_ops=None, rope=None):
    nt, d = xs.shape
    nout = w_bf.shape[1]
    batch, n_ctx, seq = dims
    tpb = (n_ctx + seq) // ROW_TILE
    ctx_tiles = n_ctx // ROW_TILE
    in_specs = [pl.BlockSpec((ROW_TILE, d), lambda i: (i, 0)),
                pl.BlockSpec((1, d), lambda i: (0, 0)),
                _mod_spec(0, d, tpb, ctx_tiles, batch),
                _mod_spec(1, d, tpb, ctx_tiles, batch),
                pl.BlockSpec((d, nout), lambda i: (0, 0))]
    args = [xs, nw.reshape(1, d), mods, mods, w_bf]
    if col_ops is not None:
        rope_spec = pl.BlockSpec((ROW_TILE, LANES), lambda i: (i % tpb, 0))
        in_specs += [rope_spec, rope_spec]
        args += [rope[0], rope[1]]
    return pl.pallas_call(
        functools.partial(_proj_kernel, col_ops=col_ops, chunk=512),
        out_shape=jax.ShapeDtypeStruct((nout // LANES, nt, LANES), F32),
        grid=(nt // ROW_TILE,),
        in_specs=in_specs,
        out_specs=pl.BlockSpec((nout // LANES, ROW_TILE, LANES), lambda i: (0, i, 0)),
        compiler_params=_cparams(1),
    )(*args)


def _rope_tables(n_ctx, seq):
    n_rows = seq // GRID_W
    row = jnp.repeat(jnp.arange(n_rows, dtype=F32), GRID_W)
    col = jnp.tile(jnp.arange(GRID_W, dtype=F32), n_rows)
    axis_dim = HEAD_DIM // 2
    inv_freq = ROPE_THETA ** (-jnp.arange(0, axis_dim, 2, dtype=F32) / axis_dim)
    ang_r = row[:, None] * inv_freq[None, :]
    ang_c = col[:, None] * inv_freq[None, :]
    cos = jnp.concatenate([jnp.cos(ang_r), jnp.cos(ang_r), jnp.cos(ang_c), jnp.cos(ang_c)], axis=-1)
    sin = jnp.concatenate([-jnp.sin(ang_r), jnp.sin(ang_r), -jnp.sin(ang_c), jnp.sin(ang_c)], axis=-1)
    cos = jnp.concatenate([jnp.ones((n_ctx, HEAD_DIM), F32), cos], axis=0)
    sin = jnp.concatenate([jnp.zeros((n_ctx, HEAD_DIM), F32), sin], axis=0)
    return cos, sin


def _head_spec(rows, index_map):
    return pl.BlockSpec((None, rows, LANES), index_map)


def _backward_chunk(t, ctx_chunks, n_chunks):
    return jnp.where(t < ctx_chunks, ctx_chunks - 1 - t, n_chunks - 1 - (t - ctx_chunks))


def _ret_kernel(raw_ref, q_ref, k_ref, v_ref, g_ref, o_ref, of_scr, ob_scr, s_scr, dec_scr, qd_scr, kd_scr,
                *, ctx_chunks, n_chunks):
    h = pl.program_id(1)
    c = CHUNK
    ii = lax.broadcasted_iota(jnp.int32, (c, c), 0).astype(F32)
    jj = lax.broadcasted_iota(jnp.int32, (c, c), 1).astype(F32)
    lg = []
    for d in range(2):
        lgd = -jnp.exp(jnp.full((c, c), raw_ref[d, h], F32))
        lg.append(lgd)
        rel = (ii - jj) if d == 0 else (jj - ii)
        dec_scr[d] = jnp.where(rel >= 0, jnp.exp(jnp.maximum(rel, 0.0) * lgd), 0.0)
        qd_scr[d] = jnp.exp(((ii + 1.0) if d == 0 else (c - ii)) * lgd)
        kd_scr[d] = jnp.exp(((c - 1.0 - ii) if d == 0 else ii) * lgd)

    def chunk(n, d):
        rows = pl.ds(pl.multiple_of(n * c, c), c)
        q = q_ref[rows, :]
        k = k_ref[rows, :]
        vb = v_ref[rows, :].astype(BF16)
        sc = _dot_nt(q.astype(BF16), k.astype(BF16)) * dec_scr[d]
        s = s_scr[d]
        o = _dot(sc.astype(BF16), vb) + _dot((q * qd_scr[d]).astype(BF16), s.astype(BF16))
        chunk_decay = jnp.exp(float(c) * lg[d][0:1, :])
        s_scr[d] = chunk_decay * s + _dot_tn((k * kd_scr[d]).astype(BF16), vb)
        return rows, o

    s_scr[...] = jnp.zeros_like(s_scr)

    def scan(t, carry):
        rows, o = chunk(t, 0)
        of_scr[rows, :] = o
        rows, o = chunk(_backward_chunk(t, ctx_chunks, n_chunks), 1)
        ob_scr[rows, :] = o
        return carry

    lax.fori_loop(0, n_chunks, scan, 0, unroll=2)

    def readout(n, carry):
        rows = pl.ds(pl.multiple_of(n * c, c), c)
        o = of_scr[rows, :] + ob_scr[rows, :]
        y = o * lax.rsqrt(jnp.mean(o * o, axis=-1, keepdims=True) + EPS)
        o_ref[rows, :] = (_silu(g_ref[rows, :]) * y).astype(BF16)
        return carry

    lax.fori_loop(0, n_chunks, readout, 0)


def _retention(proj, ret_decay_raw, dims):
    batch, n_ctx, seq = dims
    tb = n_ctx + seq
    spec = lambda off: _head_spec(tb, lambda b, h: (off + h, b, 0))
    return pl.pallas_call(
        functools.partial(_ret_kernel, ctx_chunks=n_ctx // CHUNK, n_chunks=tb // CHUNK),
        out_shape=jax.ShapeDtypeStruct((RET_HEADS, batch * tb, LANES), BF16),
        grid=(batch, RET_HEADS),
        in_specs=[pl.BlockSpec(memory_space=pltpu.SMEM),
                  spec(0), spec(RET_HEADS), spec(2 * RET_HEADS), spec(3 * RET_HEADS)],
        out_specs=_head_spec(tb, lambda b, h: (h, b, 0)),
        scratch_shapes=[pltpu.VMEM((tb, LANES), F32), pltpu.VMEM((tb, LANES), F32),
                        pltpu.VMEM((2, LANES, LANES), F32),
                        pltpu.VMEM((2, CHUNK, CHUNK), F32), pltpu.VMEM((2, CHUNK, LANES), F32),
                        pltpu.VMEM((2, CHUNK, LANES), F32)],
        compiler_params=_cparams(2),
    )(ret_decay_raw, proj, proj, proj, proj)


def _att_kernel(sink_ref, q0_ref, q1_ref, k_ref, v_ref, o_ref, kb_scr, vb_scr, bias_scr, *, n_ctx, seq):
    hk = pl.program_id(1)
    c = CHUNK
    win = 3 * c
    kb_scr[...] = k_ref[...].astype(BF16)
    vb_scr[...] = v_ref[...].astype(BF16)
    kc = kb_scr[0:n_ctx, :]
    vc = vb_scr[0:n_ctx, :]
    ii = lax.broadcasted_iota(jnp.int32, (c, win), 0)
    jj = lax.broadcasted_iota(jnp.int32, (c, win), 1)
    for off in range(3):
        bias_scr[off] = jnp.where(jnp.abs(ii - jj + off * c) <= WINDOW, 0.0, NEG_INF)

    def softmax_out(s_ctx, sink, s_loc=None, v_loc=None):
        m = jnp.maximum(jnp.max(s_ctx, axis=-1, keepdims=True), sink)
        if s_loc is not None:
            m = jnp.maximum(m, jnp.max(s_loc, axis=-1, keepdims=True))
        p_ctx = jnp.exp(s_ctx - m)
        den = jnp.sum(p_ctx, axis=-1, keepdims=True) + jnp.exp(sink - m)
        o = _dot(p_ctx.astype(BF16), vc)
        if s_loc is not None:
            p_loc = jnp.exp(s_loc - m)
            den = den + jnp.sum(p_loc, axis=-1, keepdims=True)
            o = o + _dot(p_loc.astype(BF16), v_loc)
        return o / den

    heads = []
    for g, q_ref in enumerate((q0_ref, q1_ref)):
        sink = jnp.full((c, 1), sink_ref[hk * ATT_GROUP + g], F32)
        heads.append((q_ref, sink, g))

    for q_ref, sink, g in heads:
        for cc in range(n_ctx // c):
            qb = q_ref[cc * c:(cc + 1) * c, :].astype(BF16)
            o_ref[g, cc * c:(cc + 1) * c, :] = softmax_out(_dot_nt(qb, kc), sink).astype(BF16)

    def block(n, carry):
        q_rows = pl.ds(pl.multiple_of(n_ctx + n * c, c), c)
        start = jnp.clip(n * c - c, 0, seq - win)
        k_rows = pl.ds(pl.multiple_of(n_ctx + start, c), win)
        k_loc = kb_scr[k_rows, :]
        v_loc = vb_scr[k_rows, :]
        bias = bias_scr[(n * c - start) // c]
        for q_ref, sink, g in heads:
            qb = q_ref[q_rows, :].astype(BF16)
            o = softmax_out(_dot_nt(qb, kc), sink, _dot_nt(qb, k_loc) + bias, v_loc)
            o_ref[g, q_rows, :] = o.astype(BF16)
        return carry

    lax.fori_loop(0, seq // c, block, 0)


def _attention(proj, att_sink, dims, col0):
    batch, n_ctx, seq = dims
    tb = n_ctx + seq
    qb, kb, vb = col0, col0 + ATT_Q_HEADS, col0 + ATT_Q_HEADS + ATT_KV_HEADS
    return pl.pallas_call(
        functools.partial(_att_kernel, n_ctx=n_ctx, seq=seq),
        out_shape=jax.ShapeDtypeStruct((ATT_Q_HEADS, batch * tb, LANES), BF16),
        grid=(batch, ATT_KV_HEADS),
        in_specs=[pl.BlockSpec(memory_space=pltpu.SMEM),
                  _head_spec(tb, lambda b, h: (qb + ATT_GROUP * h, b, 0)),
                  _head_spec(tb, lambda b, h: (qb + ATT_GROUP * h + 1, b, 0)),
                  _head_spec(tb, lambda b, h: (kb + h, b, 0)),
                  _head_spec(tb, lambda b, h: (vb + h, b, 0))],
        out_specs=pl.BlockSpec((ATT_GROUP, tb, LANES), lambda b, h: (h, b, 0)),
        scratch_shapes=[pltpu.VMEM((tb, LANES), BF16), pltpu.VMEM((tb, LANES), BF16),
                        pltpu.VMEM((3, CHUNK, 3 * CHUNK), F32)],
        compiler_params=_cparams(2),
    )(att_sink, proj, proj, proj, proj)


HG_LEVELS = 7
HG_MXU_LEVELS = 3


def _hgrn_constants():
    c = CHUNK
    i = np.arange(c)[:, None]
    r = np.arange(c)[None, :]
    mats = [r <= i]
    for m in range(1, HG_MXU_LEVELS + 1):
        half = 1 << (m - 1)
        beta = (i // (2 * half)) * (2 * half) + half - 1
        upper = (i % (2 * half)) >= half
        mats.append((upper & (r > beta) & (r <= i)) | ((~upper) & (r > i) & (r <= beta)))
    fwd = [m.astype(np.float32) for m in mats]
    bwd = [m[::-1, ::-1] for m in fwd]
    stack = np.stack([np.concatenate(fwd, 0), np.concatenate(bwd, 0)])
    x = i ^ r
    level = np.where(r > i, -1, np.where(r == i, 0, np.floor(np.log2(np.maximum(x, 1))) + 1)).astype(np.int32)
    levels = np.stack([level, level[::-1, ::-1]])
    return stack, levels


def _hgrn_kernel(q_ref, zf_ref, zb_ref, v_ref, g_ref, lbl_ref, nw_ref, m_ref, lv_ref, o_ref,
                 of_scr, ob_scr, st_scr, *, layer, ctx_chunks, n_chunks):
    c = CHUNK
    z = lbl_ref[...]
    e = jnp.exp(z - jnp.max(z, axis=0, keepdims=True))
    sm = e / jnp.sum(e, axis=0, keepdims=True)
    lb = jnp.sum(sm[1:layer + 1], axis=0, keepdims=True)
    row = lax.broadcasted_iota(jnp.int32, (c, LANES), 0)
    upper = [[((row if d == 0 else c - 1 - row) & (1 << (m - 1))) != 0 for m in range(1, HG_MXU_LEVELS + 1)]
             for d in range(2)]

    def level_operand(q, k, a, half, d):
        pieces = []
        for base in range(0, c, 2 * half):
            lo, hi = slice(base, base + half), slice(base + half, base + 2 * half)
            if d == 0:
                ref = a[base + half - 1:base + half]
                pieces += [k[lo] * jnp.exp(ref - a[lo]), q[hi] * jnp.exp(a[hi] - ref)]
            else:
                ref = a[base + half:base + half + 1]
                pieces += [q[lo] * jnp.exp(a[lo] - ref), k[hi] * jnp.exp(ref - a[hi])]
        return jnp.concatenate(pieces, axis=0).astype(BF16)

    def chunk(n, d, z_ref):
        rows = pl.ds(pl.multiple_of(n * c, c), c)
        q = q_ref[rows, :]
        v = v_ref[rows, :]
        f = lb + (1.0 - lb) * jax.nn.sigmoid(z_ref[rows, :])
        g = jnp.log(f)
        k = 1.0 - f
        g_hi = g.astype(BF16)
        g_lo = (g - g_hi.astype(F32)).astype(BF16)
        res = _dot(m_ref[d], jnp.concatenate([g_hi, g_lo], axis=1))
        ex = res[:, :c] + res[:, c:]
        a = ex[0:c]
        st = st_scr[d]
        o = _dot_nt((q * jnp.exp(a)).astype(BF16), st.astype(BF16))
        lv = lv_ref[d]
        sc = jnp.where(lv == 0, _dot_nt(q.astype(BF16), k.astype(BF16)), 0.0)
        for m in range(1, HG_LEVELS + 1):
            if m <= HG_MXU_LEVELS:
                zz = (jnp.where(upper[d][m - 1], q, k) * jnp.exp(ex[m * c:(m + 1) * c])).astype(BF16)
            else:
                zz = level_operand(q, k, a, 1 << (m - 1), d)
            sc = jnp.where(lv == m, _dot_nt(zz, zz), sc)
        vb = v.astype(BF16)
        o = o + _dot(sc.astype(BF16), vb)
        a_last = a[c - 1:c] if d == 0 else a[0:1]
        kd = (k * jnp.exp(a_last - a)).astype(BF16)
        st_scr[d] = st * jnp.exp(a_last) + _dot_tn(vb, kd)
        return rows, o

    st_scr[...] = jnp.zeros_like(st_scr)

    def scan(t, carry):
        rows, o = chunk(t, 0, zf_ref)
        of_scr[rows, :] = o
        rows, o = chunk(_backward_chunk(t, ctx_chunks, n_chunks), 1, zb_ref)
        ob_scr[rows, :] = o
        return carry

    lax.fori_loop(0, n_chunks, scan, 0, unroll=2)

    def readout(n, carry):
        rows = pl.ds(pl.multiple_of(n * c, c), c)
        o = of_scr[rows, :] + ob_scr[rows, :]
        y = o * lax.rsqrt(jnp.mean(o * o, axis=-1, keepdims=True) + EPS) * nw_ref[...]
        o_ref[rows, :] = (y * _silu(g_ref[rows, :])).astype(BF16)
        return carry

    lax.fori_loop(0, n_chunks, readout, 0)


def _hgrn2(proj, hg_lb_logits, hg_norm_w, layer, dims):
    batch, n_ctx, seq = dims
    tb = n_ctx + seq
    depth = hg_lb_logits.shape[0]
    stack, levels = _hgrn_constants()
    spec = lambda off: _head_spec(tb, lambda b, h: (off + h, b, 0))
    n_mats = 1 + HG_MXU_LEVELS
    return pl.pallas_call(
        functools.partial(_hgrn_kernel, layer=layer, ctx_chunks=n_ctx // CHUNK, n_chunks=tb // CHUNK),
        out_shape=jax.ShapeDtypeStruct((HG_HEADS, batch * tb, LANES), BF16),
        grid=(batch, HG_HEADS),
        in_specs=[spec(0), spec(HG_HEADS), spec(2 * HG_HEADS), spec(3 * HG_HEADS), spec(4 * HG_HEADS),
                  pl.BlockSpec((depth, LANES), lambda b, h: (0, h)),
                  pl.BlockSpec((1, LANES), lambda b, h: (0, 0)),
                  pl.BlockSpec((2, n_mats * CHUNK, CHUNK), lambda b, h: (0, 0, 0)),
                  pl.BlockSpec((2, CHUNK, CHUNK), lambda b, h: (0, 0, 0))],
        out_specs=_head_spec(tb, lambda b, h: (h, b, 0)),
        scratch_shapes=[pltpu.VMEM((tb, LANES), F32), pltpu.VMEM((tb, LANES), F32),
                        pltpu.VMEM((2, LANES, LANES), F32)],
        compiler_params=_cparams(2),
    )(proj, proj, proj, proj, proj, hg_lb_logits, hg_norm_w.reshape(1, LANES),
      jnp.asarray(stack, BF16), jnp.asarray(levels, jnp.int32))


SUBLANES = 8


def _store_token_tiles(ref, x):
    rows, tiles = x.shape[0], x.shape[1] // LANES
    for s in range(tiles):
        ref[pl.ds(s, rows, stride=tiles), :] = x[:, s * LANES:(s + 1) * LANES]


def _load_token_tiles(ref, tiles):
    rows = ref.shape[0] // tiles
    return jnp.concatenate([ref[pl.ds(s, rows, stride=tiles), :] for s in range(tiles)], axis=1)


def _route(logits, tri, counts):
    lane = lax.broadcasted_iota(jnp.int32, logits.shape, 1).astype(F32)
    big = float(LANES)
    first = lambda hit: jnp.min(jnp.where(hit, lane, big), axis=-1, keepdims=True)
    is_g = lane < MOE_GROUPS
    gl = jnp.where(is_g, logits, NEG_INF)
    gmax = jnp.max(gl, axis=-1, keepdims=True)
    gsum = jnp.sum(jnp.where(is_g, jnp.exp(gl - gmax), 0.0), axis=-1, keepdims=True)
    g_p = 1.0 / gsum
    lo = MOE_GROUPS + first(gl == gmax) * MOE_EPG
    el = jnp.where((lane >= lo) & (lane < lo + MOE_EPG), logits, NEG_INF)
    e1v = jnp.max(el, axis=-1, keepdims=True)
    e1 = first(el == e1v)
    el2 = jnp.where(lane == e1, NEG_INF, el)
    e2v = jnp.max(el2, axis=-1, keepdims=True)
    e2 = first(el2 == e2v)
    t = jnp.exp(e2v - e1v)
    w1 = 1.0 / (1.0 + t)
    w2 = t * w1
    id1 = e1 - MOE_GROUPS
    id2 = e2 - MOE_GROUPS
    onehot = jnp.where((lane == id1) | (lane == id2), 1.0, 0.0)
    before = counts + _dot(tri, onehot.astype(BF16))
    rank1 = jnp.sum(jnp.where(lane == id1, before, 0.0), axis=-1, keepdims=True)
    rank2 = jnp.sum(jnp.where(lane == id2, before, 0.0), axis=-1, keepdims=True)
    slab = jnp.zeros_like(logits)
    for col, val in enumerate((id1, id2, g_p * w1, g_p * w2, rank1, rank2)):
        slab = jnp.where(lane == col, val, slab)
    return slab, counts + jnp.sum(onehot, axis=0, keepdims=True)


def _out_route_kernel(*refs, n_parts):
    a_refs = refs[:n_parts]
    (w_ref, x_ref, gate_ref, nw_ref, shift_ref, scale_ref, wr_ref, br_ref, tri_ref,
     xo_ref, h2_ref, route_ref, cnt_ref) = refs[n_parts:]

    @pl.when(pl.program_id(0) == 0)
    def _():
        cnt_ref[...] = jnp.zeros_like(cnt_ref)

    k0 = 0
    y = None
    for a_ref in a_refs:
        a = jnp.concatenate([a_ref[h] for h in range(a_ref.shape[0])], axis=1)
        kk = a.shape[1]
        t = _dot(a, w_ref[k0:k0 + kk, :])
        y = t if y is None else y + t
        k0 += kk
    x = x_ref[...] + gate_ref[0] * y
    xo_ref[...] = x
    h2 = _norm_mod(x, nw_ref[...], scale_ref[0], shift_ref[0])
    _store_token_tiles(h2_ref, h2)
    h_hi = h2.astype(BF16)
    h_lo = (h2 - h_hi.astype(F32)).astype(BF16)
    both = _dot(h_hi, wr_ref[...])
    logits = both[:, :LANES] + both[:, LANES:] + _dot(h_lo, wr_ref[:, :LANES]) + br_ref[...]
    slab, counts = _route(logits, tri_ref[...], cnt_ref[0:1, :])
    route_ref[...] = slab
    cnt_ref[...] = jnp.broadcast_to(counts, cnt_ref.shape)


def _out_projection_route(parts, w_bf, xs, nw, mods, wr, br, dims):
    nt, d = xs.shape
    batch, n_ctx, seq = dims
    tpb = (n_ctx + seq) // ROW_TILE
    ctx_tiles = n_ctx // ROW_TILE
    row = lambda width: pl.BlockSpec((ROW_TILE, width), lambda i: (i, 0))
    full = lambda shape: pl.BlockSpec(shape, lambda i: (0,) * len(shape))
    mod = lambda chunk_idx: _mod_spec(chunk_idx, d, tpb, ctx_tiles, batch)
    tri = jnp.asarray(np.tril(np.ones((ROW_TILE, ROW_TILE), np.float32), -1), BF16)
    return pl.pallas_call(
        functools.partial(_out_route_kernel, n_parts=len(parts)),
        out_shape=(jax.ShapeDtypeStruct((nt, d), F32), jax.ShapeDtypeStruct((nt * d // LANES, LANES), F32),
                   jax.ShapeDtypeStruct((nt, LANES), F32), jax.ShapeDtypeStruct((8, LANES), F32)),
        grid=(nt // ROW_TILE,),
        in_specs=[pl.BlockSpec((p.shape[0], ROW_TILE, LANES), lambda i: (0, i, 0)) for p in parts] + [
            full(w_bf.shape), row(d), mod(2), full((1, d)), mod(3), mod(4), full(wr.shape), full(br.shape),
            full(tri.shape)],
        out_specs=(row(d), pl.BlockSpec((ROW_TILE * d // LANES, LANES), lambda i: (i, 0)), row(LANES),
                   full((8, LANES))),
        compiler_params=_cparams(1),
    )(*parts, w_bf, xs, mods, nw.reshape(1, d), mods, mods, wr, br, tri)


SCATTER_DMA_PRIORITY = 1


def _dispatch_plan(route, counts_f, n_tok, tiles):
    n_assign = n_tok * MOE_TOP_K
    flat_e = route[:, 0:MOE_TOP_K].astype(jnp.int32).reshape(-1)
    rank = route[:, 4:4 + MOE_TOP_K].astype(jnp.int32).reshape(-1)
    counts = counts_f[0, :MOE_EXPERTS].astype(jnp.int32)
    padded = (counts + MOE_BLOCK - 1) // MOE_BLOCK * MOE_BLOCK
    pad_end = jnp.cumsum(padded)
    pad_start = pad_end - padded
    dest = pad_start[flat_e] + rank
    n_blocks = -(-n_assign // MOE_BLOCK) + MOE_EXPERTS
    n_slots = n_blocks * MOE_BLOCK
    block_start = jnp.arange(n_blocks, dtype=jnp.int32) * MOE_BLOCK
    block_expert = jnp.minimum(jnp.sum((pad_end[None, :] <= block_start[:, None]).astype(jnp.int32), axis=1),
                               MOE_EXPERTS - 1)
    slot_assign = jnp.full((n_slots,), -1, jnp.int32).at[dest].set(jnp.arange(n_assign, dtype=jnp.int32))
    slot = jnp.arange(-MOE_BLOCK, n_slots + MOE_BLOCK, dtype=jnp.int32)
    spare = n_assign + ((slot // MOE_BLOCK) % 2) * MOE_BLOCK + slot % MOE_BLOCK
    pad = jnp.full((MOE_BLOCK,), -1, jnp.int32)
    assign = jnp.concatenate([pad, slot_assign, pad])
    src = jnp.where(assign >= 0, assign // MOE_TOP_K, 0) * tiles
    dst = jnp.where(assign >= 0, (assign % MOE_TOP_K) * n_tok + assign // MOE_TOP_K, spare) * tiles
    return block_expert, src[MOE_BLOCK:], dst[:n_slots + MOE_BLOCK], n_blocks


def _expert_kernel(be_ref, src_ref, dst_ref, h_hbm, w1_ref, w3_ref, w2_ref, out_hbm,
                   xbuf, ybuf, w1b, w3b, w2b, gsem, ssem):
    i = pl.program_id(0)
    last = pl.num_programs(0) - 1
    cur = i % 2
    nxt = 1 - cur

    tiles = xbuf.shape[1] // MOE_BLOCK

    def token(idx):
        return pl.ds(pl.multiple_of(idx, tiles), tiles)

    def start_gather(blk, buf):
        for r in range(MOE_BLOCK):
            pltpu.make_async_copy(h_hbm.at[token(src_ref[blk * MOE_BLOCK + r])],
                                  xbuf.at[buf, pl.ds(r * tiles, tiles)], gsem.at[buf]).start()

    def start_scatter(blk, buf):
        for r in range(MOE_BLOCK):
            pltpu.make_async_copy(ybuf.at[buf, pl.ds(r * tiles, tiles)],
                                  out_hbm.at[token(dst_ref[(blk + 1) * MOE_BLOCK + r])], ssem.at[buf]).start(
                                      priority=SCATTER_DMA_PRIORITY)

    def wait_gather(buf):
        pltpu.make_async_copy(h_hbm.at[pl.ds(0, MOE_BLOCK * tiles)], xbuf.at[buf], gsem.at[buf]).wait()

    def wait_scatter(buf):
        pltpu.make_async_copy(ybuf.at[buf], out_hbm.at[pl.ds(0, MOE_BLOCK * tiles)], ssem.at[buf]).wait()

    @pl.when(i == 0)
    def _():
        ybuf[...] = jnp.zeros_like(ybuf)
        start_gather(0, 0)

    @pl.when((i == 0) | (be_ref[i] != be_ref[jnp.maximum(i - 1, 0)]))
    def _():
        w1b[...] = w1_ref[0, 0].astype(BF16)
        w3b[...] = w3_ref[0, 0].astype(BF16)
        w2b[...] = w2_ref[0, 0].astype(BF16)

    wait_gather(cur)

    @pl.when(i >= 1)
    def _():
        wait_scatter(cur)

    start_gather(i + 1, nxt)
    start_scatter(i - 1, nxt)
    xb = _load_token_tiles(xbuf.at[cur], tiles).astype(BF16)
    act = _silu(_dot(xb, w1b[...])) * _dot(xb, w3b[...])
    _store_token_tiles(ybuf.at[cur], _dot(act.astype(BF16), w2b[...]))

    @pl.when(i == last)
    def _():
        start_scatter(i, cur)
        wait_scatter(nxt)
        wait_scatter(cur)
        wait_gather(nxt)


def _experts(h2, plan, w1, w3, w2, layer):
    block_expert, slot_src, slot_dst, n_blocks = plan
    d, ff = w1.shape[2], w1.shape[3]
    tiles = d // LANES
    nt = h2.shape[0] // tiles
    wspec = lambda shape: pl.BlockSpec((1, 1) + shape, lambda i, be, sr, ds: (layer, be[i], 0, 0))
    grid_spec = pltpu.PrefetchScalarGridSpec(
        num_scalar_prefetch=3,
        grid=(n_blocks,),
        in_specs=[pl.BlockSpec(memory_space=pl.ANY), wspec((d, ff)), wspec((d, ff)), wspec((ff, d))],
        out_specs=pl.BlockSpec(memory_space=pl.ANY),
        scratch_shapes=[pltpu.VMEM((2, MOE_BLOCK * tiles, LANES), F32), pltpu.VMEM((2, MOE_BLOCK * tiles, LANES), F32),
                        pltpu.VMEM((d, ff), BF16), pltpu.VMEM((d, ff), BF16), pltpu.VMEM((ff, d), BF16),
                        pltpu.SemaphoreType.DMA((2,)), pltpu.SemaphoreType.DMA((2,))])
    return pl.pallas_call(
        _expert_kernel,
        out_shape=jax.ShapeDtypeStruct(((nt * MOE_TOP_K + 2 * MOE_BLOCK) * tiles, LANES), F32),
        grid_spec=grid_spec,
        compiler_params=_cparams(1, has_side_effects=True, disable_bounds_checks=True),
    )(block_expert, slot_src, slot_dst, h2, w1, w3, w2)


def _combine_kernel(x_ref, y0_ref, y1_ref, route_ref, gate_ref, *rest, final):
    r = route_ref[...]
    tiles = x_ref.shape[1] // LANES
    y = r[:, 2:3] * _load_token_tiles(y0_ref, tiles) + r[:, 3:4] * _load_token_tiles(y1_ref, tiles)
    x = x_ref[...] + gate_ref[0] * y
    if final:
        fnw_ref, o_ref = rest
        x = x * lax.rsqrt(jnp.mean(x * x, axis=-1, keepdims=True) + EPS) * fnw_ref[...]
    else:
        (o_ref,) = rest
    o_ref[...] = x


def _combine(xs, y2, route, mods, dims, final_norm_w=None):
    nt, d = xs.shape
    batch, n_ctx, seq = dims
    tpb = (n_ctx + seq) // ROW_TILE
    ctx_tiles = n_ctx // ROW_TILE
    row = lambda width: pl.BlockSpec((ROW_TILE, width), lambda i: (i, 0))
    final = final_norm_w is not None
    first = pl.BlockSpec((ROW_TILE * d // LANES, LANES), lambda i: (i, 0))
    second = pl.BlockSpec((ROW_TILE * d // LANES, LANES), lambda i: (i + nt // ROW_TILE, 0))
    in_specs = [row(d), first, second, row(LANES), _mod_spec(5, d, tpb, ctx_tiles, batch)]
    args = [xs, y2, y2, route, mods]
    if final:
        in_specs.append(pl.BlockSpec((1, d), lambda i: (0, 0)))
        args.append(final_norm_w.reshape(1, d))
    return pl.pallas_call(
        functools.partial(_combine_kernel, final=final),
        out_shape=jax.ShapeDtypeStruct((nt, d), F32),
        grid=(nt // ROW_TILE,),
        in_specs=in_specs,
        out_specs=row(d),
        compiler_params=_cparams(1),
    )(*args)


def _even_col_ops():
    rh = RET_HEADS
    ops = [(True, 1.0, 1.0)] * rh + [(True, float(HEAD_DIM) ** -0.5, 1.0)] * rh + [(False, 1.0, 1.0)] * (2 * rh)
    ops += [(True, 1.0, float(HEAD_DIM) ** -0.5)] * ATT_Q_HEADS + [(True, 1.0, 1.0)] * ATT_KV_HEADS
    ops += [(False, 1.0, 1.0)] * ATT_KV_HEADS
    return tuple(ops)


def kernel(x, c, ctx, c_ctx, ada_w, ada_b, norm_w, final_norm_w, ev_w_in, ev_w_out, ret_decay_raw, att_sink,
           od_w_in, od_w_out, hg_lb_logits, hg_norm_w, moe_wg, moe_bg, moe_we, moe_be, moe_w1, moe_w3, moe_w2):
    batch, seq, d = x.shape
    n_ctx = ctx.shape[1]
    depth = ada_w.shape[0]
    assert batch + 1 <= MOD_ROWS and n_ctx % ROW_TILE == 0 and seq % ROW_TILE == 0
    assert seq % GRID_W == 0 and seq >= 3 * CHUNK
    dims = (batch, n_ctx, seq)
    tb = n_ctx + seq
    nt = batch * tb

    xs = jnp.concatenate([ctx, x], axis=1).reshape(nt, d)
    cvec = jnp.concatenate([c, c_ctx[None, :], jnp.zeros((MOD_ROWS - batch - 1, d), F32)], axis=0)
    mods_all = _ada_modulation(cvec, ada_w, ada_b)
    rope = _rope_tables(n_ctx, seq)
    even_ops = _even_col_ops()

    for l in range(depth):
        p = l // 2
        mods = mods_all[l].reshape(MOD_ROWS, 1, 6 * d)
        if l % 2 == 0:
            proj = _in_projection(xs, norm_w[l, 0], mods, ev_w_in[p].astype(BF16), dims, even_ops, rope)
            ret = _retention(proj, ret_decay_raw[p], dims)
            att = _attention(proj, att_sink[p], dims, 4 * RET_HEADS)
            parts, w_out = [ret, att], ev_w_out[p]
        else:
            proj = _in_projection(xs, norm_w[l, 0], mods, od_w_in[p].astype(BF16), dims)
            parts, w_out = [_hgrn2(proj, hg_lb_logits, hg_norm_w[p], l, dims)], od_w_out[p]
        n_logit = MOE_GROUPS + MOE_EXPERTS
        wr = jnp.concatenate([moe_wg[l], jnp.moveaxis(moe_we[l], 0, 1).reshape(d, MOE_EXPERTS),
                              jnp.zeros((d, LANES - n_logit), F32)], axis=1)
        wr_hi = wr.astype(BF16)
        wr = jnp.concatenate([wr_hi, (wr - wr_hi.astype(F32)).astype(BF16)], axis=1)
        br = jnp.concatenate([moe_bg[l], moe_be[l].reshape(-1), jnp.zeros((LANES - n_logit,), F32)])[None, :]
        xs, h2, route, counts = _out_projection_route(
            parts, w_out.astype(BF16), xs, norm_w[l, 1], mods, wr, br, dims)
        plan = _dispatch_plan(route, counts, nt, d // LANES)
        y2 = _experts(h2, plan, moe_w1, moe_w3, moe_w2, l)
        xs = _combine(xs, y2, route, mods, dims, final_norm_w if l == depth - 1 else None)

    return xs.reshape(batch, tb, d)[:, n_ctx:]
```

```python
import functools

import numpy as np
import jax
import jax.numpy as jnp
from jax import lax
from jax.experimental import pallas as pl
from jax.experimental.pallas import tpu as pltpu

F32 = jnp.float32
BF16 = jnp.bfloat16
HIGHEST = lax.Precision.HIGHEST

EPS = 1e-6
NEG_INF = -1e30
LANES = 128
GRID_W = 64
ROPE_THETA = 10000.0
HEAD_DIM = 128
RET_HEADS = 4
ATT_Q_HEADS = 4
ATT_KV_HEADS = 2
ATT_GROUP = ATT_Q_HEADS // ATT_KV_HEADS
WINDOW = 128
HG_HEADS = 8
MOE_GROUPS = 4
MOE_EPG = 8
MOE_EXPERTS = MOE_GROUPS * MOE_EPG
MOE_TOP_K = 2
MOE_BLOCK = 128
CHUNK = 128
ROW_TILE = 256
MOD_ROWS = 8
VMEM_LIMIT = 56 * 1024 * 1024


def _cparams(n_axes, **kw):
    return pltpu.CompilerParams(dimension_semantics=("arbitrary",) * n_axes,
                                vmem_limit_bytes=VMEM_LIMIT, **kw)


def _dot(a, b):
    return jnp.dot(a, b, preferred_element_type=F32)


def _dot_nt(a, b):
    return lax.dot_general(a, b, (((1,), (1,)), ((), ())), preferred_element_type=F32)


def _dot_tn(a, b):
    return lax.dot_general(a, b, (((0,), (0,)), ((), ())), preferred_element_type=F32)


def _silu(x):
    return x * jax.nn.sigmoid(x)


def _norm_mod(x, nw, scale, shift):
    ms = jnp.mean(x * x, axis=-1, keepdims=True)
    y = x * lax.rsqrt(ms + EPS) * nw
    return y * (1.0 + scale) + shift


def _ada_kernel(c_ref, w_ref, b_ref, o_ref):
    s = _silu(c_ref[...])
    o_ref[0] = jnp.dot(s, w_ref[0], precision=HIGHEST, preferred_element_type=F32) + b_ref[0]


def _ada_modulation(cvec, ada_w, ada_b):
    depth, d, n6 = ada_w.shape
    tn = 1536
    return pl.pallas_call(
        _ada_kernel,
        out_shape=jax.ShapeDtypeStruct((depth, MOD_ROWS, n6), F32),
        grid=(depth, n6 // tn),
        in_specs=[pl.BlockSpec((MOD_ROWS, d), lambda l, j: (0, 0)),
                  pl.BlockSpec((1, d, tn), lambda l, j: (l, 0, j)),
                  pl.BlockSpec((1, 1, tn), lambda l, j: (l, 0, j))],
        out_specs=pl.BlockSpec((1, MOD_ROWS, tn), lambda l, j: (l, 0, j)),
        compiler_params=_cparams(2),
    )(cvec, ada_w, ada_b.reshape(depth, 1, n6))


def _swap_halves(x):
    lane = lax.broadcasted_iota(jnp.int32, x.shape, 1)
    return jnp.where((lane % 64) < 32, pltpu.roll(x, 96, 1), pltpu.roll(x, 32, 1))


def _proj_kernel(x_ref, nw_ref, shift_ref, scale_ref, w_ref, *rest, col_ops, chunk):
    if col_ops is None:
        (o_ref,) = rest
    else:
        cos_ref, sin_ref, o_ref = rest
    h = _norm_mod(x_ref[...], nw_ref[...], scale_ref[0], shift_ref[0]).astype(BF16)
    nout = o_ref.shape[0] * LANES
    for c0 in range(0, nout, chunk):
        acc = _dot(h, w_ref[:, c0:c0 + chunk])
        for hd in range(chunk // LANES):
            col = c0 + hd * LANES
            a = acc[:, hd * LANES:(hd + 1) * LANES]
            if col_ops is not None:
                rope, pre, post = col_ops[col // LANES]
                if pre != 1.0:
                    a = a * pre
                if rope:
                    a = a * cos_ref[...] + _swap_halves(a) * sin_ref[...]
                if post != 1.0:
                    a = a * post
            o_ref[col // LANES] = a


def _tile_mod_row(i, tiles_per_batch, ctx_tiles, batch):
    return jnp.where(i % tiles_per_batch < ctx_tiles, batch, i // tiles_per_batch)


def _mod_spec(chunk_idx, d, tiles_per_batch, ctx_tiles, batch):
    return pl.BlockSpec(
        (1, 1, d), lambda i: (_tile_mod_row(i, tiles_per_batch, ctx_tiles, batch), 0, chunk_idx))


def _in_projection(xs, nw, mods, w_bf, dims, col_ops=None, rope=None):
    nt, d = xs.shape
    nout = w_bf.shape[1]
    batch, n_ctx, seq = dims
    tpb = (n_ctx + seq) // ROW_TILE
    ctx_tiles = n_ctx // ROW_TILE
    in_specs = [pl.BlockSpec((ROW_TILE, d), lambda i: (i, 0)),
                pl.BlockSpec((1, d), lambda i: (0, 0)),
                _mod_spec(0, d, tpb, ctx_tiles, batch),
                _mod_spec(1, d, tpb, ctx_tiles, batch),
                pl.BlockSpec((d, nout), lambda i: (0, 0))]
    args = [xs, nw.reshape(1, d), mods, mods, w_bf]
    if col_ops is not None:
        rope_spec = pl.BlockSpec((ROW_TILE, LANES), lambda i: (i % tpb, 0))
        in_specs += [rope_spec, rope_spec]
        args += [rope[0], rope[1]]
    return pl.pallas_call(
        functools.partial(_proj_kernel, col_ops=col_ops, chunk=512),
        out_shape=jax.ShapeDtypeStruct((nout // LANES, nt, LANES), F32),
        grid=(nt // ROW_TILE,),
        in_specs=in_specs,
        out_specs=pl.BlockSpec((nout // LANES, ROW_TILE, LANES), lambda i: (0, i, 0)),
        compiler_params=_cparams(1),
    )(*args)


def _rope_tables(n_ctx, seq):
    n_rows = seq // GRID_W
    row = jnp.repeat(jnp.arange(n_rows, dtype=F32), GRID_W)
    col = jnp.tile(jnp.arange(GRID_W, dtype=F32), n_rows)
    axis_dim = HEAD_DIM // 2
    inv_freq = ROPE_THETA ** (-jnp.arange(0, axis_dim, 2, dtype=F32) / axis_dim)
    ang_r = row[:, None] * inv_freq[None, :]
    ang_c = col[:, None] * inv_freq[None, :]
    cos = jnp.concatenate([jnp.cos(ang_r), jnp.cos(ang_r), jnp.cos(ang_c), jnp.cos(ang_c)], axis=-1)
    sin = jnp.concatenate([-jnp.sin(ang_r), jnp.sin(ang_r), -jnp.sin(ang_c), jnp.sin(ang_c)], axis=-1)
    cos = jnp.concatenate([jnp.ones((n_ctx, HEAD_DIM), F32), cos], axis=0)
    sin = jnp.concatenate([jnp.zeros((n_ctx, HEAD_DIM), F32), sin], axis=0)
    return cos, sin


def _head_spec(rows, index_map):
    return pl.BlockSpec((None, rows, LANES), index_map)


def _backward_chunk(t, ctx_chunks, n_chunks):
    return jnp.where(t < ctx_chunks, ctx_chunks - 1 - t, n_chunks - 1 - (t - ctx_chunks))


def _ret_kernel(raw_ref, q_ref, k_ref, v_ref, g_ref, o_ref, of_scr, ob_scr, s_scr, dec_scr, qd_scr, kd_scr,
                *, ctx_chunks, n_chunks):
    h = pl.program_id(1)
    c = CHUNK
    ii = lax.broadcasted_iota(jnp.int32, (c, c), 0).astype(F32)
    jj = lax.broadcasted_iota(jnp.int32, (c, c), 1).astype(F32)
    lg = []
    for d in range(2):
        lgd = -jnp.exp(jnp.full((c, c), raw_ref[d, h], F32))
        lg.append(lgd)
        rel = (ii - jj) if d == 0 else (jj - ii)
        dec_scr[d] = jnp.where(rel >= 0, jnp.exp(jnp.maximum(rel, 0.0) * lgd), 0.0)
        qd_scr[d] = jnp.exp(((ii + 1.0) if d == 0 else (c - ii)) * lgd)
        kd_scr[d] = jnp.exp(((c - 1.0 - ii) if d == 0 else ii) * lgd)

    def chunk(n, d):
        rows = pl.ds(pl.multiple_of(n * c, c), c)
        q = q_ref[rows, :]
        k = k_ref[rows, :]
        vb = v_ref[rows, :].astype(BF16)
        sc = _dot_nt(q.astype(BF16), k.astype(BF16)) * dec_scr[d]
        s = s_scr[d]
        o = _dot(sc.astype(BF16), vb) + _dot((q * qd_scr[d]).astype(BF16), s.astype(BF16))
        chunk_decay = jnp.exp(float(c) * lg[d][0:1, :])
        s_scr[d] = chunk_decay * s + _dot_tn((k * kd_scr[d]).astype(BF16), vb)
        return rows, o

    s_scr[...] = jnp.zeros_like(s_scr)

    def scan(t, carry):
        rows, o = chunk(t, 0)
        of_scr[rows, :] = o
        rows, o = chunk(_backward_chunk(t, ctx_chunks, n_chunks), 1)
        ob_scr[rows, :] = o
        return carry

    lax.fori_loop(0, n_chunks, scan, 0, unroll=2)

    def readout(n, carry):
        rows = pl.ds(pl.multiple_of(n * c, c), c)
        o = of_scr[rows, :] + ob_scr[rows, :]
        y = o * lax.rsqrt(jnp.mean(o * o, axis=-1, keepdims=True) + EPS)
        o_ref[rows, :] = (_silu(g_ref[rows, :]) * y).astype(BF16)
        return carry

    lax.fori_loop(0, n_chunks, readout, 0)


def _retention(proj, ret_decay_raw, dims):
    batch, n_ctx, seq = dims
    tb = n_ctx + seq
    spec = lambda off: _head_spec(tb, lambda b, h: (off + h, b, 0))
    return pl.pallas_call(
        functools.partial(_ret_kernel, ctx_chunks=n_ctx // CHUNK, n_chunks=tb // CHUNK),
        out_shape=jax.ShapeDtypeStruct((RET_HEADS, batch * tb, LANES), BF16),
        grid=(batch, RET_HEADS),
        in_specs=[pl.BlockSpec(memory_space=pltpu.SMEM),
                  spec(0), spec(RET_HEADS), spec(2 * RET_HEADS), spec(3 * RET_HEADS)],
        out_specs=_head_spec(tb, lambda b, h: (h, b, 0)),
        scratch_shapes=[pltpu.VMEM((tb, LANES), F32), pltpu.VMEM((tb, LANES), F32),
                        pltpu.VMEM((2, LANES, LANES), F32),
                        pltpu.VMEM((2, CHUNK, CHUNK), F32), pltpu.VMEM((2, CHUNK, LANES), F32),
                        pltpu.VMEM((2, CHUNK, LANES), F32)],
        compiler_params=_cparams(2),
    )(ret_decay_raw, proj, proj, proj, proj)


def _att_kernel(sink_ref, q0_ref, q1_ref, k_ref, v_ref, o_ref, kb_scr, vb_scr, bias_scr, *, n_ctx, seq):
    hk = pl.program_id(1)
    c = CHUNK
    win = 3 * c
    kb_scr[...] = k_ref[...].astype(BF16)
    vb_scr[...] = v_ref[...].astype(BF16)
    kc = kb_scr[0:n_ctx, :]
    vc = vb_scr[0:n_ctx, :]
    ii = lax.broadcasted_iota(jnp.int32, (c, win), 0)
    jj = lax.broadcasted_iota(jnp.int32, (c, win), 1)
    for off in range(3):
        bias_scr[off] = jnp.where(jnp.abs(ii - jj + off * c) <= WINDOW, 0.0, NEG_INF)

    def softmax_out(s_ctx, sink, s_loc=None, v_loc=None):
        m = jnp.maximum(jnp.max(s_ctx, axis=-1, keepdims=True), sink)
        if s_loc is not None:
            m = jnp.maximum(m, jnp.max(s_loc, axis=-1, keepdims=True))
        p_ctx = jnp.exp(s_ctx - m)
        den = jnp.sum(p_ctx, axis=-1, keepdims=True) + jnp.exp(sink - m)
        o = _dot(p_ctx.astype(BF16), vc)
        if s_loc is not None:
            p_loc = jnp.exp(s_loc - m)
            den = den + jnp.sum(p_loc, axis=-1, keepdims=True)
            o = o + _dot(p_loc.astype(BF16), v_loc)
        return o / den

    heads = []
    for g, q_ref in enumerate((q0_ref, q1_ref)):
        sink = jnp.full((c, 1), sink_ref[hk * ATT_GROUP + g], F32)
        heads.append((q_ref, sink, g))

    for q_ref, sink, g in heads:
        for cc in range(n_ctx // c):
            qb = q_ref[cc * c:(cc + 1) * c, :].astype(BF16)
            o_ref[g, cc * c:(cc + 1) * c, :] = softmax_out(_dot_nt(qb, kc), sink).astype(BF16)

    def block(n, carry):
        q_rows = pl.ds(pl.multiple_of(n_ctx + n * c, c), c)
        start = jnp.clip(n * c - c, 0, seq - win)
        k_rows = pl.ds(pl.multiple_of(n_ctx + start, c), win)
        k_loc = kb_scr[k_rows, :]
        v_loc = vb_scr[k_rows, :]
        bias = bias_scr[(n * c - start) // c]
        for q_ref, sink, g in heads:
            qb = q_ref[q_rows, :].astype(BF16)
            o = softmax_out(_dot_nt(qb, kc), sink, _dot_nt(qb, k_loc) + bias, v_loc)
            o_ref[g, q_rows, :] = o.astype(BF16)
        return carry

    lax.fori_loop(0, seq // c, block, 0)


def _attention(proj, att_sink, dims, col0):
    batch, n_ctx, seq = dims
    tb = n_ctx + seq
    qb, kb, vb = col0, col0 + ATT_Q_HEADS, col0 + ATT_Q_HEADS + ATT_KV_HEADS
    return pl.pallas_call(
        functools.partial(_att_kernel, n_ctx=n_ctx, seq=seq),
        out_shape=jax.ShapeDtypeStruct((ATT_Q_HEADS, batch * tb, LANES), BF16),
        grid=(batch, ATT_KV_HEADS),
        in_specs=[pl.BlockSpec(memory_space=pltpu.SMEM),
                  _head_spec(tb, lambda b, h: (qb + ATT_GROUP * h, b, 0)),
                  _head_spec(tb, lambda b, h: (qb + ATT_GROUP * h + 1, b, 0)),
                  _head_spec(tb, lambda b, h: (kb + h, b, 0)),
                  _head_spec(tb, lambda b, h: (vb + h, b, 0))],
        out_specs=pl.BlockSpec((ATT_GROUP, tb, LANES), lambda b, h: (h, b, 0)),
        scratch_shapes=[pltpu.VMEM((tb, LANES), BF16), pltpu.VMEM((tb, LANES), BF16),
                        pltpu.VMEM((3, CHUNK, 3 * CHUNK), F32)],
        compiler_params=_cparams(2),
    )(att_sink, proj, proj, proj, proj)


HG_LEVELS = 7
HG_MXU_LEVELS = 3


def _hgrn_constants():
    c = CHUNK
    i = np.arange(c)[:, None]
    r = np.arange(c)[None, :]
    mats = [r <= i]
    for m in range(1, HG_MXU_LEVELS + 1):
        half = 1 << (m - 1)
        beta = (i // (2 * half)) * (2 * half) + half - 1
        upper = (i % (2 * half)) >= half
        mats.append((upper & (r > beta) & (r <= i)) | ((~upper) & (r > i) & (r <= beta)))
    fwd = [m.astype(np.float32) for m in mats]
    bwd = [m[::-1, ::-1] for m in fwd]
    stack = np.stack([np.concatenate(fwd, 0), np.concatenate(bwd, 0)])
    x = i ^ r
    level = np.where(r > i, -1, np.where(r == i, 0, np.floor(np.log2(np.maximum(x, 1))) + 1)).astype(np.int32)
    levels = np.stack([level, level[::-1, ::-1]])
    return stack, levels


def _hgrn_kernel(q_ref, zf_ref, zb_ref, v_ref, g_ref, lbl_ref, nw_ref, m_ref, lv_ref, o_ref,
                 of_scr, ob_scr, st_scr, *, layer, ctx_chunks, n_chunks):
    c = CHUNK
    z = lbl_ref[...]
    e = jnp.exp(z - jnp.max(z, axis=0, keepdims=True))
    sm = e / jnp.sum(e, axis=0, keepdims=True)
    lb = jnp.sum(sm[1:layer + 1], axis=0, keepdims=True)
    row = lax.broadcasted_iota(jnp.int32, (c, LANES), 0)
    upper = [[((row if d == 0 else c - 1 - row) & (1 << (m - 1))) != 0 for m in range(1, HG_MXU_LEVELS + 1)]
             for d in range(2)]

    def level_operand(q, k, a, half, d):
        pieces = []
        for base in range(0, c, 2 * half):
            lo, hi = slice(base, base + half), slice(base + half, base + 2 * half)
            if d == 0:
                ref = a[base + half - 1:base + half]
                pieces += [k[lo] * jnp.exp(ref - a[lo]), q[hi] * jnp.exp(a[hi] - ref)]
            else:
                ref = a[base + half:base + half + 1]
                pieces += [q[lo] * jnp.exp(a[lo] - ref), k[hi] * jnp.exp(ref - a[hi])]
        return jnp.concatenate(pieces, axis=0).astype(BF16)

    def chunk(n, d, z_ref):
        rows = pl.ds(pl.multiple_of(n * c, c), c)
        q = q_ref[rows, :]
        v = v_ref[rows, :]
        f = lb + (1.0 - lb) * jax.nn.sigmoid(z_ref[rows, :])
        g = jnp.log(f)
        k = 1.0 - f
        g_hi = g.astype(BF16)
        g_lo = (g - g_hi.astype(F32)).astype(BF16)
        res = _dot(m_ref[d], jnp.concatenate([g_hi, g_lo], axis=1))
        ex = res[:, :c] + res[:, c:]
        a = ex[0:c]
        st = st_scr[d]
        o = _dot_nt((q * jnp.exp(a)).astype(BF16), st.astype(BF16))
        lv = lv_ref[d]
        sc = jnp.where(lv == 0, _dot_nt(q.astype(BF16), k.astype(BF16)), 0.0)
        for m in range(1, HG_LEVELS + 1):
            if m <= HG_MXU_LEVELS:
                zz = (jnp.where(upper[d][m - 1], q, k) * jnp.exp(ex[m * c:(m + 1) * c])).astype(BF16)
            else:
                zz = level_operand(q, k, a, 1 << (m - 1), d)
            sc = jnp.where(lv == m, _dot_nt(zz, zz), sc)
        vb = v.astype(BF16)
        o = o + _dot(sc.astype(BF16), vb)
        a_last = a[c - 1:c] if d == 0 else a[0:1]
        kd = (k * jnp.exp(a_last - a)).astype(BF16)
        st_scr[d] = st * jnp.exp(a_last) + _dot_tn(vb, kd)
        return rows, o

    st_scr[...] = jnp.zeros_like(st_scr)

    def scan(t, carry):
        rows, o = chunk(t, 0, zf_ref)
        of_scr[rows, :] = o
        rows, o = chunk(_backward_chunk(t, ctx_chunks, n_chunks), 1, zb_ref)
        ob_scr[rows, :] = o
        return carry

    lax.fori_loop(0, n_chunks, scan, 0, unroll=2)

    def readout(n, carry):
        rows = pl.ds(pl.multiple_of(n * c, c), c)
        o = of_scr[rows, :] + ob_scr[rows, :]
        y = o * lax.rsqrt(jnp.mean(o * o, axis=-1, keepdims=True) + EPS) * nw_ref[...]
        o_ref[rows, :] = (y * _silu(g_ref[rows, :])).astype(BF16)
        return carry

    lax.fori_loop(0, n_chunks, readout, 0)


def _hgrn2(proj, hg_lb_logits, hg_norm_w, layer, dims):
    batch, n_ctx, seq = dims
    tb = n_ctx + seq
    depth = hg_lb_logits.shape[0]
    stack, levels = _hgrn_constants()
    spec = lambda off: _head_spec(tb, lambda b, h: (off + h, b, 0))
    n_mats = 1 + HG_MXU_LEVELS
    return pl.pallas_call(
        functools.partial(_hgrn_kernel, layer=layer, ctx_chunks=n_ctx // CHUNK, n_chunks=tb // CHUNK),
        out_shape=jax.ShapeDtypeStruct((HG_HEADS, batch * tb, LANES), BF16),
        grid=(batch, HG_HEADS),
        in_specs=[spec(0), spec(HG_HEADS), spec(2 * HG_HEADS), spec(3 * HG_HEADS), spec(4 * HG_HEADS),
                  pl.BlockSpec((depth, LANES), lambda b, h: (0, h)),
                  pl.BlockSpec((1, LANES), lambda b, h: (0, 0)),
                  pl.BlockSpec((2, n_mats * CHUNK, CHUNK), lambda b, h: (0, 0, 0)),
                  pl.BlockSpec((2, CHUNK, CHUNK), lambda b, h: (0, 0, 0))],
        out_specs=_head_spec(tb, lambda b, h: (h, b, 0)),
        scratch_shapes=[pltpu.VMEM((tb, LANES), F32), pltpu.VMEM((tb, LANES), F32),
                        pltpu.VMEM((2, LANES, LANES), F32)],
        compiler_params=_cparams(2),
    )(proj, proj, proj, proj, proj, hg_lb_logits, hg_norm_w.reshape(1, LANES),
      jnp.asarray(stack, BF16), jnp.asarray(levels, jnp.int32))


SUBLANES = 8


def _store_token_tiles(ref, x):
    rows, tiles = x.shape[0], x.shape[1] // LANES
    for s in range(tiles):
        ref[pl.ds(s, rows, stride=tiles), :] = x[:, s * LANES:(s + 1) * LANES]


def _load_token_tiles(ref, tiles):
    rows = ref.shape[0] // tiles
    return jnp.concatenate([ref[pl.ds(s, rows, stride=tiles), :] for s in range(tiles)], axis=1)


def _route(logits, tri, counts):
    lane = lax.broadcasted_iota(jnp.int32, logits.shape, 1).astype(F32)
    big = float(LANES)
    first = lambda hit: jnp.min(jnp.where(hit, lane, big), axis=-1, keepdims=True)
    is_g = lane < MOE_GROUPS
    gl = jnp.where(is_g, logits, NEG_INF)
    gmax = jnp.max(gl, axis=-1, keepdims=True)
    gsum = jnp.sum(jnp.where(is_g, jnp.exp(gl - gmax), 0.0), axis=-1, keepdims=True)
    g_p = 1.0 / gsum
    lo = MOE_GROUPS + first(gl == gmax) * MOE_EPG
    el = jnp.where((lane >= lo) & (lane < lo + MOE_EPG), logits, NEG_INF)
    e1v = jnp.max(el, axis=-1, keepdims=True)
    e1 = first(el == e1v)
    el2 = jnp.where(lane == e1, NEG_INF, el)
    e2v = jnp.max(el2, axis=-1, keepdims=True)
    e2 = first(el2 == e2v)
    t = jnp.exp(e2v - e1v)
    w1 = 1.0 / (1.0 + t)
    w2 = t * w1
    id1 = e1 - MOE_GROUPS
    id2 = e2 - MOE_GROUPS
    onehot = jnp.where((lane == id1) | (lane == id2), 1.0, 0.0)
    before = counts + _dot(tri, onehot.astype(BF16))
    rank1 = jnp.sum(jnp.where(lane == id1, before, 0.0), axis=-1, keepdims=True)
    rank2 = jnp.sum(jnp.where(lane == id2, before, 0.0), axis=-1, keepdims=True)
    slab = jnp.zeros_like(logits)
    for col, val in enumerate((id1, id2, g_p * w1, g_p * w2, rank1, rank2)):
        slab = jnp.where(lane == col, val, slab)
    return slab, counts + jnp.sum(onehot, axis=0, keepdims=True)


def _out_route_kernel(*refs, n_parts):
    a_refs = refs[:n_parts]
    (w_ref, x_ref, gate_ref, nw_ref, shift_ref, scale_ref, wr_ref, br_ref, tri_ref,
     xo_ref, h2_ref, route_ref, cnt_ref) = refs[n_parts:]

    @pl.when(pl.program_id(0) == 0)
    def _():
        cnt_ref[...] = jnp.zeros_like(cnt_ref)

    k0 = 0
    y = None
    for a_ref in a_refs:
        a = jnp.concatenate([a_ref[h] for h in range(a_ref.shape[0])], axis=1)
        kk = a.shape[1]
        t = _dot(a, w_ref[k0:k0 + kk, :])
        y = t if y is None else y + t
        k0 += kk
    x = x_ref[...] + gate_ref[0] * y
    xo_ref[...] = x
    h2 = _norm_mod(x, nw_ref[...], scale_ref[0], shift_ref[0])
    _store_token_tiles(h2_ref, h2)
    h_hi = h2.astype(BF16)
    h_lo = (h2 - h_hi.astype(F32)).astype(BF16)
    both = _dot(h_hi, wr_ref[...])
    logits = both[:, :LANES] + both[:, LANES:] + _dot(h_lo, wr_ref[:, :LANES]) + br_ref[...]
    slab, counts = _route(logits, tri_ref[...], cnt_ref[0:1, :])
    route_ref[...] = slab
    cnt_ref[...] = jnp.broadcast_to(counts, cnt_ref.shape)


def _out_projection_route(parts, w_bf, xs, nw, mods, wr, br, dims):
    nt, d = xs.shape
    batch, n_ctx, seq = dims
    tpb = (n_ctx + seq) // ROW_TILE
    ctx_tiles = n_ctx // ROW_TILE
    row = lambda width: pl.BlockSpec((ROW_TILE, width), lambda i: (i, 0))
    full = lambda shape: pl.BlockSpec(shape, lambda i: (0,) * len(shape))
    mod = lambda chunk_idx: _mod_spec(chunk_idx, d, tpb, ctx_tiles, batch)
    tri = jnp.asarray(np.tril(np.ones((ROW_TILE, ROW_TILE), np.float32), -1), BF16)
    return pl.pallas_call(
        functools.partial(_out_route_kernel, n_parts=len(parts)),
        out_shape=(jax.ShapeDtypeStruct((nt, d), F32), jax.ShapeDtypeStruct((nt * d // LANES, LANES), F32),
                   jax.ShapeDtypeStruct((nt, LANES), F32), jax.ShapeDtypeStruct((8, LANES), F32)),
        grid=(nt // ROW_TILE,),
        in_specs=[pl.BlockSpec((p.shape[0], ROW_TILE, LANES), lambda i: (0, i, 0)) for p in parts] + [
            full(w_bf.shape), row(d), mod(2), full((1, d)), mod(3), mod(4), full(wr.shape), full(br.shape),
            full(tri.shape)],
        out_specs=(row(d), pl.BlockSpec((ROW_TILE * d // LANES, LANES), lambda i: (i, 0)), row(LANES),
                   full((8, LANES))),
        compiler_params=_cparams(1),
    )(*parts, w_bf, xs, mods, nw.reshape(1, d), mods, mods, wr, br, tri)


SCATTER_DMA_PRIORITY = 1
MOE_BUFFERS = 3


def _dispatch_plan(route, counts_f, n_tok, tiles):
    n_assign = n_tok * MOE_TOP_K
    flat_e = route[:, 0:MOE_TOP_K].astype(jnp.int32).reshape(-1)
    rank = route[:, 4:4 + MOE_TOP_K].astype(jnp.int32).reshape(-1)
    counts = counts_f[0, :MOE_EXPERTS].astype(jnp.int32)
    padded = (counts + MOE_BLOCK - 1) // MOE_BLOCK * MOE_BLOCK
    pad_end = jnp.cumsum(padded)
    pad_start = pad_end - padded
    dest = pad_start[flat_e] + rank
    n_blocks = -(-n_assign // MOE_BLOCK) + MOE_EXPERTS
    n_slots = n_blocks * MOE_BLOCK
    block_start = jnp.arange(n_blocks, dtype=jnp.int32) * MOE_BLOCK
    block_expert = jnp.minimum(jnp.sum((pad_end[None, :] <= block_start[:, None]).astype(jnp.int32), axis=1),
                               MOE_EXPERTS - 1)
    slot_assign = jnp.full((n_slots,), -1, jnp.int32).at[dest].set(jnp.arange(n_assign, dtype=jnp.int32))
    lead, trail = MOE_BLOCK, (MOE_BUFFERS - 1) * MOE_BLOCK
    slot = jnp.arange(-lead, n_slots + trail, dtype=jnp.int32)
    spare = n_assign + ((slot // MOE_BLOCK) % MOE_BUFFERS) * MOE_BLOCK + slot % MOE_BLOCK
    assign = jnp.concatenate([jnp.full((lead,), -1, jnp.int32), slot_assign, jnp.full((trail,), -1, jnp.int32)])
    src = jnp.where(assign >= 0, assign // MOE_TOP_K, 0) * tiles
    dst = jnp.where(assign >= 0, (assign % MOE_TOP_K) * n_tok + assign // MOE_TOP_K, spare) * tiles
    n_used = (pad_end[-1:] // MOE_BLOCK).astype(jnp.int32)
    return block_expert, n_used, src[lead:], dst[:n_slots + lead], n_blocks


def _expert_kernel(be_ref, nused_ref, src_ref, dst_ref, h_hbm, w1_ref, w3_ref, w2_ref, out_hbm,
                   xbuf, ybuf, w1b, w3b, w2b, gsem, ssem):
    i = pl.program_id(0)
    n_used = nused_ref[0]
    nb = MOE_BUFFERS
    cur = i % nb
    buffer_of = lambda blk: (blk + nb) % nb

    tiles = xbuf.shape[1] // MOE_BLOCK

    def token(idx):
        return pl.ds(pl.multiple_of(idx, tiles), tiles)

    def start_gather(blk, buf):
        for r in range(MOE_BLOCK):
            pltpu.make_async_copy(h_hbm.at[token(src_ref[blk * MOE_BLOCK + r])],
                                  xbuf.at[buf, pl.ds(r * tiles, tiles)], gsem.at[buf]).start()

    def start_scatter(blk, buf):
        for r in range(MOE_BLOCK):
            pltpu.make_async_copy(ybuf.at[buf, pl.ds(r * tiles, tiles)],
                                  out_hbm.at[token(dst_ref[(blk + 1) * MOE_BLOCK + r])], ssem.at[buf]).start(
                                      priority=SCATTER_DMA_PRIORITY)

    def wait_gather(buf):
        pltpu.make_async_copy(h_hbm.at[pl.ds(0, MOE_BLOCK * tiles)], xbuf.at[buf], gsem.at[buf]).wait()

    def wait_scatter(buf):
        pltpu.make_async_copy(ybuf.at[buf], out_hbm.at[pl.ds(0, MOE_BLOCK * tiles)], ssem.at[buf]).wait()

    @pl.when(i < n_used)
    def _():
        @pl.when(i == 0)
        def _():
            ybuf[...] = jnp.zeros_like(ybuf)
            block_rows = MOE_BLOCK * tiles
            spare0 = out_hbm.shape[0] - nb * block_rows
            for b in range(nb):
                spare = pltpu.make_async_copy(ybuf.at[b], out_hbm.at[pl.ds(spare0 + b * block_rows, block_rows)],
                                              ssem.at[b])
                spare.start()
                spare.wait()
            for blk in range(nb - 1):
                start_gather(blk, blk)

        @pl.when((i == 0) | (be_ref[i] != be_ref[jnp.maximum(i - 1, 0)]))
        def _():
            w1b[...] = w1_ref[0, 0].astype(BF16)
            w3b[...] = w3_ref[0, 0].astype(BF16)
            w2b[...] = w2_ref[0, 0].astype(BF16)

        wait_gather(cur)

        @pl.when(i >= nb - 1)
        def _():
            wait_scatter(cur)

        start_gather(i + nb - 1, buffer_of(i - 1))
        start_scatter(i - 1, buffer_of(i - 1))
        xb = _load_token_tiles(xbuf.at[cur], tiles).astype(BF16)
        act = _silu(_dot(xb, w1b[...])) * _dot(xb, w3b[...])
        _store_token_tiles(ybuf.at[cur], _dot(act.astype(BF16), w2b[...]))

        @pl.when(i == n_used - 1)
        def _():
            start_scatter(i, cur)
            for back in range(nb):
                @pl.when(i - back >= -1)
                def _():
                    wait_scatter(buffer_of(i - back))
            for ahead in range(1, nb):
                wait_gather(buffer_of(i + ahead))


def _experts(h2, plan, w1, w3, w2, layer):
    block_expert, n_used, slot_src, slot_dst, n_blocks = plan
    d, ff = w1.shape[2], w1.shape[3]
    tiles = d // LANES
    nt = h2.shape[0] // tiles
    wspec = lambda shape: pl.BlockSpec((1, 1) + shape, lambda i, be, nu, sr, ds: (layer, be[i], 0, 0))
    row_buffers = pltpu.VMEM((MOE_BUFFERS, MOE_BLOCK * tiles, LANES), F32)
    grid_spec = pltpu.PrefetchScalarGridSpec(
        num_scalar_prefetch=4,
        grid=(n_blocks,),
        in_specs=[pl.BlockSpec(memory_space=pl.ANY), wspec((d, ff)), wspec((d, ff)), wspec((ff, d))],
        out_specs=pl.BlockSpec(memory_space=pl.ANY),
        scratch_shapes=[row_buffers, row_buffers,
                        pltpu.VMEM((d, ff), BF16), pltpu.VMEM((d, ff), BF16), pltpu.VMEM((ff, d), BF16),
                        pltpu.SemaphoreType.DMA((MOE_BUFFERS,)), pltpu.SemaphoreType.DMA((MOE_BUFFERS,))])
    return pl.pallas_call(
        _expert_kernel,
        out_shape=jax.ShapeDtypeStruct(((nt * MOE_TOP_K + MOE_BUFFERS * MOE_BLOCK) * tiles, LANES), F32),
        grid_spec=grid_spec,
        compiler_params=_cparams(1, has_side_effects=True, disable_bounds_checks=True),
    )(block_expert, n_used, slot_src, slot_dst, h2, w1, w3, w2)


def _combine_kernel(x_ref, y0_ref, y1_ref, route_ref, gate_ref, *rest, final):
    r = route_ref[...]
    tiles = x_ref.shape[1] // LANES
    y = r[:, 2:3] * _load_token_tiles(y0_ref, tiles) + r[:, 3:4] * _load_token_tiles(y1_ref, tiles)
    x = x_ref[...] + gate_ref[0] * y
    if final:
        fnw_ref, o_ref = rest
        x = x * lax.rsqrt(jnp.mean(x * x, axis=-1, keepdims=True) + EPS) * fnw_ref[...]
    else:
        (o_ref,) = rest
    o_ref[...] = x


def _combine(xs, y2, route, mods, dims, final_norm_w=None):
    nt, d = xs.shape
    batch, n_ctx, seq = dims
    tpb = (n_ctx + seq) // ROW_TILE
    ctx_tiles = n_ctx // ROW_TILE
    row = lambda width: pl.BlockSpec((ROW_TILE, width), lambda i: (i, 0))
    final = final_norm_w is not None
    first = pl.BlockSpec((ROW_TILE * d // LANES, LANES), lambda i: (i, 0))
    second = pl.BlockSpec((ROW_TILE * d // LANES, LANES), lambda i: (i + nt // ROW_TILE, 0))
    in_specs = [row(d), first, second, row(LANES), _mod_spec(5, d, tpb, ctx_tiles, batch)]
    args = [xs, y2, y2, route, mods]
    if final:
        in_specs.append(pl.BlockSpec((1, d), lambda i: (0, 0)))
        args.append(final_norm_w.reshape(1, d))
    return pl.pallas_call(
        functools.partial(_combine_kernel, final=final),
        out_shape=jax.ShapeDtypeStruct((nt, d), F32),
        grid=(nt // ROW_TILE,),
        in_specs=in_specs,
        out_specs=row(d),
        compiler_params=_cparams(1),
    )(*args)


def _even_col_ops():
    rh = RET_HEADS
    ops = [(True, 1.0, 1.0)] * rh + [(True, float(HEAD_DIM) ** -0.5, 1.0)] * rh + [(False, 1.0, 1.0)] * (2 * rh)
    ops += [(True, 1.0, float(HEAD_DIM) ** -0.5)] * ATT_Q_HEADS + [(True, 1.0, 1.0)] * ATT_KV_HEADS
    ops += [(False, 1.0, 1.0)] * ATT_KV_HEADS
    return tuple(ops)


def kernel(x, c, ctx, c_ctx, ada_w, ada_b, norm_w, final_norm_w, ev_w_in, ev_w_out, ret_decay_raw, att_sink,
           od_w_in, od_w_out, hg_lb_logits, hg_norm_w, moe_wg, moe_bg, moe_we, moe_be, moe_w1, moe_w3, moe_w2):
    batch, seq, d = x.shape
    n_ctx = ctx.shape[1]
    depth = ada_w.shape[0]
    assert batch + 1 <= MOD_ROWS and n_ctx % ROW_TILE == 0 and seq % ROW_TILE == 0
    assert seq % GRID_W == 0 and seq >= 3 * CHUNK
    dims = (batch, n_ctx, seq)
    tb = n_ctx + seq
    nt = batch * tb

    xs = jnp.concatenate([ctx, x], axis=1).reshape(nt, d)
    cvec = jnp.concatenate([c, c_ctx[None, :], jnp.zeros((MOD_ROWS - batch - 1, d), F32)], axis=0)
    mods_all = _ada_modulation(cvec, ada_w, ada_b)
    rope = _rope_tables(n_ctx, seq)
    even_ops = _even_col_ops()

    for l in range(depth):
        p = l // 2
        mods = mods_all[l].reshape(MOD_ROWS, 1, 6 * d)
        if l % 2 == 0:
            proj = _in_projection(xs, norm_w[l, 0], mods, ev_w_in[p].astype(BF16), dims, even_ops, rope)
            ret = _retention(proj, ret_decay_raw[p], dims)
            att = _attention(proj, att_sink[p], dims, 4 * RET_HEADS)
            parts, w_out = [ret, att], ev_w_out[p]
        else:
            proj = _in_projection(xs, norm_w[l, 0], mods, od_w_in[p].astype(BF16), dims)
            parts, w_out = [_hgrn2(proj, hg_lb_logits, hg_norm_w[p], l, dims)], od_w_out[p]
        n_logit = MOE_GROUPS + MOE_EXPERTS
        wr = jnp.concatenate([moe_wg[l], jnp.moveaxis(moe_we[l], 0, 1).reshape(d, MOE_EXPERTS),
                              jnp.zeros((d, LANES - n_logit), F32)], axis=1)
        wr_hi = wr.astype(BF16)
        wr = jnp.concatenate([wr_hi, (wr - wr_hi.astype(F32)).astype(BF16)], axis=1)
        br = jnp.concatenate([moe_bg[l], moe_be[l].reshape(-1), jnp.zeros((LANES - n_logit,), F32)])[None, :]
        xs, h2, route, counts = _out_projection_route(
            parts, w_out.astype(BF16), xs, norm_w[l, 1], mods, wr, br, dims)
        plan = _dispatch_plan(route, counts, nt, d // LANES)
        y2 = _experts(h2, plan, moe_w1, moe_w3, moe_w2, l)
        xs = _combine(xs, y2, route, mods, dims, final_norm_w if l == depth - 1 else None)

    return xs.reshape(batch, tb, d)[:, n_ctx:]
```

```python
import functools

import numpy as np
import jax
import jax.numpy as jnp
from jax import lax
from jax.experimental import pallas as pl
from jax.experimental.pallas import tpu as pltpu

F32 = jnp.float32
BF16 = jnp.bfloat16
HIGHEST = lax.Precision.HIGHEST

EPS = 1e-6
NEG_INF = -1e30
LANES = 128
GRID_W = 64
ROPE_THETA = 10000.0
HEAD_DIM = 128
RET_HEADS = 4
ATT_Q_HEADS = 4
ATT_KV_HEADS = 2
ATT_GROUP = ATT_Q_HEADS // ATT_KV_HEADS
WINDOW = 128
HG_HEADS = 8
MOE_GROUPS = 4
MOE_EPG = 8
MOE_EXPERTS = MOE_GROUPS * MOE_EPG
MOE_TOP_K = 2
MOE_BLOCK = 128
CHUNK = 128
ROW_TILE = 256
MOD_ROWS = 8
VMEM_LIMIT = 56 * 1024 * 1024


def _cparams(n_axes, **kw):
    return pltpu.CompilerParams(dimension_semantics=("arbitrary",) * n_axes,
                                vmem_limit_bytes=VMEM_LIMIT, **kw)


def _dot(a, b):
    return jnp.dot(a, b, preferred_element_type=F32)


def _dot_nt(a, b):
    return lax.dot_general(a, b, (((1,), (1,)), ((), ())), preferred_element_type=F32)


def _dot_tn(a, b):
    return lax.dot_general(a, b, (((0,), (0,)), ((), ())), preferred_element_type=F32)


def _silu(x):
    return x * jax.nn.sigmoid(x)


def _norm_mod(x, nw, scale, shift):
    ms = jnp.mean(x * x, axis=-1, keepdims=True)
    y = x * lax.rsqrt(ms + EPS) * nw
    return y * (1.0 + scale) + shift


def _ada_kernel(c_ref, w_ref, b_ref, o_ref):
    s = _silu(c_ref[...])
    o_ref[0] = jnp.dot(s, w_ref[0], precision=HIGHEST, preferred_element_type=F32) + b_ref[0]


def _ada_modulation(cvec, ada_w, ada_b):
    depth, d, n6 = ada_w.shape
    tn = 1536
    return pl.pallas_call(
        _ada_kernel,
        out_shape=jax.ShapeDtypeStruct((depth, MOD_ROWS, n6), F32),
        grid=(depth, n6 // tn),
        in_specs=[pl.BlockSpec((MOD_ROWS, d), lambda l, j: (0, 0)),
                  pl.BlockSpec((1, d, tn), lambda l, j: (l, 0, j)),
                  pl.BlockSpec((1, 1, tn), lambda l, j: (l, 0, j))],
        out_specs=pl.BlockSpec((1, MOD_ROWS, tn), lambda l, j: (l, 0, j)),
        compiler_params=_cparams(2),
    )(cvec, ada_w, ada_b.reshape(depth, 1, n6))


def _swap_halves(x):
    lane = lax.broadcasted_iota(jnp.int32, x.shape, 1)
    return jnp.where((lane % 64) < 32, pltpu.roll(x, 96, 1), pltpu.roll(x, 32, 1))


def _proj_kernel(x_ref, nw_ref, shift_ref, scale_ref, w_ref, *rest, col_ops, chunk):
    if col_ops is None:
        (o_ref,) = rest
    else:
        cos_ref, sin_ref, o_ref = rest
    h = _norm_mod(x_ref[...], nw_ref[...], scale_ref[0], shift_ref[0]).astype(BF16)
    nout = o_ref.shape[0] * LANES
    for c0 in range(0, nout, chunk):
        acc = _dot(h, w_ref[:, c0:c0 + chunk])
        for hd in range(chunk // LANES):
            col = c0 + hd * LANES
            a = acc[:, hd * LANES:(hd + 1) * LANES]
            if col_ops is not None:
                rope, pre, post = col_ops[col // LANES]
                if pre != 1.0:
                    a = a * pre
                if rope:
                    a = a * cos_ref[...] + _swap_halves(a) * sin_ref[...]
                if post != 1.0:
                    a = a * post
            o_ref[col // LANES] = a


def _tile_mod_row(i, tiles_per_batch, ctx_tiles, batch):
    return jnp.where(i % tiles_per_batch < ctx_tiles, batch, i // tiles_per_batch)


def _mod_spec(chunk_idx, d, tiles_per_batch, ctx_tiles, batch):
    return pl.BlockSpec(
        (1, 1, d), lambda i: (_tile_mod_row(i, tiles_per_batch, ctx_tiles, batch), 0, chunk_idx))


def _in_projection(xs, nw, mods, w_bf, dims, col_ops=None, rope=None):
    nt, d = xs.shape
    nout = w_bf.shape[1]
    batch, n_ctx, seq = dims
    tpb = (n_ctx + seq) // ROW_TILE
    ctx_tiles = n_ctx // ROW_TILE
    in_specs = [pl.BlockSpec((ROW_TILE, d), lambda i: (i, 0)),
                pl.BlockSpec((1, d), lambda i: (0, 0)),
                _mod_spec(0, d, tpb, ctx_tiles, batch),
                _mod_spec(1, d, tpb, ctx_tiles, batch),
                pl.BlockSpec((d, nout), lambda i: (0, 0))]
    args = [xs, nw.reshape(1, d), mods, mods, w_bf]
    if col_ops is not None:
        rope_spec = pl.BlockSpec((ROW_TILE, LANES), lambda i: (i % tpb, 0))
        in_specs += [rope_spec, rope_spec]
        args += [rope[0], rope[1]]
    return pl.pallas_call(
        functools.partial(_proj_kernel, col_ops=col_ops, chunk=512),
        out_shape=jax.ShapeDtypeStruct((nout // LANES, nt, LANES), F32),
        grid=(nt // ROW_TILE,),
        in_specs=in_specs,
        out_specs=pl.BlockSpec((nout // LANES, ROW_TILE, LANES), lambda i: (0, i, 0)),
        compiler_params=_cparams(1),
    )(*args)


def _rope_tables(n_ctx, seq):
    n_rows = seq // GRID_W
    row = jnp.repeat(jnp.arange(n_rows, dtype=F32), GRID_W)
    col = jnp.tile(jnp.arange(GRID_W, dtype=F32), n_rows)
    axis_dim = HEAD_DIM // 2
    inv_freq = ROPE_THETA ** (-jnp.arange(0, axis_dim, 2, dtype=F32) / axis_dim)
    ang_r = row[:, None] * inv_freq[None, :]
    ang_c = col[:, None] * inv_freq[None, :]
    cos = jnp.concatenate([jnp.cos(ang_r), jnp.cos(ang_r), jnp.cos(ang_c), jnp.cos(ang_c)], axis=-1)
    sin = jnp.concatenate([-jnp.sin(ang_r), jnp.sin(ang_r), -jnp.sin(ang_c), jnp.sin(ang_c)], axis=-1)
    cos = jnp.concatenate([jnp.ones((n_ctx, HEAD_DIM), F32), cos], axis=0)
    sin = jnp.concatenate([jnp.zeros((n_ctx, HEAD_DIM), F32), sin], axis=0)
    return cos, sin


def _head_spec(rows, index_map):
    return pl.BlockSpec((None, rows, LANES), index_map)


HEAD_DMA_ROWS = 256


def _head_slabs(hbm_refs, head_offsets, bufs, sems, n_heads):
    step = pl.program_id(0) * n_heads + pl.program_id(1)
    n_steps = pl.num_programs(0) * n_heads
    rows = bufs[0].shape[1]
    piece = HEAD_DMA_ROWS
    assert rows % piece == 0

    def start(at_step, slot):
        b, h = at_step // n_heads, at_step % n_heads
        for j, (ref, off) in enumerate(zip(hbm_refs, head_offsets)):
            for p in range(rows // piece):
                pltpu.make_async_copy(ref.at[off + h, pl.ds(b * rows + p * piece, piece)],
                                      bufs[j].at[slot, pl.ds(p * piece, piece)], sems.at[j, slot]).start()

    slot = step % 2

    @pl.when(step == 0)
    def _():
        start(step, slot)

    @pl.when(step + 1 < n_steps)
    def _():
        start(step + 1, 1 - slot)

    for j, ref in enumerate(hbm_refs):
        pltpu.make_async_copy(ref.at[0, pl.ds(0, rows)], bufs[j].at[slot], sems.at[j, slot]).wait()
    return [buf.at[slot] for buf in bufs]


def _backward_chunk(t, ctx_chunks, n_chunks):
    return jnp.where(t < ctx_chunks, ctx_chunks - 1 - t, n_chunks - 1 - (t - ctx_chunks))


def _ret_kernel(raw_ref, proj_hbm, o_ref, qbuf, kbuf, vbuf, gbuf, sems, of_scr, ob_scr, s_scr, dec_scr, qd_scr,
                kd_scr, *, ctx_chunks, n_chunks):
    h = pl.program_id(1)
    c = CHUNK
    q_ref, k_ref, v_ref, g_ref = _head_slabs(
        [proj_hbm] * 4, [0, RET_HEADS, 2 * RET_HEADS, 3 * RET_HEADS], [qbuf, kbuf, vbuf, gbuf], sems, RET_HEADS)
    ii = lax.broadcasted_iota(jnp.int32, (c, c), 0).astype(F32)
    jj = lax.broadcasted_iota(jnp.int32, (c, c), 1).astype(F32)
    lg = []
    for d in range(2):
        lgd = -jnp.exp(jnp.full((c, c), raw_ref[d, h], F32))
        lg.append(lgd)
        rel = (ii - jj) if d == 0 else (jj - ii)
        dec_scr[d] = jnp.where(rel >= 0, jnp.exp(jnp.maximum(rel, 0.0) * lgd), 0.0)
        qd_scr[d] = jnp.exp(((ii + 1.0) if d == 0 else (c - ii)) * lgd)
        kd_scr[d] = jnp.exp(((c - 1.0 - ii) if d == 0 else ii) * lgd)

    def chunk(n, d):
        rows = pl.ds(pl.multiple_of(n * c, c), c)
        q = q_ref[rows, :]
        k = k_ref[rows, :]
        vb = v_ref[rows, :].astype(BF16)
        sc = _dot_nt(q.astype(BF16), k.astype(BF16)) * dec_scr[d]
        s = s_scr[d]
        o = _dot(sc.astype(BF16), vb) + _dot((q * qd_scr[d]).astype(BF16), s.astype(BF16))
        chunk_decay = jnp.exp(float(c) * lg[d][0:1, :])
        s_scr[d] = chunk_decay * s + _dot_tn((k * kd_scr[d]).astype(BF16), vb)
        return rows, o

    s_scr[...] = jnp.zeros_like(s_scr)

    def scan(t, carry):
        rows, o = chunk(t, 0)
        of_scr[rows, :] = o
        rows, o = chunk(_backward_chunk(t, ctx_chunks, n_chunks), 1)
        ob_scr[rows, :] = o
        return carry

    lax.fori_loop(0, n_chunks, scan, 0, unroll=2)

    def readout(n, carry):
        rows = pl.ds(pl.multiple_of(n * c, c), c)
        o = of_scr[rows, :] + ob_scr[rows, :]
        y = o * lax.rsqrt(jnp.mean(o * o, axis=-1, keepdims=True) + EPS)
        o_ref[rows, :] = (_silu(g_ref[rows, :]) * y).astype(BF16)
        return carry

    lax.fori_loop(0, n_chunks, readout, 0)


def _retention(proj, ret_decay_raw, dims):
    batch, n_ctx, seq = dims
    tb = n_ctx + seq
    slab = pltpu.VMEM((2, tb, LANES), F32)
    return pl.pallas_call(
        functools.partial(_ret_kernel, ctx_chunks=n_ctx // CHUNK, n_chunks=tb // CHUNK),
        out_shape=jax.ShapeDtypeStruct((RET_HEADS, batch * tb, LANES), BF16),
        grid=(batch, RET_HEADS),
        in_specs=[pl.BlockSpec(memory_space=pltpu.SMEM), pl.BlockSpec(memory_space=pl.ANY)],
        out_specs=_head_spec(tb, lambda b, h: (h, b, 0)),
        scratch_shapes=[slab, slab, slab, slab, pltpu.SemaphoreType.DMA((4, 2)),
                        pltpu.VMEM((tb, LANES), F32), pltpu.VMEM((tb, LANES), F32),
                        pltpu.VMEM((2, LANES, LANES), F32),
                        pltpu.VMEM((2, CHUNK, CHUNK), F32), pltpu.VMEM((2, CHUNK, LANES), F32),
                        pltpu.VMEM((2, CHUNK, LANES), F32)],
        compiler_params=_cparams(2),
    )(ret_decay_raw, proj)


def _att_kernel(sink_ref, q0_ref, q1_ref, k_ref, v_ref, o_ref, kb_scr, vb_scr, bias_scr, *, n_ctx, seq):
    hk = pl.program_id(1)
    c = CHUNK
    win = 3 * c
    kb_scr[...] = k_ref[...].astype(BF16)
    vb_scr[...] = v_ref[...].astype(BF16)
    kc = kb_scr[0:n_ctx, :]
    vc = vb_scr[0:n_ctx, :]
    ii = lax.broadcasted_iota(jnp.int32, (c, win), 0)
    jj = lax.broadcasted_iota(jnp.int32, (c, win), 1)
    for off in range(3):
        bias_scr[off] = jnp.where(jnp.abs(ii - jj + off * c) <= WINDOW, 0.0, NEG_INF)

    def softmax_out(s_ctx, sink, s_loc=None, v_loc=None):
        m = jnp.maximum(jnp.max(s_ctx, axis=-1, keepdims=True), sink)
        if s_loc is not None:
            m = jnp.maximum(m, jnp.max(s_loc, axis=-1, keepdims=True))
        p_ctx = jnp.exp(s_ctx - m)
        den = jnp.sum(p_ctx, axis=-1, keepdims=True) + jnp.exp(sink - m)
        o = _dot(p_ctx.astype(BF16), vc)
        if s_loc is not None:
            p_loc = jnp.exp(s_loc - m)
            den = den + jnp.sum(p_loc, axis=-1, keepdims=True)
            o = o + _dot(p_loc.astype(BF16), v_loc)
        return o / den

    heads = []
    for g, q_ref in enumerate((q0_ref, q1_ref)):
        sink = jnp.full((c, 1), sink_ref[hk * ATT_GROUP + g], F32)
        heads.append((q_ref, sink, g))

    for q_ref, sink, g in heads:
        for cc in range(n_ctx // c):
            qb = q_ref[cc * c:(cc + 1) * c, :].astype(BF16)
            o_ref[g, cc * c:(cc + 1) * c, :] = softmax_out(_dot_nt(qb, kc), sink).astype(BF16)

    def block(n, carry):
        q_rows = pl.ds(pl.multiple_of(n_ctx + n * c, c), c)
        start = jnp.clip(n * c - c, 0, seq - win)
        k_rows = pl.ds(pl.multiple_of(n_ctx + start, c), win)
        k_loc = kb_scr[k_rows, :]
        v_loc = vb_scr[k_rows, :]
        bias = bias_scr[(n * c - start) // c]
        for q_ref, sink, g in heads:
            qb = q_ref[q_rows, :].astype(BF16)
            o = softmax_out(_dot_nt(qb, kc), sink, _dot_nt(qb, k_loc) + bias, v_loc)
            o_ref[g, q_rows, :] = o.astype(BF16)
        return carry

    lax.fori_loop(0, seq // c, block, 0)


def _attention(proj, att_sink, dims, col0):
    batch, n_ctx, seq = dims
    tb = n_ctx + seq
    qb, kb, vb = col0, col0 + ATT_Q_HEADS, col0 + ATT_Q_HEADS + ATT_KV_HEADS
    return pl.pallas_call(
        functools.partial(_att_kernel, n_ctx=n_ctx, seq=seq),
        out_shape=jax.ShapeDtypeStruct((ATT_Q_HEADS, batch * tb, LANES), BF16),
        grid=(batch, ATT_KV_HEADS),
        in_specs=[pl.BlockSpec(memory_space=pltpu.SMEM),
                  _head_spec(tb, lambda b, h: (qb + ATT_GROUP * h, b, 0)),
                  _head_spec(tb, lambda b, h: (qb + ATT_GROUP * h + 1, b, 0)),
                  _head_spec(tb, lambda b, h: (kb + h, b, 0)),
                  _head_spec(tb, lambda b, h: (vb + h, b, 0))],
        out_specs=pl.BlockSpec((ATT_GROUP, tb, LANES), lambda b, h: (h, b, 0)),
        scratch_shapes=[pltpu.VMEM((tb, LANES), BF16), pltpu.VMEM((tb, LANES), BF16),
                        pltpu.VMEM((3, CHUNK, 3 * CHUNK), F32)],
        compiler_params=_cparams(2),
    )(att_sink, proj, proj, proj, proj)


HG_LEVELS = 7
HG_MXU_LEVELS = 3


def _hgrn_constants():
    c = CHUNK
    i = np.arange(c)[:, None]
    r = np.arange(c)[None, :]
    mats = [r <= i]
    for m in range(1, HG_MXU_LEVELS + 1):
        half = 1 << (m - 1)
        beta = (i // (2 * half)) * (2 * half) + half - 1
        upper = (i % (2 * half)) >= half
        mats.append((upper & (r > beta) & (r <= i)) | ((~upper) & (r > i) & (r <= beta)))
    fwd = [m.astype(np.float32) for m in mats]
    bwd = [m[::-1, ::-1] for m in fwd]
    stack = np.stack([np.concatenate(fwd, 0), np.concatenate(bwd, 0)])
    x = i ^ r
    level = np.where(r > i, -1, np.where(r == i, 0, np.floor(np.log2(np.maximum(x, 1))) + 1)).astype(np.int32)
    levels = np.stack([level, level[::-1, ::-1]])
    return stack, levels


def _hgrn_kernel(proj_hbm, lbl_ref, nw_ref, m_ref, lv_ref, o_ref, qbuf, zfbuf, zbbuf, vbuf, gbuf, sems,
                 of_scr, ob_scr, st_scr, *, layer, ctx_chunks, n_chunks):
    c = CHUNK
    q_ref, zf_ref, zb_ref, v_ref, g_ref = _head_slabs(
        [proj_hbm] * 5, [j * HG_HEADS for j in range(5)], [qbuf, zfbuf, zbbuf, vbuf, gbuf], sems, HG_HEADS)
    z = lbl_ref[...]
    e = jnp.exp(z - jnp.max(z, axis=0, keepdims=True))
    sm = e / jnp.sum(e, axis=0, keepdims=True)
    lb = jnp.sum(sm[1:layer + 1], axis=0, keepdims=True)
    row = lax.broadcasted_iota(jnp.int32, (c, LANES), 0)
    upper = [[((row if d == 0 else c - 1 - row) & (1 << (m - 1))) != 0 for m in range(1, HG_MXU_LEVELS + 1)]
             for d in range(2)]

    def level_operand(q, k, a, half, d):
        pieces = []
        for base in range(0, c, 2 * half):
            lo, hi = slice(base, base + half), slice(base + half, base + 2 * half)
            if d == 0:
                ref = a[base + half - 1:base + half]
                pieces += [k[lo] * jnp.exp(ref - a[lo]), q[hi] * jnp.exp(a[hi] - ref)]
            else:
                ref = a[base + half:base + half + 1]
                pieces += [q[lo] * jnp.exp(a[lo] - ref), k[hi] * jnp.exp(ref - a[hi])]
        return jnp.concatenate(pieces, axis=0).astype(BF16)

    def chunk(n, d, z_ref):
        rows = pl.ds(pl.multiple_of(n * c, c), c)
        q = q_ref[rows, :]
        v = v_ref[rows, :]
        f = lb + (1.0 - lb) * jax.nn.sigmoid(z_ref[rows, :])
        g = jnp.log(f)
        k = 1.0 - f
        g_hi = g.astype(BF16)
        g_lo = (g - g_hi.astype(F32)).astype(BF16)
        res = _dot(m_ref[d], jnp.concatenate([g_hi, g_lo], axis=1))
        ex = res[:, :c] + res[:, c:]
        a = ex[0:c]
        st = st_scr[d]
        o = _dot_nt((q * jnp.exp(a)).astype(BF16), st.astype(BF16))
        lv = lv_ref[d]
        sc = jnp.where(lv == 0, _dot_nt(q.astype(BF16), k.astype(BF16)), 0.0)
        for m in range(1, HG_LEVELS + 1):
            if m <= HG_MXU_LEVELS:
                zz = (jnp.where(upper[d][m - 1], q, k) * jnp.exp(ex[m * c:(m + 1) * c])).astype(BF16)
            else:
                zz = level_operand(q, k, a, 1 << (m - 1), d)
            sc = jnp.where(lv == m, _dot_nt(zz, zz), sc)
        vb = v.astype(BF16)
        o = o + _dot(sc.astype(BF16), vb)
        a_last = a[c - 1:c] if d == 0 else a[0:1]
        kd = (k * jnp.exp(a_last - a)).astype(BF16)
        st_scr[d] = st * jnp.exp(a_last) + _dot_tn(vb, kd)
        return rows, o

    st_scr[...] = jnp.zeros_like(st_scr)

    def scan(t, carry):
        rows, o = chunk(t, 0, zf_ref)
        of_scr[rows, :] = o
        rows, o = chunk(_backward_chunk(t, ctx_chunks, n_chunks), 1, zb_ref)
        ob_scr[rows, :] = o
        return carry

    lax.fori_loop(0, n_chunks, scan, 0, unroll=2)

    def readout(n, carry):
        rows = pl.ds(pl.multiple_of(n * c, c), c)
        o = of_scr[rows, :] + ob_scr[rows, :]
        y = o * lax.rsqrt(jnp.mean(o * o, axis=-1, keepdims=True) + EPS) * nw_ref[...]
        o_ref[rows, :] = (y * _silu(g_ref[rows, :])).astype(BF16)
        return carry

    lax.fori_loop(0, n_chunks, readout, 0)


def _hgrn2(proj, hg_lb_logits, hg_norm_w, layer, dims):
    batch, n_ctx, seq = dims
    tb = n_ctx + seq
    depth = hg_lb_logits.shape[0]
    stack, levels = _hgrn_constants()
    slab = pltpu.VMEM((2, tb, LANES), F32)
    n_mats = 1 + HG_MXU_LEVELS
    return pl.pallas_call(
        functools.partial(_hgrn_kernel, layer=layer, ctx_chunks=n_ctx // CHUNK, n_chunks=tb // CHUNK),
        out_shape=jax.ShapeDtypeStruct((HG_HEADS, batch * tb, LANES), BF16),
        grid=(batch, HG_HEADS),
        in_specs=[pl.BlockSpec(memory_space=pl.ANY),
                  pl.BlockSpec((depth, LANES), lambda b, h: (0, h)),
                  pl.BlockSpec((1, LANES), lambda b, h: (0, 0)),
                  pl.BlockSpec((2, n_mats * CHUNK, CHUNK), lambda b, h: (0, 0, 0)),
                  pl.BlockSpec((2, CHUNK, CHUNK), lambda b, h: (0, 0, 0))],
        out_specs=_head_spec(tb, lambda b, h: (h, b, 0)),
        scratch_shapes=[slab, slab, slab, slab, slab, pltpu.SemaphoreType.DMA((5, 2)),
                        pltpu.VMEM((tb, LANES), F32), pltpu.VMEM((tb, LANES), F32),
                        pltpu.VMEM((2, LANES, LANES), F32)],
        compiler_params=_cparams(2),
    )(proj, hg_lb_logits, hg_norm_w.reshape(1, LANES),
      jnp.asarray(stack, BF16), jnp.asarray(levels, jnp.int32))


SUBLANES = 8


def _store_token_tiles(ref, x):
    rows, tiles = x.shape[0], x.shape[1] // LANES
    for s in range(tiles):
        ref[pl.ds(s, rows, stride=tiles), :] = x[:, s * LANES:(s + 1) * LANES]


def _load_token_tiles(ref, tiles):
    rows = ref.shape[0] // tiles
    return jnp.concatenate([ref[pl.ds(s, rows, stride=tiles), :] for s in range(tiles)], axis=1)


def _route(logits, tri, counts):
    lane = lax.broadcasted_iota(jnp.int32, logits.shape, 1).astype(F32)
    big = float(LANES)
    first = lambda hit: jnp.min(jnp.where(hit, lane, big), axis=-1, keepdims=True)
    is_g = lane < MOE_GROUPS
    gl = jnp.where(is_g, logits, NEG_INF)
    gmax = jnp.max(gl, axis=-1, keepdims=True)
    gsum = jnp.sum(jnp.where(is_g, jnp.exp(gl - gmax), 0.0), axis=-1, keepdims=True)
    g_p = 1.0 / gsum
    lo = MOE_GROUPS + first(gl == gmax) * MOE_EPG
    el = jnp.where((lane >= lo) & (lane < lo + MOE_EPG), logits, NEG_INF)
    e1v = jnp.max(el, axis=-1, keepdims=True)
    e1 = first(el == e1v)
    el2 = jnp.where(lane == e1, NEG_INF, el)
    e2v = jnp.max(el2, axis=-1, keepdims=True)
    e2 = first(el2 == e2v)
    t = jnp.exp(e2v - e1v)
    w1 = 1.0 / (1.0 + t)
    w2 = t * w1
    id1 = e1 - MOE_GROUPS
    id2 = e2 - MOE_GROUPS
    onehot = jnp.where((lane == id1) | (lane == id2), 1.0, 0.0)
    before = counts + _dot(tri, onehot.astype(BF16))
    rank1 = jnp.sum(jnp.where(lane == id1, before, 0.0), axis=-1, keepdims=True)
    rank2 = jnp.sum(jnp.where(lane == id2, before, 0.0), axis=-1, keepdims=True)
    slab = jnp.zeros_like(logits)
    for col, val in enumerate((id1, id2, g_p * w1, g_p * w2, rank1, rank2)):
        slab = jnp.where(lane == col, val, slab)
    return slab, counts + jnp.sum(onehot, axis=0, keepdims=True)


def _out_route_kernel(*refs, n_parts):
    a_refs = refs[:n_parts]
    (w_ref, x_ref, gate_ref, nw_ref, shift_ref, scale_ref, wr_ref, br_ref, tri_ref,
     xo_ref, h2_ref, route_ref, cnt_ref) = refs[n_parts:]

    @pl.when(pl.program_id(0) == 0)
    def _():
        cnt_ref[...] = jnp.zeros_like(cnt_ref)

    k0 = 0
    y = None
    for a_ref in a_refs:
        a = jnp.concatenate([a_ref[h] for h in range(a_ref.shape[0])], axis=1)
        kk = a.shape[1]
        t = _dot(a, w_ref[k0:k0 + kk, :])
        y = t if y is None else y + t
        k0 += kk
    x = x_ref[...] + gate_ref[0] * y
    xo_ref[...] = x
    h2 = _norm_mod(x, nw_ref[...], scale_ref[0], shift_ref[0])
    _store_token_tiles(h2_ref, h2)
    h_hi = h2.astype(BF16)
    h_lo = (h2 - h_hi.astype(F32)).astype(BF16)
    both = _dot(h_hi, wr_ref[...])
    logits = both[:, :LANES] + both[:, LANES:] + _dot(h_lo, wr_ref[:, :LANES]) + br_ref[...]
    slab, counts = _route(logits, tri_ref[...], cnt_ref[0:1, :])
    route_ref[...] = slab
    cnt_ref[...] = jnp.broadcast_to(counts, cnt_ref.shape)


def _out_projection_route(parts, w_bf, xs, nw, mods, wr, br, dims):
    nt, d = xs.shape
    batch, n_ctx, seq = dims
    tpb = (n_ctx + seq) // ROW_TILE
    ctx_tiles = n_ctx // ROW_TILE
    row = lambda width: pl.BlockSpec((ROW_TILE, width), lambda i: (i, 0))
    full = lambda shape: pl.BlockSpec(shape, lambda i: (0,) * len(shape))
    mod = lambda chunk_idx: _mod_spec(chunk_idx, d, tpb, ctx_tiles, batch)
    tri = jnp.asarray(np.tril(np.ones((ROW_TILE, ROW_TILE), np.float32), -1), BF16)
    return pl.pallas_call(
        functools.partial(_out_route_kernel, n_parts=len(parts)),
        out_shape=(jax.ShapeDtypeStruct((nt, d), F32), jax.ShapeDtypeStruct((nt * d // LANES, LANES), F32),
                   jax.ShapeDtypeStruct((nt, LANES), F32), jax.ShapeDtypeStruct((8, LANES), F32)),
        grid=(nt // ROW_TILE,),
        in_specs=[pl.BlockSpec((p.shape[0], ROW_TILE, LANES), lambda i: (0, i, 0)) for p in parts] + [
            full(w_bf.shape), row(d), mod(2), full((1, d)), mod(3), mod(4), full(wr.shape), full(br.shape),
            full(tri.shape)],
        out_specs=(row(d), pl.BlockSpec((ROW_TILE * d // LANES, LANES), lambda i: (i, 0)), row(LANES),
                   full((8, LANES))),
        compiler_params=_cparams(1),
    )(*parts, w_bf, xs, mods, nw.reshape(1, d), mods, mods, wr, br, tri)


SCATTER_DMA_PRIORITY = 1
MOE_BUFFERS = 4


def _dispatch_plan(route, counts_f, n_tok, tiles):
    n_assign = n_tok * MOE_TOP_K
    flat_e = route[:, 0:MOE_TOP_K].astype(jnp.int32).reshape(-1)
    rank = route[:, 4:4 + MOE_TOP_K].astype(jnp.int32).reshape(-1)
    counts = counts_f[0, :MOE_EXPERTS].astype(jnp.int32)
    padded = (counts + MOE_BLOCK - 1) // MOE_BLOCK * MOE_BLOCK
    pad_end = jnp.cumsum(padded)
    pad_start = pad_end - padded
    dest = pad_start[flat_e] + rank
    n_blocks = -(-n_assign // MOE_BLOCK) + MOE_EXPERTS
    n_slots = n_blocks * MOE_BLOCK
    block_start = jnp.arange(n_blocks, dtype=jnp.int32) * MOE_BLOCK
    block_expert = jnp.minimum(jnp.sum((pad_end[None, :] <= block_start[:, None]).astype(jnp.int32), axis=1),
                               MOE_EXPERTS - 1)
    slot_assign = jnp.full((n_slots,), -1, jnp.int32).at[dest].set(jnp.arange(n_assign, dtype=jnp.int32))
    lead, trail = MOE_BLOCK, (MOE_BUFFERS - 1) * MOE_BLOCK
    slot = jnp.arange(-lead, n_slots + trail, dtype=jnp.int32)
    spare = n_assign + ((slot // MOE_BLOCK) % MOE_BUFFERS) * MOE_BLOCK + slot % MOE_BLOCK
    assign = jnp.concatenate([jnp.full((lead,), -1, jnp.int32), slot_assign, jnp.full((trail,), -1, jnp.int32)])
    src = jnp.where(assign >= 0, assign // MOE_TOP_K, 0) * tiles
    dst = jnp.where(assign >= 0, (assign % MOE_TOP_K) * n_tok + assign // MOE_TOP_K, spare) * tiles
    n_used = (pad_end[-1:] // MOE_BLOCK).astype(jnp.int32)
    return block_expert, n_used, src[lead:], dst[:n_slots + lead], n_blocks


def _expert_kernel(be_ref, nused_ref, src_ref, dst_ref, h_hbm, w1_ref, w3_ref, w2_ref, out_hbm,
                   xbuf, ybuf, w1b, w3b, w2b, gsem, ssem):
    i = pl.program_id(0)
    n_used = nused_ref[0]
    nb = MOE_BUFFERS
    cur = i % nb
    buffer_of = lambda blk: (blk + nb) % nb

    tiles = xbuf.shape[1] // MOE_BLOCK

    def token(idx):
        return pl.ds(pl.multiple_of(idx, tiles), tiles)

    def start_gather(blk, buf):
        for r in range(MOE_BLOCK):
            pltpu.make_async_copy(h_hbm.at[token(src_ref[blk * MOE_BLOCK + r])],
                                  xbuf.at[buf, pl.ds(r * tiles, tiles)], gsem.at[buf]).start()

    def start_scatter(blk, buf):
        for r in range(MOE_BLOCK):
            pltpu.make_async_copy(ybuf.at[buf, pl.ds(r * tiles, tiles)],
                                  out_hbm.at[token(dst_ref[(blk + 1) * MOE_BLOCK + r])], ssem.at[buf]).start(
                                      priority=SCATTER_DMA_PRIORITY)

    def wait_gather(buf):
        pltpu.make_async_copy(h_hbm.at[pl.ds(0, MOE_BLOCK * tiles)], xbuf.at[buf], gsem.at[buf]).wait()

    def wait_scatter(buf):
        pltpu.make_async_copy(ybuf.at[buf], out_hbm.at[pl.ds(0, MOE_BLOCK * tiles)], ssem.at[buf]).wait()

    @pl.when(i < n_used)
    def _():
        @pl.when(i == 0)
        def _():
            ybuf[...] = jnp.zeros_like(ybuf)
            block_rows = MOE_BLOCK * tiles
            spare0 = out_hbm.shape[0] - nb * block_rows
            for b in range(nb):
                spare = pltpu.make_async_copy(ybuf.at[b], out_hbm.at[pl.ds(spare0 + b * block_rows, block_rows)],
                                              ssem.at[b])
                spare.start()
                spare.wait()
            for blk in range(nb - 1):
                start_gather(blk, blk)

        @pl.when((i == 0) | (be_ref[i] != be_ref[jnp.maximum(i - 1, 0)]))
        def _():
            w1b[...] = w1_ref[0, 0].astype(BF16)
            w3b[...] = w3_ref[0, 0].astype(BF16)
            w2b[...] = w2_ref[0, 0].astype(BF16)

        wait_gather(cur)

        @pl.when(i >= nb - 1)
        def _():
            wait_scatter(cur)

        start_gather(i + nb - 1, buffer_of(i - 1))
        start_scatter(i - 1, buffer_of(i - 1))
        xb = _load_token_tiles(xbuf.at[cur], tiles).astype(BF16)
        act = _silu(_dot(xb, w1b[...])) * _dot(xb, w3b[...])
        _store_token_tiles(ybuf.at[cur], _dot(act.astype(BF16), w2b[...]))

        @pl.when(i == n_used - 1)
        def _():
            start_scatter(i, cur)
            for back in range(nb):
                @pl.when(i - back >= -1)
                def _():
                    wait_scatter(buffer_of(i - back))
            for ahead in range(1, nb):
                wait_gather(buffer_of(i + ahead))


def _experts(h2, plan, w1, w3, w2, layer):
    block_expert, n_used, slot_src, slot_dst, n_blocks = plan
    d, ff = w1.shape[2], w1.shape[3]
    tiles = d // LANES
    nt = h2.shape[0] // tiles
    wspec = lambda shape: pl.BlockSpec((1, 1) + shape, lambda i, be, nu, sr, ds: (layer, be[i], 0, 0))
    row_buffers = pltpu.VMEM((MOE_BUFFERS, MOE_BLOCK * tiles, LANES), F32)
    grid_spec = pltpu.PrefetchScalarGridSpec(
        num_scalar_prefetch=4,
        grid=(n_blocks,),
        in_specs=[pl.BlockSpec(memory_space=pl.ANY), wspec((d, ff)), wspec((d, ff)), wspec((ff, d))],
        out_specs=pl.BlockSpec(memory_space=pl.ANY),
        scratch_shapes=[row_buffers, row_buffers,
                        pltpu.VMEM((d, ff), BF16), pltpu.VMEM((d, ff), BF16), pltpu.VMEM((ff, d), BF16),
                        pltpu.SemaphoreType.DMA((MOE_BUFFERS,)), pltpu.SemaphoreType.DMA((MOE_BUFFERS,))])
    return pl.pallas_call(
        _expert_kernel,
        out_shape=jax.ShapeDtypeStruct(((nt * MOE_TOP_K + MOE_BUFFERS * MOE_BLOCK) * tiles, LANES), F32),
        grid_spec=grid_spec,
        compiler_params=_cparams(1, has_side_effects=True, disable_bounds_checks=True),
    )(block_expert, n_used, slot_src, slot_dst, h2, w1, w3, w2)


def _combine_kernel(x_ref, y0_ref, y1_ref, route_ref, gate_ref, *rest, final):
    r = route_ref[...]
    tiles = x_ref.shape[1] // LANES
    y = r[:, 2:3] * _load_token_tiles(y0_ref, tiles) + r[:, 3:4] * _load_token_tiles(y1_ref, tiles)
    x = x_ref[...] + gate_ref[0] * y
    if final:
        fnw_ref, o_ref = rest
        x = x * lax.rsqrt(jnp.mean(x * x, axis=-1, keepdims=True) + EPS) * fnw_ref[...]
    else:
        (o_ref,) = rest
    o_ref[...] = x


def _combine(xs, y2, route, mods, dims, final_norm_w=None):
    nt, d = xs.shape
    batch, n_ctx, seq = dims
    tpb = (n_ctx + seq) // ROW_TILE
    ctx_tiles = n_ctx // ROW_TILE
    row = lambda width: pl.BlockSpec((ROW_TILE, width), lambda i: (i, 0))
    final = final_norm_w is not None
    first = pl.BlockSpec((ROW_TILE * d // LANES, LANES), lambda i: (i, 0))
    second = pl.BlockSpec((ROW_TILE * d // LANES, LANES), lambda i: (i + nt // ROW_TILE, 0))
    in_specs = [row(d), first, second, row(LANES), _mod_spec(5, d, tpb, ctx_tiles, batch)]
    args = [xs, y2, y2, route, mods]
    if final:
        in_specs.append(pl.BlockSpec((1, d), lambda i: (0, 0)))
        args.append(final_norm_w.reshape(1, d))
    return pl.pallas_call(
        functools.partial(_combine_kernel, final=final),
        out_shape=jax.ShapeDtypeStruct((nt, d), F32),
        grid=(nt // ROW_TILE,),
        in_specs=in_specs,
        out_specs=row(d),
        compiler_params=_cparams(1),
    )(*args)


def _even_col_ops():
    rh = RET_HEADS
    ops = [(True, 1.0, 1.0)] * rh + [(True, float(HEAD_DIM) ** -0.5, 1.0)] * rh + [(False, 1.0, 1.0)] * (2 * rh)
    ops += [(True, 1.0, float(HEAD_DIM) ** -0.5)] * ATT_Q_HEADS + [(True, 1.0, 1.0)] * ATT_KV_HEADS
    ops += [(False, 1.0, 1.0)] * ATT_KV_HEADS
    return tuple(ops)


def kernel(x, c, ctx, c_ctx, ada_w, ada_b, norm_w, final_norm_w, ev_w_in, ev_w_out, ret_decay_raw, att_sink,
           od_w_in, od_w_out, hg_lb_logits, hg_norm_w, moe_wg, moe_bg, moe_we, moe_be, moe_w1, moe_w3, moe_w2):
    batch, seq, d = x.shape
    n_ctx = ctx.shape[1]
    depth = ada_w.shape[0]
    assert batch + 1 <= MOD_ROWS and n_ctx % ROW_TILE == 0 and seq % ROW_TILE == 0
    assert seq % GRID_W == 0 and seq >= 3 * CHUNK
    dims = (batch, n_ctx, seq)
    tb = n_ctx + seq
    nt = batch * tb

    xs = jnp.concatenate([ctx, x], axis=1).reshape(nt, d)
    cvec = jnp.concatenate([c, c_ctx[None, :], jnp.zeros((MOD_ROWS - batch - 1, d), F32)], axis=0)
    mods_all = _ada_modulation(cvec, ada_w, ada_b)
    rope = _rope_tables(n_ctx, seq)
    even_ops = _even_col_ops()

    for l in range(depth):
        p = l // 2
        mods = mods_all[l].reshape(MOD_ROWS, 1, 6 * d)
        if l % 2 == 0:
            proj = _in_projection(xs, norm_w[l, 0], mods, ev_w_in[p].astype(BF16), dims, even_ops, rope)
            ret = _retention(proj, ret_decay_raw[p], dims)
            att = _attention(proj, att_sink[p], dims, 4 * RET_HEADS)
            parts, w_out = [ret, att], ev_w_out[p]
        else:
            proj = _in_projection(xs, norm_w[l, 0], mods, od_w_in[p].astype(BF16), dims)
            parts, w_out = [_hgrn2(proj, hg_lb_logits, hg_norm_w[p], l, dims)], od_w_out[p]
        n_logit = MOE_GROUPS + MOE_EXPERTS
        wr = jnp.concatenate([moe_wg[l], jnp.moveaxis(moe_we[l], 0, 1).reshape(d, MOE_EXPERTS),
                              jnp.zeros((d, LANES - n_logit), F32)], axis=1)
        wr_hi = wr.astype(BF16)
        wr = jnp.concatenate([wr_hi, (wr - wr_hi.astype(F32)).astype(BF16)], axis=1)
        br = jnp.concatenate([moe_bg[l], moe_be[l].reshape(-1), jnp.zeros((LANES - n_logit,), F32)])[None, :]
        xs, h2, route, counts = _out_projection_route(
            parts, w_out.astype(BF16), xs, norm_w[l, 1], mods, wr, br, dims)
        plan = _dispatch_plan(route, counts, nt, d // LANES)
        y2 = _experts(h2, plan, moe_w1, moe_w3, moe_w2, l)
        xs = _combine(xs, y2, route, mods, dims, final_norm_w if l == depth - 1 else None)

    return xs.reshape(batch, tb, d)[:, n_ctx:]
```

```python
import functools

import numpy as np
import jax
import jax.numpy as jnp
from jax import lax
from jax.experimental import pallas as pl
from jax.experimental.pallas import tpu as pltpu

F32 = jnp.float32
BF16 = jnp.bfloat16
HIGHEST = lax.Precision.HIGHEST

EPS = 1e-6
NEG_INF = -1e30
LANES = 128
GRID_W = 64
ROPE_THETA = 10000.0
HEAD_DIM = 128
RET_HEADS = 4
ATT_Q_HEADS = 4
ATT_KV_HEADS = 2
ATT_GROUP = ATT_Q_HEADS // ATT_KV_HEADS
WINDOW = 128
HG_HEADS = 8
MOE_GROUPS = 4
MOE_EPG = 8
MOE_EXPERTS = MOE_GROUPS * MOE_EPG
MOE_TOP_K = 2
MOE_BLOCK = 128
CHUNK = 128
ROW_TILE = 256
MOD_ROWS = 8
VMEM_LIMIT = 56 * 1024 * 1024


def _cparams(n_axes, **kw):
    return pltpu.CompilerParams(dimension_semantics=("arbitrary",) * n_axes,
                                vmem_limit_bytes=VMEM_LIMIT, **kw)


def _dot(a, b):
    return jnp.dot(a, b, preferred_element_type=F32)


def _dot_nt(a, b):
    return lax.dot_general(a, b, (((1,), (1,)), ((), ())), preferred_element_type=F32)


def _dot_tn(a, b):
    return lax.dot_general(a, b, (((0,), (0,)), ((), ())), preferred_element_type=F32)


def _silu(x):
    return x * jax.nn.sigmoid(x)


def _norm_mod(x, nw, scale, shift):
    ms = jnp.mean(x * x, axis=-1, keepdims=True)
    y = x * lax.rsqrt(ms + EPS) * nw
    return y * (1.0 + scale) + shift


def _ada_kernel(c_ref, w_ref, b_ref, o_ref):
    s = _silu(c_ref[...])
    o_ref[0] = jnp.dot(s, w_ref[0], precision=HIGHEST, preferred_element_type=F32) + b_ref[0]


def _ada_modulation(cvec, ada_w, ada_b):
    depth, d, n6 = ada_w.shape
    tn = 1536
    return pl.pallas_call(
        _ada_kernel,
        out_shape=jax.ShapeDtypeStruct((depth, MOD_ROWS, n6), F32),
        grid=(depth, n6 // tn),
        in_specs=[pl.BlockSpec((MOD_ROWS, d), lambda l, j: (0, 0)),
                  pl.BlockSpec((1, d, tn), lambda l, j: (l, 0, j)),
                  pl.BlockSpec((1, 1, tn), lambda l, j: (l, 0, j))],
        out_specs=pl.BlockSpec((1, MOD_ROWS, tn), lambda l, j: (l, 0, j)),
        compiler_params=_cparams(2),
    )(cvec, ada_w, ada_b.reshape(depth, 1, n6))


def _swap_halves(x):
    lane = lax.broadcasted_iota(jnp.int32, x.shape, 1)
    return jnp.where((lane % 64) < 32, pltpu.roll(x, 96, 1), pltpu.roll(x, 32, 1))


def _proj_kernel(x_ref, nw_ref, shift_ref, scale_ref, w_ref, *rest, col_ops, chunk):
    if col_ops is None:
        (o_ref,) = rest
    else:
        cos_ref, sin_ref, o_ref = rest
    h = _norm_mod(x_ref[...], nw_ref[...], scale_ref[0], shift_ref[0]).astype(BF16)
    nout = o_ref.shape[0] * LANES
    for c0 in range(0, nout, chunk):
        acc = _dot(h, w_ref[:, c0:c0 + chunk])
        for hd in range(chunk // LANES):
            col = c0 + hd * LANES
            a = acc[:, hd * LANES:(hd + 1) * LANES]
            if col_ops is not None:
                rope, pre, post = col_ops[col // LANES]
                if pre != 1.0:
                    a = a * pre
                if rope:
                    a = a * cos_ref[...] + _swap_halves(a) * sin_ref[...]
                if post != 1.0:
                    a = a * post
            o_ref[col // LANES] = a


def _tile_mod_row(i, tiles_per_batch, ctx_tiles, batch):
    return jnp.where(i % tiles_per_batch < ctx_tiles, batch, i // tiles_per_batch)


def _mod_spec(chunk_idx, d, tiles_per_batch, ctx_tiles, batch):
    return pl.BlockSpec(
        (1, 1, d), lambda i: (_tile_mod_row(i, tiles_per_batch, ctx_tiles, batch), 0, chunk_idx))


def _in_projection(xs, nw, mods, w_bf, dims, col_ops=None, rope=None):
    nt, d = xs.shape
    nout = w_bf.shape[1]
    batch, n_ctx, seq = dims
    tpb = (n_ctx + seq) // ROW_TILE
    ctx_tiles = n_ctx // ROW_TILE
    in_specs = [pl.BlockSpec((ROW_TILE, d), lambda i: (i, 0)),
                pl.BlockSpec((1, d), lambda i: (0, 0)),
                _mod_spec(0, d, tpb, ctx_tiles, batch),
                _mod_spec(1, d, tpb, ctx_tiles, batch),
                pl.BlockSpec((d, nout), lambda i: (0, 0))]
    args = [xs, nw.reshape(1, d), mods, mods, w_bf]
    if col_ops is not None:
        rope_spec = pl.BlockSpec((ROW_TILE, LANES), lambda i: (i % tpb, 0))
        in_specs += [rope_spec, rope_spec]
        args += [rope[0], rope[1]]
    return pl.pallas_call(
        functools.partial(_proj_kernel, col_ops=col_ops, chunk=512),
        out_shape=jax.ShapeDtypeStruct((nout // LANES, nt, LANES), F32),
        grid=(nt // ROW_TILE,),
        in_specs=in_specs,
        out_specs=pl.BlockSpec((nout // LANES, ROW_TILE, LANES), lambda i: (0, i, 0)),
        compiler_params=_cparams(1),
    )(*args)


def _rope_tables(n_ctx, seq):
    n_rows = seq // GRID_W
    row = jnp.repeat(jnp.arange(n_rows, dtype=F32), GRID_W)
    col = jnp.tile(jnp.arange(GRID_W, dtype=F32), n_rows)
    axis_dim = HEAD_DIM // 2
    inv_freq = ROPE_THETA ** (-jnp.arange(0, axis_dim, 2, dtype=F32) / axis_dim)
    ang_r = row[:, None] * inv_freq[None, :]
    ang_c = col[:, None] * inv_freq[None, :]
    cos = jnp.concatenate([jnp.cos(ang_r), jnp.cos(ang_r), jnp.cos(ang_c), jnp.cos(ang_c)], axis=-1)
    sin = jnp.concatenate([-jnp.sin(ang_r), jnp.sin(ang_r), -jnp.sin(ang_c), jnp.sin(ang_c)], axis=-1)
    cos = jnp.concatenate([jnp.ones((n_ctx, HEAD_DIM), F32), cos], axis=0)
    sin = jnp.concatenate([jnp.zeros((n_ctx, HEAD_DIM), F32), sin], axis=0)
    return cos, sin


def _head_spec(rows, index_map):
    return pl.BlockSpec((None, rows, LANES), index_map)


HEAD_DMA_ROWS = 256


def _head_slabs(hbm_refs, head_offsets, bufs, sems, n_heads):
    step = pl.program_id(0) * n_heads + pl.program_id(1)
    n_steps = pl.num_programs(0) * n_heads
    rows = bufs[0].shape[1]
    piece = HEAD_DMA_ROWS
    assert rows % piece == 0

    def start(at_step, slot):
        b, h = at_step // n_heads, at_step % n_heads
        for j, (ref, off) in enumerate(zip(hbm_refs, head_offsets)):
            for p in range(rows // piece):
                pltpu.make_async_copy(ref.at[off + h, pl.ds(b * rows + p * piece, piece)],
                                      bufs[j].at[slot, pl.ds(p * piece, piece)], sems.at[j, slot]).start()

    slot = step % 2

    @pl.when(step == 0)
    def _():
        start(step, slot)

    @pl.when(step + 1 < n_steps)
    def _():
        start(step + 1, 1 - slot)

    for j, ref in enumerate(hbm_refs):
        pltpu.make_async_copy(ref.at[0, pl.ds(0, rows)], bufs[j].at[slot], sems.at[j, slot]).wait()
    return [buf.at[slot] for buf in bufs]


def _backward_chunk(t, ctx_chunks, n_chunks):
    return jnp.where(t < ctx_chunks, ctx_chunks - 1 - t, n_chunks - 1 - (t - ctx_chunks))


def _ret_kernel(raw_ref, proj_hbm, o_ref, qbuf, kbuf, vbuf, gbuf, sems, of_scr, ob_scr, s_scr, dec_scr, qd_scr,
                kd_scr, *, ctx_chunks, n_chunks):
    h = pl.program_id(1)
    c = CHUNK
    q_ref, k_ref, v_ref, g_ref = _head_slabs(
        [proj_hbm] * 4, [0, RET_HEADS, 2 * RET_HEADS, 3 * RET_HEADS], [qbuf, kbuf, vbuf, gbuf], sems, RET_HEADS)
    ii = lax.broadcasted_iota(jnp.int32, (c, c), 0).astype(F32)
    jj = lax.broadcasted_iota(jnp.int32, (c, c), 1).astype(F32)
    lg = []
    for d in range(2):
        lgd = -jnp.exp(jnp.full((c, c), raw_ref[d, h], F32))
        lg.append(lgd)
        rel = (ii - jj) if d == 0 else (jj - ii)
        dec_scr[d] = jnp.where(rel >= 0, jnp.exp(jnp.maximum(rel, 0.0) * lgd), 0.0)
        qd_scr[d] = jnp.exp(((ii + 1.0) if d == 0 else (c - ii)) * lgd)
        kd_scr[d] = jnp.exp(((c - 1.0 - ii) if d == 0 else ii) * lgd)

    def chunk(n, d):
        rows = pl.ds(pl.multiple_of(n * c, c), c)
        q = q_ref[rows, :]
        k = k_ref[rows, :]
        vb = v_ref[rows, :].astype(BF16)
        sc = _dot_nt(q.astype(BF16), k.astype(BF16)) * dec_scr[d]
        s = s_scr[d]
        o = _dot(sc.astype(BF16), vb) + _dot((q * qd_scr[d]).astype(BF16), s.astype(BF16))
        chunk_decay = jnp.exp(float(c) * lg[d][0:1, :])
        s_scr[d] = chunk_decay * s + _dot_tn((k * kd_scr[d]).astype(BF16), vb)
        return rows, o

    s_scr[...] = jnp.zeros_like(s_scr)

    def scan(t, carry):
        rows, o = chunk(t, 0)
        of_scr[rows, :] = o
        rows, o = chunk(_backward_chunk(t, ctx_chunks, n_chunks), 1)
        ob_scr[rows, :] = o
        return carry

    lax.fori_loop(0, n_chunks, scan, 0, unroll=2)

    def readout(n, carry):
        rows = pl.ds(pl.multiple_of(n * c, c), c)
        o = of_scr[rows, :] + ob_scr[rows, :]
        y = o * lax.rsqrt(jnp.mean(o * o, axis=-1, keepdims=True) + EPS)
        o_ref[rows, :] = (_silu(g_ref[rows, :]) * y).astype(BF16)
        return carry

    lax.fori_loop(0, n_chunks, readout, 0)


def _retention(proj, ret_decay_raw, dims):
    batch, n_ctx, seq = dims
    tb = n_ctx + seq
    slab = pltpu.VMEM((2, tb, LANES), F32)
    return pl.pallas_call(
        functools.partial(_ret_kernel, ctx_chunks=n_ctx // CHUNK, n_chunks=tb // CHUNK),
        out_shape=jax.ShapeDtypeStruct((RET_HEADS, batch * tb, LANES), BF16),
        grid=(batch, RET_HEADS),
        in_specs=[pl.BlockSpec(memory_space=pltpu.SMEM), pl.BlockSpec(memory_space=pl.ANY)],
        out_specs=_head_spec(tb, lambda b, h: (h, b, 0)),
        scratch_shapes=[slab, slab, slab, slab, pltpu.SemaphoreType.DMA((4, 2)),
                        pltpu.VMEM((tb, LANES), F32), pltpu.VMEM((tb, LANES), F32),
                        pltpu.VMEM((2, LANES, LANES), F32),
                        pltpu.VMEM((2, CHUNK, CHUNK), F32), pltpu.VMEM((2, CHUNK, LANES), F32),
                        pltpu.VMEM((2, CHUNK, LANES), F32)],
        compiler_params=_cparams(2),
    )(ret_decay_raw, proj)


def _att_kernel(sink_ref, q0_ref, q1_ref, k_ref, v_ref, o_ref, kb_scr, vb_scr, bias_scr, *, n_ctx, seq):
    hk = pl.program_id(1)
    c = CHUNK
    win = 3 * c
    kb_scr[...] = k_ref[...].astype(BF16)
    vb_scr[...] = v_ref[...].astype(BF16)
    kc = kb_scr[0:n_ctx, :]
    vc = vb_scr[0:n_ctx, :]
    ii = lax.broadcasted_iota(jnp.int32, (c, win), 0)
    jj = lax.broadcasted_iota(jnp.int32, (c, win), 1)
    for off in range(3):
        bias_scr[off] = jnp.where(jnp.abs(ii - jj + off * c) <= WINDOW, 0.0, NEG_INF)

    def softmax_out(s_ctx, sink, s_loc=None, v_loc=None):
        m = jnp.maximum(jnp.max(s_ctx, axis=-1, keepdims=True), sink)
        if s_loc is not None:
            m = jnp.maximum(m, jnp.max(s_loc, axis=-1, keepdims=True))
        p_ctx = jnp.exp(s_ctx - m)
        den = jnp.sum(p_ctx, axis=-1, keepdims=True) + jnp.exp(sink - m)
        o = _dot(p_ctx.astype(BF16), vc)
        if s_loc is not None:
            p_loc = jnp.exp(s_loc - m)
            den = den + jnp.sum(p_loc, axis=-1, keepdims=True)
            o = o + _dot(p_loc.astype(BF16), v_loc)
        return o / den

    heads = []
    for g, q_ref in enumerate((q0_ref, q1_ref)):
        sink = jnp.full((c, 1), sink_ref[hk * ATT_GROUP + g], F32)
        heads.append((q_ref, sink, g))

    for q_ref, sink, g in heads:
        for cc in range(n_ctx // c):
            qb = q_ref[cc * c:(cc + 1) * c, :].astype(BF16)
            o_ref[g, cc * c:(cc + 1) * c, :] = softmax_out(_dot_nt(qb, kc), sink).astype(BF16)

    def block(n, carry):
        q_rows = pl.ds(pl.multiple_of(n_ctx + n * c, c), c)
        start = jnp.clip(n * c - c, 0, seq - win)
        k_rows = pl.ds(pl.multiple_of(n_ctx + start, c), win)
        k_loc = kb_scr[k_rows, :]
        v_loc = vb_scr[k_rows, :]
        bias = bias_scr[(n * c - start) // c]
        for q_ref, sink, g in heads:
            qb = q_ref[q_rows, :].astype(BF16)
            o = softmax_out(_dot_nt(qb, kc), sink, _dot_nt(qb, k_loc) + bias, v_loc)
            o_ref[g, q_rows, :] = o.astype(BF16)
        return carry

    lax.fori_loop(0, seq // c, block, 0)


def _attention(proj, att_sink, dims, col0):
    batch, n_ctx, seq = dims
    tb = n_ctx + seq
    qb, kb, vb = col0, col0 + ATT_Q_HEADS, col0 + ATT_Q_HEADS + ATT_KV_HEADS
    return pl.pallas_call(
        functools.partial(_att_kernel, n_ctx=n_ctx, seq=seq),
        out_shape=jax.ShapeDtypeStruct((ATT_Q_HEADS, batch * tb, LANES), BF16),
        grid=(batch, ATT_KV_HEADS),
        in_specs=[pl.BlockSpec(memory_space=pltpu.SMEM),
                  _head_spec(tb, lambda b, h: (qb + ATT_GROUP * h, b, 0)),
                  _head_spec(tb, lambda b, h: (qb + ATT_GROUP * h + 1, b, 0)),
                  _head_spec(tb, lambda b, h: (kb + h, b, 0)),
                  _head_spec(tb, lambda b, h: (vb + h, b, 0))],
        out_specs=pl.BlockSpec((ATT_GROUP, tb, LANES), lambda b, h: (h, b, 0)),
        scratch_shapes=[pltpu.VMEM((tb, LANES), BF16), pltpu.VMEM((tb, LANES), BF16),
                        pltpu.VMEM((3, CHUNK, 3 * CHUNK), F32)],
        compiler_params=_cparams(2),
    )(att_sink, proj, proj, proj, proj)


HG_LEVELS = 7
HG_MXU_LEVELS = 3


def _hgrn_constants():
    c = CHUNK
    i = np.arange(c)[:, None]
    r = np.arange(c)[None, :]
    mats = [r <= i]
    for m in range(1, HG_MXU_LEVELS + 1):
        half = 1 << (m - 1)
        beta = (i // (2 * half)) * (2 * half) + half - 1
        upper = (i % (2 * half)) >= half
        mats.append((upper & (r > beta) & (r <= i)) | ((~upper) & (r > i) & (r <= beta)))
    fwd = [m.astype(np.float32) for m in mats]
    bwd = [m[::-1, ::-1] for m in fwd]
    stack = np.stack([np.concatenate(fwd, 0), np.concatenate(bwd, 0)])
    x = i ^ r
    level = np.where(r > i, -1, np.where(r == i, 0, np.floor(np.log2(np.maximum(x, 1))) + 1)).astype(np.int32)
    levels = np.stack([level, level[::-1, ::-1]])
    return stack, levels


def _hgrn_kernel(proj_hbm, lbl_ref, nw_ref, m_ref, lv_ref, o_ref, qbuf, zfbuf, zbbuf, vbuf, gbuf, sems,
                 of_scr, ob_scr, st_scr, *, layer, ctx_chunks, n_chunks):
    c = CHUNK
    q_ref, zf_ref, zb_ref, v_ref, g_ref = _head_slabs(
        [proj_hbm] * 5, [j * HG_HEADS for j in range(5)], [qbuf, zfbuf, zbbuf, vbuf, gbuf], sems, HG_HEADS)
    z = lbl_ref[...]
    e = jnp.exp(z - jnp.max(z, axis=0, keepdims=True))
    sm = e / jnp.sum(e, axis=0, keepdims=True)
    lb = jnp.sum(sm[1:layer + 1], axis=0, keepdims=True)
    row = lax.broadcasted_iota(jnp.int32, (c, LANES), 0)
    upper = [[((row if d == 0 else c - 1 - row) & (1 << (m - 1))) != 0 for m in range(1, HG_MXU_LEVELS + 1)]
             for d in range(2)]

    def level_operand(q, k, a, half, d):
        pieces = []
        for base in range(0, c, 2 * half):
            lo, hi = slice(base, base + half), slice(base + half, base + 2 * half)
            if d == 0:
                ref = a[base + half - 1:base + half]
                pieces += [k[lo] * jnp.exp(ref - a[lo]), q[hi] * jnp.exp(a[hi] - ref)]
            else:
                ref = a[base + half:base + half + 1]
                pieces += [q[lo] * jnp.exp(a[lo] - ref), k[hi] * jnp.exp(ref - a[hi])]
        return jnp.concatenate(pieces, axis=0).astype(BF16)

    def chunk(n, d, z_ref):
        rows = pl.ds(pl.multiple_of(n * c, c), c)
        q = q_ref[rows, :]
        v = v_ref[rows, :]
        f = lb + (1.0 - lb) * jax.nn.sigmoid(z_ref[rows, :])
        g = jnp.log(f)
        k = 1.0 - f
        g_hi = g.astype(BF16)
        g_lo = (g - g_hi.astype(F32)).astype(BF16)
        res = _dot(m_ref[d], jnp.concatenate([g_hi, g_lo], axis=1))
        ex = res[:, :c] + res[:, c:]
        a = ex[0:c]
        st = st_scr[d]
        o = _dot_nt((q * jnp.exp(a)).astype(BF16), st.astype(BF16))
        lv = lv_ref[d]
        sc = jnp.where(lv == 0, _dot_nt(q.astype(BF16), k.astype(BF16)), 0.0)
        for m in range(1, HG_LEVELS + 1):
            if m <= HG_MXU_LEVELS:
                zz = (jnp.where(upper[d][m - 1], q, k) * jnp.exp(ex[m * c:(m + 1) * c])).astype(BF16)
            else:
                zz = level_operand(q, k, a, 1 << (m - 1), d)
            sc = jnp.where(lv == m, _dot_nt(zz, zz), sc)
        vb = v.astype(BF16)
        o = o + _dot(sc.astype(BF16), vb)
        a_last = a[c - 1:c] if d == 0 else a[0:1]
        kd = (k * jnp.exp(a_last - a)).astype(BF16)
        st_scr[d] = st * jnp.exp(a_last) + _dot_tn(vb, kd)
        return rows, o

    st_scr[...] = jnp.zeros_like(st_scr)

    def scan(t, carry):
        rows, o = chunk(t, 0, zf_ref)
        of_scr[rows, :] = o
        rows, o = chunk(_backward_chunk(t, ctx_chunks, n_chunks), 1, zb_ref)
        ob_scr[rows, :] = o
        return carry

    lax.fori_loop(0, n_chunks, scan, 0, unroll=2)

    def readout(n, carry):
        rows = pl.ds(pl.multiple_of(n * c, c), c)
        o = of_scr[rows, :] + ob_scr[rows, :]
        y = o * lax.rsqrt(jnp.mean(o * o, axis=-1, keepdims=True) + EPS) * nw_ref[...]
        o_ref[rows, :] = (y * _silu(g_ref[rows, :])).astype(BF16)
        return carry

    lax.fori_loop(0, n_chunks, readout, 0)


def _hgrn2(proj, hg_lb_logits, hg_norm_w, layer, dims):
    batch, n_ctx, seq = dims
    tb = n_ctx + seq
    depth = hg_lb_logits.shape[0]
    stack, levels = _hgrn_constants()
    slab = pltpu.VMEM((2, tb, LANES), F32)
    n_mats = 1 + HG_MXU_LEVELS
    return pl.pallas_call(
        functools.partial(_hgrn_kernel, layer=layer, ctx_chunks=n_ctx // CHUNK, n_chunks=tb // CHUNK),
        out_shape=jax.ShapeDtypeStruct((HG_HEADS, batch * tb, LANES), BF16),
        grid=(batch, HG_HEADS),
        in_specs=[pl.BlockSpec(memory_space=pl.ANY),
                  pl.BlockSpec((depth, LANES), lambda b, h: (0, h)),
                  pl.BlockSpec((1, LANES), lambda b, h: (0, 0)),
                  pl.BlockSpec((2, n_mats * CHUNK, CHUNK), lambda b, h: (0, 0, 0)),
                  pl.BlockSpec((2, CHUNK, CHUNK), lambda b, h: (0, 0, 0))],
        out_specs=_head_spec(tb, lambda b, h: (h, b, 0)),
        scratch_shapes=[slab, slab, slab, slab, slab, pltpu.SemaphoreType.DMA((5, 2)),
                        pltpu.VMEM((tb, LANES), F32), pltpu.VMEM((tb, LANES), F32),
                        pltpu.VMEM((2, LANES, LANES), F32)],
        compiler_params=_cparams(2),
    )(proj, hg_lb_logits, hg_norm_w.reshape(1, LANES),
      jnp.asarray(stack, BF16), jnp.asarray(levels, jnp.int32))


SUBLANES = 8


def _store_token_tiles(ref, x):
    rows, tiles = x.shape[0], x.shape[1] // LANES
    for s in range(tiles):
        ref[pl.ds(s, rows, stride=tiles), :] = x[:, s * LANES:(s + 1) * LANES]


def _load_token_tiles(ref, tiles):
    rows = ref.shape[0] // tiles
    return jnp.concatenate([ref[pl.ds(s, rows, stride=tiles), :] for s in range(tiles)], axis=1)


def _route(logits, tri, counts):
    lane = lax.broadcasted_iota(jnp.int32, logits.shape, 1).astype(F32)
    big = float(LANES)
    first = lambda hit: jnp.min(jnp.where(hit, lane, big), axis=-1, keepdims=True)
    is_g = lane < MOE_GROUPS
    gl = jnp.where(is_g, logits, NEG_INF)
    gmax = jnp.max(gl, axis=-1, keepdims=True)
    gsum = jnp.sum(jnp.where(is_g, jnp.exp(gl - gmax), 0.0), axis=-1, keepdims=True)
    g_p = 1.0 / gsum
    lo = MOE_GROUPS + first(gl == gmax) * MOE_EPG
    el = jnp.where((lane >= lo) & (lane < lo + MOE_EPG), logits, NEG_INF)
    e1v = jnp.max(el, axis=-1, keepdims=True)
    e1 = first(el == e1v)
    el2 = jnp.where(lane == e1, NEG_INF, el)
    e2v = jnp.max(el2, axis=-1, keepdims=True)
    e2 = first(el2 == e2v)
    t = jnp.exp(e2v - e1v)
    w1 = 1.0 / (1.0 + t)
    w2 = t * w1
    id1 = e1 - MOE_GROUPS
    id2 = e2 - MOE_GROUPS
    onehot = jnp.where((lane == id1) | (lane == id2), 1.0, 0.0)
    before = counts + _dot(tri, onehot.astype(BF16))
    rank1 = jnp.sum(jnp.where(lane == id1, before, 0.0), axis=-1, keepdims=True)
    rank2 = jnp.sum(jnp.where(lane == id2, before, 0.0), axis=-1, keepdims=True)
    slab = jnp.zeros_like(logits)
    for col, val in enumerate((id1, id2, g_p * w1, g_p * w2, rank1, rank2)):
        slab = jnp.where(lane == col, val, slab)
    return slab, counts + jnp.sum(onehot, axis=0, keepdims=True)


def _out_route_kernel(*refs, n_parts):
    a_refs = refs[:n_parts]
    (w_ref, x_ref, gate_ref, nw_ref, shift_ref, scale_ref, wr_ref, br_ref, tri_ref,
     xo_ref, h2_ref, route_ref, cnt_ref) = refs[n_parts:]

    @pl.when(pl.program_id(0) == 0)
    def _():
        cnt_ref[...] = jnp.zeros_like(cnt_ref)

    k0 = 0
    y = None
    for a_ref in a_refs:
        a = jnp.concatenate([a_ref[h] for h in range(a_ref.shape[0])], axis=1)
        kk = a.shape[1]
        t = _dot(a, w_ref[k0:k0 + kk, :])
        y = t if y is None else y + t
        k0 += kk
    x = x_ref[...] + gate_ref[0] * y
    xo_ref[...] = x
    h2 = _norm_mod(x, nw_ref[...], scale_ref[0], shift_ref[0])
    _store_token_tiles(h2_ref, h2)
    h_hi = h2.astype(BF16)
    h_lo = (h2 - h_hi.astype(F32)).astype(BF16)
    both = _dot(h_hi, wr_ref[...])
    logits = both[:, :LANES] + both[:, LANES:] + _dot(h_lo, wr_ref[:, :LANES]) + br_ref[...]
    slab, counts = _route(logits, tri_ref[...], cnt_ref[0:1, :])
    route_ref[...] = slab
    cnt_ref[...] = jnp.broadcast_to(counts, cnt_ref.shape)


def _out_projection_route(parts, w_bf, xs, nw, mods, wr, br, dims):
    nt, d = xs.shape
    batch, n_ctx, seq = dims
    tpb = (n_ctx + seq) // ROW_TILE
    ctx_tiles = n_ctx // ROW_TILE
    row = lambda width: pl.BlockSpec((ROW_TILE, width), lambda i: (i, 0))
    full = lambda shape: pl.BlockSpec(shape, lambda i: (0,) * len(shape))
    mod = lambda chunk_idx: _mod_spec(chunk_idx, d, tpb, ctx_tiles, batch)
    tri = jnp.asarray(np.tril(np.ones((ROW_TILE, ROW_TILE), np.float32), -1), BF16)
    return pl.pallas_call(
        functools.partial(_out_route_kernel, n_parts=len(parts)),
        out_shape=(jax.ShapeDtypeStruct((nt, d), F32), jax.ShapeDtypeStruct((nt * d // LANES, LANES), F32),
                   jax.ShapeDtypeStruct((nt, LANES), F32), jax.ShapeDtypeStruct((8, LANES), F32)),
        grid=(nt // ROW_TILE,),
        in_specs=[pl.BlockSpec((p.shape[0], ROW_TILE, LANES), lambda i: (0, i, 0)) for p in parts] + [
            full(w_bf.shape), row(d), mod(2), full((1, d)), mod(3), mod(4), full(wr.shape), full(br.shape),
            full(tri.shape)],
        out_specs=(row(d), pl.BlockSpec((ROW_TILE * d // LANES, LANES), lambda i: (i, 0)), row(LANES),
                   full((8, LANES))),
        compiler_params=_cparams(1),
    )(*parts, w_bf, xs, mods, nw.reshape(1, d), mods, mods, wr, br, tri)


SCATTER_DMA_PRIORITY = 1
MOE_BUFFERS = 4


def _dispatch_plan(route, counts_f, n_tok, tiles):
    n_assign = n_tok * MOE_TOP_K
    flat_e = route[:, 0:MOE_TOP_K].astype(jnp.int32).reshape(-1)
    rank = route[:, 4:4 + MOE_TOP_K].astype(jnp.int32).reshape(-1)
    counts = counts_f[0, :MOE_EXPERTS].astype(jnp.int32)
    padded = (counts + MOE_BLOCK - 1) // MOE_BLOCK * MOE_BLOCK
    pad_end = jnp.cumsum(padded)
    pad_start = pad_end - padded
    dest = pad_start[flat_e] + rank
    n_blocks = -(-n_assign // MOE_BLOCK) + MOE_EXPERTS
    n_slots = n_blocks * MOE_BLOCK
    block_start = jnp.arange(n_blocks, dtype=jnp.int32) * MOE_BLOCK
    block_expert = jnp.minimum(jnp.sum((pad_end[None, :] <= block_start[:, None]).astype(jnp.int32), axis=1),
                               MOE_EXPERTS - 1)
    slot_assign = jnp.full((n_slots,), -1, jnp.int32).at[dest].set(jnp.arange(n_assign, dtype=jnp.int32))
    lead, trail = MOE_BLOCK, (MOE_BUFFERS - 1) * MOE_BLOCK
    slot = jnp.arange(-lead, n_slots + trail, dtype=jnp.int32)
    spare = n_assign + ((slot // MOE_BLOCK) % MOE_BUFFERS) * MOE_BLOCK + slot % MOE_BLOCK
    assign = jnp.concatenate([jnp.full((lead,), -1, jnp.int32), slot_assign, jnp.full((trail,), -1, jnp.int32)])
    src = jnp.where(assign >= 0, assign // MOE_TOP_K, 0) * tiles
    dst = jnp.where(assign >= 0, (assign % MOE_TOP_K) * n_tok + assign // MOE_TOP_K, spare) * tiles
    n_used = (pad_end[-1:] // MOE_BLOCK).astype(jnp.int32)
    return block_expert, n_used, src[lead:], dst[:n_slots + lead], n_blocks


def _expert_kernel(be_ref, nused_ref, src_ref, dst_ref, h_hbm, w1_ref, w3_ref, w2_ref, out_hbm,
                   xbuf, ybuf, w1b, w3b, w2b, gsem, ssem):
    i = pl.program_id(0)
    n_used = nused_ref[0]
    nb = MOE_BUFFERS
    cur = i % nb
    buffer_of = lambda blk: (blk + nb) % nb

    tiles = xbuf.shape[1] // MOE_BLOCK

    def token(idx):
        return pl.ds(pl.multiple_of(idx, tiles), tiles)

    def start_gather(blk, buf):
        for r in range(MOE_BLOCK):
            pltpu.make_async_copy(h_hbm.at[token(src_ref[blk * MOE_BLOCK + r])],
                                  xbuf.at[buf, pl.ds(r * tiles, tiles)], gsem.at[buf]).start()

    def start_scatter(blk, buf):
        for r in range(MOE_BLOCK):
            pltpu.make_async_copy(ybuf.at[buf, pl.ds(r * tiles, tiles)],
                                  out_hbm.at[token(dst_ref[(blk + 1) * MOE_BLOCK + r])], ssem.at[buf]).start(
                                      priority=SCATTER_DMA_PRIORITY)

    def wait_gather(buf):
        pltpu.make_async_copy(h_hbm.at[pl.ds(0, MOE_BLOCK * tiles)], xbuf.at[buf], gsem.at[buf]).wait()

    def wait_scatter(buf):
        pltpu.make_async_copy(ybuf.at[buf], out_hbm.at[pl.ds(0, MOE_BLOCK * tiles)], ssem.at[buf]).wait()

    @pl.when(i < n_used)
    def _():
        @pl.when(i == 0)
        def _():
            ybuf[...] = jnp.zeros_like(ybuf)
            block_rows = MOE_BLOCK * tiles
            spare0 = out_hbm.shape[0] - nb * block_rows
            for b in range(nb):
                spare = pltpu.make_async_copy(ybuf.at[b], out_hbm.at[pl.ds(spare0 + b * block_rows, block_rows)],
                                              ssem.at[b])
                spare.start()
                spare.wait()
            for blk in range(nb - 1):
                start_gather(blk, blk)

        @pl.when((i == 0) | (be_ref[i] != be_ref[jnp.maximum(i - 1, 0)]))
        def _():
            w1b[...] = w1_ref[0, 0].astype(BF16)
            w3b[...] = w3_ref[0, 0].astype(BF16)
            w2b[...] = w2_ref[0, 0].astype(BF16)

        wait_gather(cur)

        @pl.when(i >= nb - 1)
        def _():
            wait_scatter(cur)

        xb = _load_token_tiles(xbuf.at[cur], tiles).astype(BF16)
        start_gather(i + nb - 1, buffer_of(i - 1))
        start_scatter(i - 1, buffer_of(i - 1))
        act = _silu(_dot(xb, w1b[...])) * _dot(xb, w3b[...])
        _store_token_tiles(ybuf.at[cur], _dot(act.astype(BF16), w2b[...]))

        @pl.when(i == n_used - 1)
        def _():
            start_scatter(i, cur)
            for back in range(nb):
                @pl.when(i - back >= -1)
                def _():
                    wait_scatter(buffer_of(i - back))
            for ahead in range(1, nb):
                wait_gather(buffer_of(i + ahead))


def _experts(h2, plan, w1, w3, w2, layer):
    block_expert, n_used, slot_src, slot_dst, n_blocks = plan
    d, ff = w1.shape[2], w1.shape[3]
    tiles = d // LANES
    nt = h2.shape[0] // tiles
    wspec = lambda shape: pl.BlockSpec((1, 1) + shape, lambda i, be, nu, sr, ds: (layer, be[i], 0, 0))
    row_buffers = pltpu.VMEM((MOE_BUFFERS, MOE_BLOCK * tiles, LANES), F32)
    grid_spec = pltpu.PrefetchScalarGridSpec(
        num_scalar_prefetch=4,
        grid=(n_blocks,),
        in_specs=[pl.BlockSpec(memory_space=pl.ANY), wspec((d, ff)), wspec((d, ff)), wspec((ff, d))],
        out_specs=pl.BlockSpec(memory_space=pl.ANY),
        scratch_shapes=[row_buffers, row_buffers,
                        pltpu.VMEM((d, ff), BF16), pltpu.VMEM((d, ff), BF16), pltpu.VMEM((ff, d), BF16),
                        pltpu.SemaphoreType.DMA((MOE_BUFFERS,)), pltpu.SemaphoreType.DMA((MOE_BUFFERS,))])
    return pl.pallas_call(
        _expert_kernel,
        out_shape=jax.ShapeDtypeStruct(((nt * MOE_TOP_K + MOE_BUFFERS * MOE_BLOCK) * tiles, LANES), F32),
        grid_spec=grid_spec,
        compiler_params=_cparams(1, has_side_effects=True, disable_bounds_checks=True),
    )(block_expert, n_used, slot_src, slot_dst, h2, w1, w3, w2)


def _combine_kernel(x_ref, y0_ref, y1_ref, route_ref, gate_ref, *rest, final):
    r = route_ref[...]
    tiles = x_ref.shape[1] // LANES
    y = r[:, 2:3] * _load_token_tiles(y0_ref, tiles) + r[:, 3:4] * _load_token_tiles(y1_ref, tiles)
    x = x_ref[...] + gate_ref[0] * y
    if final:
        fnw_ref, o_ref = rest
        x = x * lax.rsqrt(jnp.mean(x * x, axis=-1, keepdims=True) + EPS) * fnw_ref[...]
    else:
        (o_ref,) = rest
    o_ref[...] = x


def _combine(xs, y2, route, mods, dims, final_norm_w=None):
    nt, d = xs.shape
    batch, n_ctx, seq = dims
    tpb = (n_ctx + seq) // ROW_TILE
    ctx_tiles = n_ctx // ROW_TILE
    row = lambda width: pl.BlockSpec((ROW_TILE, width), lambda i: (i, 0))
    final = final_norm_w is not None
    first = pl.BlockSpec((ROW_TILE * d // LANES, LANES), lambda i: (i, 0))
    second = pl.BlockSpec((ROW_TILE * d // LANES, LANES), lambda i: (i + nt // ROW_TILE, 0))
    in_specs = [row(d), first, second, row(LANES), _mod_spec(5, d, tpb, ctx_tiles, batch)]
    args = [xs, y2, y2, route, mods]
    if final:
        in_specs.append(pl.BlockSpec((1, d), lambda i: (0, 0)))
        args.append(final_norm_w.reshape(1, d))
    return pl.pallas_call(
        functools.partial(_combine_kernel, final=final),
        out_shape=jax.ShapeDtypeStruct((nt, d), F32),
        grid=(nt // ROW_TILE,),
        in_specs=in_specs,
        out_specs=row(d),
        compiler_params=_cparams(1),
    )(*args)


def _even_col_ops():
    rh = RET_HEADS
    ops = [(True, 1.0, 1.0)] * rh + [(True, float(HEAD_DIM) ** -0.5, 1.0)] * rh + [(False, 1.0, 1.0)] * (2 * rh)
    ops += [(True, 1.0, float(HEAD_DIM) ** -0.5)] * ATT_Q_HEADS + [(True, 1.0, 1.0)] * ATT_KV_HEADS
    ops += [(False, 1.0, 1.0)] * ATT_KV_HEADS
    return tuple(ops)


def kernel(x, c, ctx, c_ctx, ada_w, ada_b, norm_w, final_norm_w, ev_w_in, ev_w_out, ret_decay_raw, att_sink,
           od_w_in, od_w_out, hg_lb_logits, hg_norm_w, moe_wg, moe_bg, moe_we, moe_be, moe_w1, moe_w3, moe_w2):
    batch, seq, d = x.shape
    n_ctx = ctx.shape[1]
    depth = ada_w.shape[0]
    assert batch + 1 <= MOD_ROWS and n_ctx % ROW_TILE == 0 and seq % ROW_TILE == 0
    assert seq % GRID_W == 0 and seq >= 3 * CHUNK
    dims = (batch, n_ctx, seq)
    tb = n_ctx + seq
    nt = batch * tb

    xs = jnp.concatenate([ctx, x], axis=1).reshape(nt, d)
    cvec = jnp.concatenate([c, c_ctx[None, :], jnp.zeros((MOD_ROWS - batch - 1, d), F32)], axis=0)
    mods_all = _ada_modulation(cvec, ada_w, ada_b)
    rope = _rope_tables(n_ctx, seq)
    even_ops = _even_col_ops()

    for l in range(depth):
        p = l // 2
        mods = mods_all[l].reshape(MOD_ROWS, 1, 6 * d)
        if l % 2 == 0:
            proj = _in_projection(xs, norm_w[l, 0], mods, ev_w_in[p].astype(BF16), dims, even_ops, rope)
            ret = _retention(proj, ret_decay_raw[p], dims)
            att = _attention(proj, att_sink[p], dims, 4 * RET_HEADS)
            parts, w_out = [ret, att], ev_w_out[p]
        else:
            proj = _in_projection(xs, norm_w[l, 0], mods, od_w_in[p].astype(BF16), dims)
            parts, w_out = [_hgrn2(proj, hg_lb_logits, hg_norm_w[p], l, dims)], od_w_out[p]
        n_logit = MOE_GROUPS + MOE_EXPERTS
        wr = jnp.concatenate([moe_wg[l], jnp.moveaxis(moe_we[l], 0, 1).reshape(d, MOE_EXPERTS),
                              jnp.zeros((d, LANES - n_logit), F32)], axis=1)
        wr_hi = wr.astype(BF16)
        wr = jnp.concatenate([wr_hi, (wr - wr_hi.astype(F32)).astype(BF16)], axis=1)
        br = jnp.concatenate([moe_bg[l], moe_be[l].reshape(-1), jnp.zeros((LANES - n_logit,), F32)])[None, :]
        xs, h2, route, counts = _out_projection_route(
            parts, w_out.astype(BF16), xs, norm_w[l, 1], mods, wr, br, dims)
        plan = _dispatch_plan(route, counts, nt, d // LANES)
        y2 = _experts(h2, plan, moe_w1, moe_w3, moe_w2, l)
        xs = _combine(xs, y2, route, mods, dims, final_norm_w if l == depth - 1 else None)

    return xs.reshape(batch, tb, d)[:, n_ctx:]
```

```python
import functools

import numpy as np
import jax
import jax.numpy as jnp
from jax import lax
from jax.experimental import pallas as pl
from jax.experimental.pallas import tpu as pltpu

F32 = jnp.float32
BF16 = jnp.bfloat16
HIGHEST = lax.Precision.HIGHEST

EPS = 1e-6
NEG_INF = -1e30
LOG2_E = 1.4426950408889634
LANES = 128
GRID_W = 64
ROPE_THETA = 10000.0
HEAD_DIM = 128
RET_HEADS = 4
ATT_Q_HEADS = 4
ATT_KV_HEADS = 2
ATT_GROUP = ATT_Q_HEADS // ATT_KV_HEADS
WINDOW = 128
HG_HEADS = 8
MOE_GROUPS = 4
MOE_EPG = 8
MOE_EXPERTS = MOE_GROUPS * MOE_EPG
MOE_TOP_K = 2
MOE_BLOCK = 128
CHUNK = 128
ROW_TILE = 256
MOD_ROWS = 8
VMEM_LIMIT = 56 * 1024 * 1024


def _cparams(n_axes, **kw):
    return pltpu.CompilerParams(dimension_semantics=("arbitrary",) * n_axes,
                                vmem_limit_bytes=VMEM_LIMIT, **kw)


def _dot(a, b):
    return jnp.dot(a, b, preferred_element_type=F32)


def _dot_nt(a, b):
    return lax.dot_general(a, b, (((1,), (1,)), ((), ())), preferred_element_type=F32)


def _dot_tn(a, b):
    return lax.dot_general(a, b, (((0,), (0,)), ((), ())), preferred_element_type=F32)


def _silu(x):
    return x * jax.nn.sigmoid(x)


def _norm_mod(x, nw, scale, shift):
    ms = jnp.mean(x * x, axis=-1, keepdims=True)
    y = x * lax.rsqrt(ms + EPS) * nw
    return y * (1.0 + scale) + shift


def _ada_kernel(c_ref, w_ref, b_ref, o_ref):
    s = _silu(c_ref[...])
    o_ref[0] = jnp.dot(s, w_ref[0], precision=HIGHEST, preferred_element_type=F32) + b_ref[0]


def _ada_modulation(cvec, ada_w, ada_b):
    depth, d, n6 = ada_w.shape
    tn = 1536
    return pl.pallas_call(
        _ada_kernel,
        out_shape=jax.ShapeDtypeStruct((depth, MOD_ROWS, n6), F32),
        grid=(depth, n6 // tn),
        in_specs=[pl.BlockSpec((MOD_ROWS, d), lambda l, j: (0, 0)),
                  pl.BlockSpec((1, d, tn), lambda l, j: (l, 0, j)),
                  pl.BlockSpec((1, 1, tn), lambda l, j: (l, 0, j))],
        out_specs=pl.BlockSpec((1, MOD_ROWS, tn), lambda l, j: (l, 0, j)),
        compiler_params=_cparams(2),
    )(cvec, ada_w, ada_b.reshape(depth, 1, n6))


def _swap_halves(x):
    lane = lax.broadcasted_iota(jnp.int32, x.shape, 1)
    return jnp.where((lane % 64) < 32, pltpu.roll(x, 96, 1), pltpu.roll(x, 32, 1))


def _proj_kernel(x_ref, nw_ref, shift_ref, scale_ref, w_ref, *rest, col_ops, chunk):
    if col_ops is None:
        (o_ref,) = rest
    else:
        cos_ref, sin_ref, o_ref = rest
    h = _norm_mod(x_ref[...], nw_ref[...], scale_ref[0], shift_ref[0]).astype(BF16)
    nout = o_ref.shape[0] * LANES
    for c0 in range(0, nout, chunk):
        acc = _dot(h, w_ref[:, c0:c0 + chunk])
        for hd in range(chunk // LANES):
            col = c0 + hd * LANES
            a = acc[:, hd * LANES:(hd + 1) * LANES]
            if col_ops is not None:
                rope, pre, post = col_ops[col // LANES]
                if pre != 1.0:
                    a = a * pre
                if rope:
                    a = a * cos_ref[...] + _swap_halves(a) * sin_ref[...]
                if post != 1.0:
                    a = a * post
            o_ref[col // LANES] = a


def _tile_mod_row(i, tiles_per_batch, ctx_tiles, batch):
    return jnp.where(i % tiles_per_batch < ctx_tiles, batch, i // tiles_per_batch)


def _mod_spec(chunk_idx, d, tiles_per_batch, ctx_tiles, batch):
    return pl.BlockSpec(
        (1, 1, d), lambda i: (_tile_mod_row(i, tiles_per_batch, ctx_tiles, batch), 0, chunk_idx))


def _in_projection(xs, nw, mods, w_bf, dims, col_ops=None, rope=None):
    nt, d = xs.shape
    nout = w_bf.shape[1]
    batch, n_ctx, seq = dims
    tpb = (n_ctx + seq) // ROW_TILE
    ctx_tiles = n_ctx // ROW_TILE
    in_specs = [pl.BlockSpec((ROW_TILE, d), lambda i: (i, 0)),
                pl.BlockSpec((1, d), lambda i: (0, 0)),
                _mod_spec(0, d, tpb, ctx_tiles, batch),
                _mod_spec(1, d, tpb, ctx_tiles, batch),
                pl.BlockSpec((d, nout), lambda i: (0, 0))]
    args = [xs, nw.reshape(1, d), mods, mods, w_bf]
    if col_ops is not None:
        rope_spec = pl.BlockSpec((ROW_TILE, LANES), lambda i: (i % tpb, 0))
        in_specs += [rope_spec, rope_spec]
        args += [rope[0], rope[1]]
    return pl.pallas_call(
        functools.partial(_proj_kernel, col_ops=col_ops, chunk=512),
        out_shape=jax.ShapeDtypeStruct((nout // LANES, nt, LANES), F32),
        grid=(nt // ROW_TILE,),
        in_specs=in_specs,
        out_specs=pl.BlockSpec((nout // LANES, ROW_TILE, LANES), lambda i: (0, i, 0)),
        compiler_params=_cparams(1),
    )(*args)


def _rope_tables(n_ctx, seq):
    n_rows = seq // GRID_W
    row = jnp.repeat(jnp.arange(n_rows, dtype=F32), GRID_W)
    col = jnp.tile(jnp.arange(GRID_W, dtype=F32), n_rows)
    axis_dim = HEAD_DIM // 2
    inv_freq = ROPE_THETA ** (-jnp.arange(0, axis_dim, 2, dtype=F32) / axis_dim)
    ang_r = row[:, None] * inv_freq[None, :]
    ang_c = col[:, None] * inv_freq[None, :]
    cos = jnp.concatenate([jnp.cos(ang_r), jnp.cos(ang_r), jnp.cos(ang_c), jnp.cos(ang_c)], axis=-1)
    sin = jnp.concatenate([-jnp.sin(ang_r), jnp.sin(ang_r), -jnp.sin(ang_c), jnp.sin(ang_c)], axis=-1)
    cos = jnp.concatenate([jnp.ones((n_ctx, HEAD_DIM), F32), cos], axis=0)
    sin = jnp.concatenate([jnp.zeros((n_ctx, HEAD_DIM), F32), sin], axis=0)
    return cos, sin


def _head_spec(rows, index_map):
    return pl.BlockSpec((None, rows, LANES), index_map)


HEAD_DMA_ROWS = 256


def _head_slabs(hbm_refs, head_offsets, bufs, sems, n_heads):
    step = pl.program_id(0) * n_heads + pl.program_id(1)
    n_steps = pl.num_programs(0) * n_heads
    rows = bufs[0].shape[1]
    piece = HEAD_DMA_ROWS
    assert rows % piece == 0

    def start(at_step, slot):
        b, h = at_step // n_heads, at_step % n_heads
        for j, (ref, off) in enumerate(zip(hbm_refs, head_offsets)):
            for p in range(rows // piece):
                pltpu.make_async_copy(ref.at[off + h, pl.ds(b * rows + p * piece, piece)],
                                      bufs[j].at[slot, pl.ds(p * piece, piece)], sems.at[j, slot]).start()

    slot = step % 2

    @pl.when(step == 0)
    def _():
        start(step, slot)

    @pl.when(step + 1 < n_steps)
    def _():
        start(step + 1, 1 - slot)

    for j, ref in enumerate(hbm_refs):
        pltpu.make_async_copy(ref.at[0, pl.ds(0, rows)], bufs[j].at[slot], sems.at[j, slot]).wait()
    return [buf.at[slot] for buf in bufs]


def _backward_chunk(t, ctx_chunks, n_chunks):
    return jnp.where(t < ctx_chunks, ctx_chunks - 1 - t, n_chunks - 1 - (t - ctx_chunks))


def _ret_kernel(raw_ref, proj_hbm, o_ref, qbuf, kbuf, vbuf, gbuf, sems, of_scr, ob_scr, s_scr, dec_scr, qd_scr,
                kd_scr, *, ctx_chunks, n_chunks):
    h = pl.program_id(1)
    c = CHUNK
    q_ref, k_ref, v_ref, g_ref = _head_slabs(
        [proj_hbm] * 4, [0, RET_HEADS, 2 * RET_HEADS, 3 * RET_HEADS], [qbuf, kbuf, vbuf, gbuf], sems, RET_HEADS)
    ii = lax.broadcasted_iota(jnp.int32, (c, c), 0).astype(F32)
    jj = lax.broadcasted_iota(jnp.int32, (c, c), 1).astype(F32)
    lg = []
    for d in range(2):
        lgd = -jnp.exp(jnp.full((c, c), raw_ref[d, h], F32))
        lg.append(lgd)
        rel = (ii - jj) if d == 0 else (jj - ii)
        dec_scr[d] = jnp.where(rel >= 0, jnp.exp(jnp.maximum(rel, 0.0) * lgd), 0.0)
        qd_scr[d] = jnp.exp(((ii + 1.0) if d == 0 else (c - ii)) * lgd)
        kd_scr[d] = jnp.exp(((c - 1.0 - ii) if d == 0 else ii) * lgd)

    def chunk(n, d):
        rows = pl.ds(pl.multiple_of(n * c, c), c)
        q = q_ref[rows, :]
        k = k_ref[rows, :]
        vb = v_ref[rows, :].astype(BF16)
        sc = _dot_nt(q.astype(BF16), k.astype(BF16)) * dec_scr[d]
        s = s_scr[d]
        o = _dot(sc.astype(BF16), vb) + _dot((q * qd_scr[d]).astype(BF16), s.astype(BF16))
        chunk_decay = jnp.exp(float(c) * lg[d][0:1, :])
        s_scr[d] = chunk_decay * s + _dot_tn((k * kd_scr[d]).astype(BF16), vb)
        return rows, o

    s_scr[...] = jnp.zeros_like(s_scr)

    def scan(t, carry):
        rows, o = chunk(t, 0)
        of_scr[rows, :] = o
        rows, o = chunk(_backward_chunk(t, ctx_chunks, n_chunks), 1)
        ob_scr[rows, :] = o
        return carry

    lax.fori_loop(0, n_chunks, scan, 0, unroll=2)

    def readout(n, carry):
        rows = pl.ds(pl.multiple_of(n * c, c), c)
        o = of_scr[rows, :] + ob_scr[rows, :]
        y = o * lax.rsqrt(jnp.mean(o * o, axis=-1, keepdims=True) + EPS)
        o_ref[rows, :] = (_silu(g_ref[rows, :]) * y).astype(BF16)
        return carry

    lax.fori_loop(0, n_chunks, readout, 0)


def _retention(proj, ret_decay_raw, dims):
    batch, n_ctx, seq = dims
    tb = n_ctx + seq
    slab = pltpu.VMEM((2, tb, LANES), F32)
    return pl.pallas_call(
        functools.partial(_ret_kernel, ctx_chunks=n_ctx // CHUNK, n_chunks=tb // CHUNK),
        out_shape=jax.ShapeDtypeStruct((RET_HEADS, batch * tb, LANES), BF16),
        grid=(batch, RET_HEADS),
        in_specs=[pl.BlockSpec(memory_space=pltpu.SMEM), pl.BlockSpec(memory_space=pl.ANY)],
        out_specs=_head_spec(tb, lambda b, h: (h, b, 0)),
        scratch_shapes=[slab, slab, slab, slab, pltpu.SemaphoreType.DMA((4, 2)),
                        pltpu.VMEM((tb, LANES), F32), pltpu.VMEM((tb, LANES), F32),
                        pltpu.VMEM((2, LANES, LANES), F32),
                        pltpu.VMEM((2, CHUNK, CHUNK), F32), pltpu.VMEM((2, CHUNK, LANES), F32),
                        pltpu.VMEM((2, CHUNK, LANES), F32)],
        compiler_params=_cparams(2),
    )(ret_decay_raw, proj)


def _att_kernel(sink_ref, q0_ref, q1_ref, k_ref, v_ref, o_ref, kb_scr, vb_scr, bias_scr, *, n_ctx, seq):
    hk = pl.program_id(1)
    c = CHUNK
    win = 3 * c
    kb_scr[...] = k_ref[...].astype(BF16)
    vb_scr[...] = v_ref[...].astype(BF16)
    kc = kb_scr[0:n_ctx, :]
    vc = vb_scr[0:n_ctx, :]
    ii = lax.broadcasted_iota(jnp.int32, (c, win), 0)
    jj = lax.broadcasted_iota(jnp.int32, (c, win), 1)
    for off in range(3):
        bias_scr[off] = jnp.where(jnp.abs(ii - jj + off * c) <= WINDOW, 0.0, NEG_INF)

    def softmax_out(s_ctx, sink, s_loc=None, v_loc=None):
        m = jnp.maximum(jnp.max(s_ctx, axis=-1, keepdims=True), sink)
        if s_loc is not None:
            m = jnp.maximum(m, jnp.max(s_loc, axis=-1, keepdims=True))
        p_ctx = jnp.exp2(s_ctx - m)
        den = jnp.sum(p_ctx, axis=-1, keepdims=True) + jnp.exp2(sink - m)
        o = _dot(p_ctx.astype(BF16), vc)
        if s_loc is not None:
            p_loc = jnp.exp2(s_loc - m)
            den = den + jnp.sum(p_loc, axis=-1, keepdims=True)
            o = o + _dot(p_loc.astype(BF16), v_loc)
        return o / den

    heads = []
    for g, q_ref in enumerate((q0_ref, q1_ref)):
        sink = jnp.full((c, 1), sink_ref[hk * ATT_GROUP + g], F32) * LOG2_E
        heads.append((q_ref, sink, g))

    for q_ref, sink, g in heads:
        for cc in range(n_ctx // c):
            qb = q_ref[cc * c:(cc + 1) * c, :].astype(BF16)
            o_ref[g, cc * c:(cc + 1) * c, :] = softmax_out(_dot_nt(qb, kc), sink).astype(BF16)

    def block(n, carry):
        q_rows = pl.ds(pl.multiple_of(n_ctx + n * c, c), c)
        start = jnp.clip(n * c - c, 0, seq - win)
        k_rows = pl.ds(pl.multiple_of(n_ctx + start, c), win)
        k_loc = kb_scr[k_rows, :]
        v_loc = vb_scr[k_rows, :]
        bias = bias_scr[(n * c - start) // c]
        for q_ref, sink, g in heads:
            qb = q_ref[q_rows, :].astype(BF16)
            o = softmax_out(_dot_nt(qb, kc), sink, _dot_nt(qb, k_loc) + bias, v_loc)
            o_ref[g, q_rows, :] = o.astype(BF16)
        return carry

    lax.fori_loop(0, seq // c, block, 0)


def _attention(proj, att_sink, dims, col0):
    batch, n_ctx, seq = dims
    tb = n_ctx + seq
    qb, kb, vb = col0, col0 + ATT_Q_HEADS, col0 + ATT_Q_HEADS + ATT_KV_HEADS
    return pl.pallas_call(
        functools.partial(_att_kernel, n_ctx=n_ctx, seq=seq),
        out_shape=jax.ShapeDtypeStruct((ATT_Q_HEADS, batch * tb, LANES), BF16),
        grid=(batch, ATT_KV_HEADS),
        in_specs=[pl.BlockSpec(memory_space=pltpu.SMEM),
                  _head_spec(tb, lambda b, h: (qb + ATT_GROUP * h, b, 0)),
                  _head_spec(tb, lambda b, h: (qb + ATT_GROUP * h + 1, b, 0)),
                  _head_spec(tb, lambda b, h: (kb + h, b, 0)),
                  _head_spec(tb, lambda b, h: (vb + h, b, 0))],
        out_specs=pl.BlockSpec((ATT_GROUP, tb, LANES), lambda b, h: (h, b, 0)),
        scratch_shapes=[pltpu.VMEM((tb, LANES), BF16), pltpu.VMEM((tb, LANES), BF16),
                        pltpu.VMEM((3, CHUNK, 3 * CHUNK), F32)],
        compiler_params=_cparams(2),
    )(att_sink, proj, proj, proj, proj)


HG_LEVELS = 7
HG_MXU_LEVELS = 3


def _hgrn_constants():
    c = CHUNK
    i = np.arange(c)[:, None]
    r = np.arange(c)[None, :]
    mats = [r <= i]
    for m in range(1, HG_MXU_LEVELS + 1):
        half = 1 << (m - 1)
        beta = (i // (2 * half)) * (2 * half) + half - 1
        upper = (i % (2 * half)) >= half
        mats.append((upper & (r > beta) & (r <= i)) | ((~upper) & (r > i) & (r <= beta)))
    fwd = [m.astype(np.float32) for m in mats]
    bwd = [m[::-1, ::-1] for m in fwd]
    stack = np.stack([np.concatenate(fwd, 0), np.concatenate(bwd, 0)])
    x = i ^ r
    level = np.where(r > i, -1, np.where(r == i, 0, np.floor(np.log2(np.maximum(x, 1))) + 1)).astype(np.int32)
    levels = np.stack([level, level[::-1, ::-1]])
    return stack, levels


def _hgrn_kernel(proj_hbm, lbl_ref, nw_ref, m_ref, lv_ref, o_ref, qbuf, zfbuf, zbbuf, vbuf, gbuf, sems,
                 of_scr, ob_scr, st_scr, *, layer, ctx_chunks, n_chunks):
    c = CHUNK
    q_ref, zf_ref, zb_ref, v_ref, g_ref = _head_slabs(
        [proj_hbm] * 5, [j * HG_HEADS for j in range(5)], [qbuf, zfbuf, zbbuf, vbuf, gbuf], sems, HG_HEADS)
    z = lbl_ref[...]
    e = jnp.exp(z - jnp.max(z, axis=0, keepdims=True))
    sm = e / jnp.sum(e, axis=0, keepdims=True)
    lb = jnp.sum(sm[1:layer + 1], axis=0, keepdims=True)
    row = lax.broadcasted_iota(jnp.int32, (c, LANES), 0)
    upper = [[((row if d == 0 else c - 1 - row) & (1 << (m - 1))) != 0 for m in range(1, HG_MXU_LEVELS + 1)]
             for d in range(2)]

    def level_operand(q, k, a, half, d):
        pieces = []
        for base in range(0, c, 2 * half):
            lo, hi = slice(base, base + half), slice(base + half, base + 2 * half)
            if d == 0:
                ref = a[base + half - 1:base + half]
                pieces += [k[lo] * jnp.exp2(ref - a[lo]), q[hi] * jnp.exp2(a[hi] - ref)]
            else:
                ref = a[base + half:base + half + 1]
                pieces += [q[lo] * jnp.exp2(a[lo] - ref), k[hi] * jnp.exp2(ref - a[hi])]
        return jnp.concatenate(pieces, axis=0).astype(BF16)

    def chunk(n, d, z_ref):
        rows = pl.ds(pl.multiple_of(n * c, c), c)
        q = q_ref[rows, :]
        v = v_ref[rows, :]
        f = lb + (1.0 - lb) * jax.nn.sigmoid(z_ref[rows, :])
        g = jnp.log2(f)
        k = 1.0 - f
        g_hi = g.astype(BF16)
        g_lo = (g - g_hi.astype(F32)).astype(BF16)
        res = _dot(m_ref[d], jnp.concatenate([g_hi, g_lo], axis=1))
        ex = res[:, :c] + res[:, c:]
        a = ex[0:c]
        st = st_scr[d]
        o = _dot_nt((q * jnp.exp2(a)).astype(BF16), st.astype(BF16))
        lv = lv_ref[d]
        sc = jnp.where(lv == 0, _dot_nt(q.astype(BF16), k.astype(BF16)), 0.0)
        for m in range(1, HG_LEVELS + 1):
            if m <= HG_MXU_LEVELS:
                zz = (jnp.where(upper[d][m - 1], q, k) * jnp.exp2(ex[m * c:(m + 1) * c])).astype(BF16)
            else:
                zz = level_operand(q, k, a, 1 << (m - 1), d)
            sc = jnp.where(lv == m, _dot_nt(zz, zz), sc)
        vb = v.astype(BF16)
        o = o + _dot(sc.astype(BF16), vb)
        a_last = a[c - 1:c] if d == 0 else a[0:1]
        kd = (k * jnp.exp2(a_last - a)).astype(BF16)
        st_scr[d] = st * jnp.exp2(a_last) + _dot_tn(vb, kd)
        return rows, o

    st_scr[...] = jnp.zeros_like(st_scr)

    def scan(t, carry):
        rows, o = chunk(t, 0, zf_ref)
        of_scr[rows, :] = o
        rows, o = chunk(_backward_chunk(t, ctx_chunks, n_chunks), 1, zb_ref)
        ob_scr[rows, :] = o
        return carry

    lax.fori_loop(0, n_chunks, scan, 0, unroll=2)

    def readout(n, carry):
        rows = pl.ds(pl.multiple_of(n * c, c), c)
        o = of_scr[rows, :] + ob_scr[rows, :]
        y = o * lax.rsqrt(jnp.mean(o * o, axis=-1, keepdims=True) + EPS) * nw_ref[...]
        o_ref[rows, :] = (y * _silu(g_ref[rows, :])).astype(BF16)
        return carry

    lax.fori_loop(0, n_chunks, readout, 0)


def _hgrn2(proj, hg_lb_logits, hg_norm_w, layer, dims):
    batch, n_ctx, seq = dims
    tb = n_ctx + seq
    depth = hg_lb_logits.shape[0]
    stack, levels = _hgrn_constants()
    slab = pltpu.VMEM((2, tb, LANES), F32)
    n_mats = 1 + HG_MXU_LEVELS
    return pl.pallas_call(
        functools.partial(_hgrn_kernel, layer=layer, ctx_chunks=n_ctx // CHUNK, n_chunks=tb // CHUNK),
        out_shape=jax.ShapeDtypeStruct((HG_HEADS, batch * tb, LANES), BF16),
        grid=(batch, HG_HEADS),
        in_specs=[pl.BlockSpec(memory_space=pl.ANY),
                  pl.BlockSpec((depth, LANES), lambda b, h: (0, h)),
                  pl.BlockSpec((1, LANES), lambda b, h: (0, 0)),
                  pl.BlockSpec((2, n_mats * CHUNK, CHUNK), lambda b, h: (0, 0, 0)),
                  pl.BlockSpec((2, CHUNK, CHUNK), lambda b, h: (0, 0, 0))],
        out_specs=_head_spec(tb, lambda b, h: (h, b, 0)),
        scratch_shapes=[slab, slab, slab, slab, slab, pltpu.SemaphoreType.DMA((5, 2)),
                        pltpu.VMEM((tb, LANES), F32), pltpu.VMEM((tb, LANES), F32),
                        pltpu.VMEM((2, LANES, LANES), F32)],
        compiler_params=_cparams(2),
    )(proj, hg_lb_logits, hg_norm_w.reshape(1, LANES),
      jnp.asarray(stack, BF16), jnp.asarray(levels, jnp.int32))


SUBLANES = 8


def _store_token_tiles(ref, x):
    rows, tiles = x.shape[0], x.shape[1] // LANES
    for s in range(tiles):
        ref[pl.ds(s, rows, stride=tiles), :] = x[:, s * LANES:(s + 1) * LANES]


def _load_token_tiles(ref, tiles):
    rows = ref.shape[0] // tiles
    return jnp.concatenate([ref[pl.ds(s, rows, stride=tiles), :] for s in range(tiles)], axis=1)


def _route(logits, tri, counts):
    lane = lax.broadcasted_iota(jnp.int32, logits.shape, 1).astype(F32)
    big = float(LANES)
    first = lambda hit: jnp.min(jnp.where(hit, lane, big), axis=-1, keepdims=True)
    is_g = lane < MOE_GROUPS
    gl = jnp.where(is_g, logits, NEG_INF)
    gmax = jnp.max(gl, axis=-1, keepdims=True)
    gsum = jnp.sum(jnp.where(is_g, jnp.exp(gl - gmax), 0.0), axis=-1, keepdims=True)
    g_p = 1.0 / gsum
    lo = MOE_GROUPS + first(gl == gmax) * MOE_EPG
    el = jnp.where((lane >= lo) & (lane < lo + MOE_EPG), logits, NEG_INF)
    e1v = jnp.max(el, axis=-1, keepdims=True)
    e1 = first(el == e1v)
    el2 = jnp.where(lane == e1, NEG_INF, el)
    e2v = jnp.max(el2, axis=-1, keepdims=True)
    e2 = first(el2 == e2v)
    t = jnp.exp(e2v - e1v)
    w1 = 1.0 / (1.0 + t)
    w2 = t * w1
    id1 = e1 - MOE_GROUPS
    id2 = e2 - MOE_GROUPS
    onehot = jnp.where((lane == id1) | (lane == id2), 1.0, 0.0)
    before = counts + _dot(tri, onehot.astype(BF16))
    rank1 = jnp.sum(jnp.where(lane == id1, before, 0.0), axis=-1, keepdims=True)
    rank2 = jnp.sum(jnp.where(lane == id2, before, 0.0), axis=-1, keepdims=True)
    slab = jnp.zeros_like(logits)
    for col, val in enumerate((id1, id2, g_p * w1, g_p * w2, rank1, rank2)):
        slab = jnp.where(lane == col, val, slab)
    return slab, counts + jnp.sum(onehot, axis=0, keepdims=True)


def _out_route_kernel(*refs, n_parts):
    a_refs = refs[:n_parts]
    (w_ref, x_ref, gate_ref, nw_ref, shift_ref, scale_ref, wr_ref, br_ref, tri_ref,
     xo_ref, h2_ref, route_ref, cnt_ref) = refs[n_parts:]

    @pl.when(pl.program_id(0) == 0)
    def _():
        cnt_ref[...] = jnp.zeros_like(cnt_ref)

    k0 = 0
    y = None
    for a_ref in a_refs:
        a = jnp.concatenate([a_ref[h] for h in range(a_ref.shape[0])], axis=1)
        kk = a.shape[1]
        t = _dot(a, w_ref[k0:k0 + kk, :])
        y = t if y is None else y + t
        k0 += kk
    x = x_ref[...] + gate_ref[0] * y
    xo_ref[...] = x
    h2 = _norm_mod(x, nw_ref[...], scale_ref[0], shift_ref[0])
    _store_token_tiles(h2_ref, h2)
    h_hi = h2.astype(BF16)
    h_lo = (h2 - h_hi.astype(F32)).astype(BF16)
    both = _dot(h_hi, wr_ref[...])
    logits = both[:, :LANES] + both[:, LANES:] + _dot(h_lo, wr_ref[:, :LANES]) + br_ref[...]
    slab, counts = _route(logits, tri_ref[...], cnt_ref[0:1, :])
    route_ref[...] = slab
    cnt_ref[...] = jnp.broadcast_to(counts, cnt_ref.shape)


def _out_projection_route(parts, w_bf, xs, nw, mods, wr, br, dims):
    nt, d = xs.shape
    batch, n_ctx, seq = dims
    tpb = (n_ctx + seq) // ROW_TILE
    ctx_tiles = n_ctx // ROW_TILE
    row = lambda width: pl.BlockSpec((ROW_TILE, width), lambda i: (i, 0))
    full = lambda shape: pl.BlockSpec(shape, lambda i: (0,) * len(shape))
    mod = lambda chunk_idx: _mod_spec(chunk_idx, d, tpb, ctx_tiles, batch)
    tri = jnp.asarray(np.tril(np.ones((ROW_TILE, ROW_TILE), np.float32), -1), BF16)
    return pl.pallas_call(
        functools.partial(_out_route_kernel, n_parts=len(parts)),
        out_shape=(jax.ShapeDtypeStruct((nt, d), F32), jax.ShapeDtypeStruct((nt * d // LANES, LANES), F32),
                   jax.ShapeDtypeStruct((nt, LANES), F32), jax.ShapeDtypeStruct((8, LANES), F32)),
        grid=(nt // ROW_TILE,),
        in_specs=[pl.BlockSpec((p.shape[0], ROW_TILE, LANES), lambda i: (0, i, 0)) for p in parts] + [
            full(w_bf.shape), row(d), mod(2), full((1, d)), mod(3), mod(4), full(wr.shape), full(br.shape),
            full(tri.shape)],
        out_specs=(row(d), pl.BlockSpec((ROW_TILE * d // LANES, LANES), lambda i: (i, 0)), row(LANES),
                   full((8, LANES))),
        compiler_params=_cparams(1),
    )(*parts, w_bf, xs, mods, nw.reshape(1, d), mods, mods, wr, br, tri)


SCATTER_DMA_PRIORITY = 1
MOE_BUFFERS = 4


def _dispatch_plan(route, counts_f, n_tok, tiles):
    n_assign = n_tok * MOE_TOP_K
    flat_e = route[:, 0:MOE_TOP_K].astype(jnp.int32).reshape(-1)
    rank = route[:, 4:4 + MOE_TOP_K].astype(jnp.int32).reshape(-1)
    counts = counts_f[0, :MOE_EXPERTS].astype(jnp.int32)
    padded = (counts + MOE_BLOCK - 1) // MOE_BLOCK * MOE_BLOCK
    pad_end = jnp.cumsum(padded)
    pad_start = pad_end - padded
    dest = pad_start[flat_e] + rank
    n_blocks = -(-n_assign // MOE_BLOCK) + MOE_EXPERTS
    n_slots = n_blocks * MOE_BLOCK
    block_start = jnp.arange(n_blocks, dtype=jnp.int32) * MOE_BLOCK
    block_expert = jnp.minimum(jnp.sum((pad_end[None, :] <= block_start[:, None]).astype(jnp.int32), axis=1),
                               MOE_EXPERTS - 1)
    slot_assign = jnp.full((n_slots,), -1, jnp.int32).at[dest].set(jnp.arange(n_assign, dtype=jnp.int32))
    lead, trail = MOE_BLOCK, (MOE_BUFFERS - 1) * MOE_BLOCK
    slot = jnp.arange(-lead, n_slots + trail, dtype=jnp.int32)
    spare = n_assign + ((slot // MOE_BLOCK) % MOE_BUFFERS) * MOE_BLOCK + slot % MOE_BLOCK
    assign = jnp.concatenate([jnp.full((lead,), -1, jnp.int32), slot_assign, jnp.full((trail,), -1, jnp.int32)])
    src = jnp.where(assign >= 0, assign // MOE_TOP_K, 0) * tiles
    dst = jnp.where(assign >= 0, (assign % MOE_TOP_K) * n_tok + assign // MOE_TOP_K, spare) * tiles
    n_used = (pad_end[-1:] // MOE_BLOCK).astype(jnp.int32)
    return block_expert, n_used, src[lead:], dst[:n_slots + lead], n_blocks


def _expert_kernel(be_ref, nused_ref, src_ref, dst_ref, h_hbm, w1_ref, w3_ref, w2_ref, out_hbm,
                   xbuf, ybuf, w1b, w3b, w2b, gsem, ssem):
    i = pl.program_id(0)
    n_used = nused_ref[0]
    nb = MOE_BUFFERS
    cur = i % nb
    buffer_of = lambda blk: (blk + nb) % nb

    tiles = xbuf.shape[1] // MOE_BLOCK

    def token(idx):
        return pl.ds(pl.multiple_of(idx, tiles), tiles)

    def start_gather(blk, buf):
        for r in range(MOE_BLOCK):
            pltpu.make_async_copy(h_hbm.at[token(src_ref[blk * MOE_BLOCK + r])],
                                  xbuf.at[buf, pl.ds(r * tiles, tiles)], gsem.at[buf]).start()

    def start_scatter(blk, buf):
        for r in range(MOE_BLOCK):
            pltpu.make_async_copy(ybuf.at[buf, pl.ds(r * tiles, tiles)],
                                  out_hbm.at[token(dst_ref[(blk + 1) * MOE_BLOCK + r])], ssem.at[buf]).start(
                                      priority=SCATTER_DMA_PRIORITY)

    def wait_gather(buf):
        pltpu.make_async_copy(h_hbm.at[pl.ds(0, MOE_BLOCK * tiles)], xbuf.at[buf], gsem.at[buf]).wait()

    def wait_scatter(buf):
        pltpu.make_async_copy(ybuf.at[buf], out_hbm.at[pl.ds(0, MOE_BLOCK * tiles)], ssem.at[buf]).wait()

    @pl.when(i < n_used)
    def _():
        @pl.when(i == 0)
        def _():
            ybuf[...] = jnp.zeros_like(ybuf)
            block_rows = MOE_BLOCK * tiles
            spare0 = out_hbm.shape[0] - nb * block_rows
            for b in range(nb):
                spare = pltpu.make_async_copy(ybuf.at[b], out_hbm.at[pl.ds(spare0 + b * block_rows, block_rows)],
                                              ssem.at[b])
                spare.start()
                spare.wait()
            for blk in range(nb - 1):
                start_gather(blk, blk)

        @pl.when((i == 0) | (be_ref[i] != be_ref[jnp.maximum(i - 1, 0)]))
        def _():
            w1b[...] = w1_ref[0, 0].astype(BF16)
            w3b[...] = w3_ref[0, 0].astype(BF16)
            w2b[...] = w2_ref[0, 0].astype(BF16)

        wait_gather(cur)

        @pl.when(i >= nb - 1)
        def _():
            wait_scatter(cur)

        xb = _load_token_tiles(xbuf.at[cur], tiles).astype(BF16)
        start_gather(i + nb - 1, buffer_of(i - 1))
        start_scatter(i - 1, buffer_of(i - 1))
        act = _silu(_dot(xb, w1b[...])) * _dot(xb, w3b[...])
        _store_token_tiles(ybuf.at[cur], _dot(act.astype(BF16), w2b[...]))

        @pl.when(i == n_used - 1)
        def _():
            start_scatter(i, cur)
            for back in range(nb):
                @pl.when(i - back >= -1)
                def _():
                    wait_scatter(buffer_of(i - back))
            for ahead in range(1, nb):
                wait_gather(buffer_of(i + ahead))


def _experts(h2, plan, w1, w3, w2, layer):
    block_expert, n_used, slot_src, slot_dst, n_blocks = plan
    d, ff = w1.shape[2], w1.shape[3]
    tiles = d // LANES
    nt = h2.shape[0] // tiles
    wspec = lambda shape: pl.BlockSpec((1, 1) + shape, lambda i, be, nu, sr, ds: (layer, be[i], 0, 0))
    row_buffers = pltpu.VMEM((MOE_BUFFERS, MOE_BLOCK * tiles, LANES), F32)
    grid_spec = pltpu.PrefetchScalarGridSpec(
        num_scalar_prefetch=4,
        grid=(n_blocks,),
        in_specs=[pl.BlockSpec(memory_space=pl.ANY), wspec((d, ff)), wspec((d, ff)), wspec((ff, d))],
        out_specs=pl.BlockSpec(memory_space=pl.ANY),
        scratch_shapes=[row_buffers, row_buffers,
                        pltpu.VMEM((d, ff), BF16), pltpu.VMEM((d, ff), BF16), pltpu.VMEM((ff, d), BF16),
                        pltpu.SemaphoreType.DMA((MOE_BUFFERS,)), pltpu.SemaphoreType.DMA((MOE_BUFFERS,))])
    return pl.pallas_call(
        _expert_kernel,
        out_shape=jax.ShapeDtypeStruct(((nt * MOE_TOP_K + MOE_BUFFERS * MOE_BLOCK) * tiles, LANES), F32),
        grid_spec=grid_spec,
        compiler_params=_cparams(1, has_side_effects=True, disable_bounds_checks=True),
    )(block_expert, n_used, slot_src, slot_dst, h2, w1, w3, w2)


def _combine_kernel(x_ref, y0_ref, y1_ref, route_ref, gate_ref, *rest, final):
    r = route_ref[...]
    tiles = x_ref.shape[1] // LANES
    y = r[:, 2:3] * _load_token_tiles(y0_ref, tiles) + r[:, 3:4] * _load_token_tiles(y1_ref, tiles)
    x = x_ref[...] + gate_ref[0] * y
    if final:
        fnw_ref, o_ref = rest
        x = x * lax.rsqrt(jnp.mean(x * x, axis=-1, keepdims=True) + EPS) * fnw_ref[...]
    else:
        (o_ref,) = rest
    o_ref[...] = x


def _combine(xs, y2, route, mods, dims, final_norm_w=None):
    nt, d = xs.shape
    batch, n_ctx, seq = dims
    tpb = (n_ctx + seq) // ROW_TILE
    ctx_tiles = n_ctx // ROW_TILE
    row = lambda width: pl.BlockSpec((ROW_TILE, width), lambda i: (i, 0))
    final = final_norm_w is not None
    first = pl.BlockSpec((ROW_TILE * d // LANES, LANES), lambda i: (i, 0))
    second = pl.BlockSpec((ROW_TILE * d // LANES, LANES), lambda i: (i + nt // ROW_TILE, 0))
    in_specs = [row(d), first, second, row(LANES), _mod_spec(5, d, tpb, ctx_tiles, batch)]
    args = [xs, y2, y2, route, mods]
    out_rows, out_spec = nt, row(d)
    if final:
        in_specs.append(pl.BlockSpec((1, d), lambda i: (0, 0)))
        args.append(final_norm_w.reshape(1, d))
        lat_tiles = seq // ROW_TILE
        out_rows = batch * seq
        out_spec = pl.BlockSpec(
            (ROW_TILE, d), lambda i: ((i // tpb) * lat_tiles + jnp.maximum(i % tpb - ctx_tiles, 0), 0))
    return pl.pallas_call(
        functools.partial(_combine_kernel, final=final),
        out_shape=jax.ShapeDtypeStruct((out_rows, d), F32),
        grid=(nt // ROW_TILE,),
        in_specs=in_specs,
        out_specs=out_spec,
        compiler_params=_cparams(1),
    )(*args)


def _even_col_ops():
    rh = RET_HEADS
    ops = [(True, 1.0, 1.0)] * rh + [(True, float(HEAD_DIM) ** -0.5, 1.0)] * rh + [(False, 1.0, 1.0)] * (2 * rh)
    ops += [(True, 1.0, float(HEAD_DIM) ** -0.5 * LOG2_E)] * ATT_Q_HEADS + [(True, 1.0, 1.0)] * ATT_KV_HEADS
    ops += [(False, 1.0, 1.0)] * ATT_KV_HEADS
    return tuple(ops)


def kernel(x, c, ctx, c_ctx, ada_w, ada_b, norm_w, final_norm_w, ev_w_in, ev_w_out, ret_decay_raw, att_sink,
           od_w_in, od_w_out, hg_lb_logits, hg_norm_w, moe_wg, moe_bg, moe_we, moe_be, moe_w1, moe_w3, moe_w2):
    batch, seq, d = x.shape
    n_ctx = ctx.shape[1]
    depth = ada_w.shape[0]
    assert batch + 1 <= MOD_ROWS and n_ctx % ROW_TILE == 0 and seq % ROW_TILE == 0
    assert seq % GRID_W == 0 and seq >= 3 * CHUNK
    dims = (batch, n_ctx, seq)
    tb = n_ctx + seq
    nt = batch * tb

    xs = jnp.concatenate([ctx, x], axis=1).reshape(nt, d)
    cvec = jnp.concatenate([c, c_ctx[None, :], jnp.zeros((MOD_ROWS - batch - 1, d), F32)], axis=0)
    mods_all = _ada_modulation(cvec, ada_w, ada_b)
    rope = _rope_tables(n_ctx, seq)
    even_ops = _even_col_ops()

    for l in range(depth):
        p = l // 2
        mods = mods_all[l].reshape(MOD_ROWS, 1, 6 * d)
        if l % 2 == 0:
            proj = _in_projection(xs, norm_w[l, 0], mods, ev_w_in[p].astype(BF16), dims, even_ops, rope)
            ret = _retention(proj, ret_decay_raw[p], dims)
            att = _attention(proj, att_sink[p], dims, 4 * RET_HEADS)
            parts, w_out = [ret, att], ev_w_out[p]
        else:
            proj = _in_projection(xs, norm_w[l, 0], mods, od_w_in[p].astype(BF16), dims)
            parts, w_out = [_hgrn2(proj, hg_lb_logits, hg_norm_w[p], l, dims)], od_w_out[p]
        n_logit = MOE_GROUPS + MOE_EXPERTS
        wr = jnp.concatenate([moe_wg[l], jnp.moveaxis(moe_we[l], 0, 1).reshape(d, MOE_EXPERTS),
                              jnp.zeros((d, LANES - n_logit), F32)], axis=1)
        wr_hi = wr.astype(BF16)
        wr = jnp.concatenate([wr_hi, (wr - wr_hi.astype(F32)).astype(BF16)], axis=1)
        br = jnp.concatenate([moe_bg[l], moe_be[l].reshape(-1), jnp.zeros((LANES - n_logit,), F32)])[None, :]
        xs, h2, route, counts = _out_projection_route(
            parts, w_out.astype(BF16), xs, norm_w[l, 1], mods, wr, br, dims)
        plan = _dispatch_plan(route, counts, nt, d // LANES)
        y2 = _experts(h2, plan, moe_w1, moe_w3, moe_w2, l)
        xs = _combine(xs, y2, route, mods, dims, final_norm_w if l == depth - 1 else None)

    return xs.reshape(batch, seq, d)
```

```python
import functools

import numpy as np
import jax
import jax.numpy as jnp
from jax import lax
from jax.experimental import pallas as pl
from jax.experimental.pallas import tpu as pltpu

F32 = jnp.float32
BF16 = jnp.bfloat16
HIGHEST = lax.Precision.HIGHEST

EPS = 1e-6
NEG_INF = -1e30
LOG2_E = 1.4426950408889634
LANES = 128
GRID_W = 64
ROPE_THETA = 10000.0
HEAD_DIM = 128
RET_HEADS = 4
ATT_Q_HEADS = 4
ATT_KV_HEADS = 2
ATT_GROUP = ATT_Q_HEADS // ATT_KV_HEADS
WINDOW = 128
HG_HEADS = 8
MOE_GROUPS = 4
MOE_EPG = 8
MOE_EXPERTS = MOE_GROUPS * MOE_EPG
MOE_TOP_K = 2
MOE_BLOCK = 128
CHUNK = 128
ROW_TILE = 256
MOD_ROWS = 8
VMEM_LIMIT = 56 * 1024 * 1024


def _cparams(n_axes, **kw):
    return pltpu.CompilerParams(dimension_semantics=("arbitrary",) * n_axes,
                                vmem_limit_bytes=VMEM_LIMIT, **kw)


def _dot(a, b):
    return jnp.dot(a, b, preferred_element_type=F32)


def _dot_nt(a, b):
    return lax.dot_general(a, b, (((1,), (1,)), ((), ())), preferred_element_type=F32)


def _dot_tn(a, b):
    return lax.dot_general(a, b, (((0,), (0,)), ((), ())), preferred_element_type=F32)


def _silu(x):
    return x * jax.nn.sigmoid(x)


def _norm_mod(x, nw, scale, shift):
    ms = jnp.mean(x * x, axis=-1, keepdims=True)
    y = x * lax.rsqrt(ms + EPS) * nw
    return y * (1.0 + scale) + shift


def _ada_kernel(c_ref, w_ref, b_ref, o_ref):
    s = _silu(c_ref[...])
    o_ref[0] = jnp.dot(s, w_ref[0], precision=HIGHEST, preferred_element_type=F32) + b_ref[0]


def _ada_modulation(cvec, ada_w, ada_b):
    depth, d, n6 = ada_w.shape
    tn = 1536
    return pl.pallas_call(
        _ada_kernel,
        out_shape=jax.ShapeDtypeStruct((depth, MOD_ROWS, n6), F32),
        grid=(depth, n6 // tn),
        in_specs=[pl.BlockSpec((MOD_ROWS, d), lambda l, j: (0, 0)),
                  pl.BlockSpec((1, d, tn), lambda l, j: (l, 0, j)),
                  pl.BlockSpec((1, 1, tn), lambda l, j: (l, 0, j))],
        out_specs=pl.BlockSpec((1, MOD_ROWS, tn), lambda l, j: (l, 0, j)),
        compiler_params=_cparams(2),
    )(cvec, ada_w, ada_b.reshape(depth, 1, n6))


def _swap_halves(x):
    lane = lax.broadcasted_iota(jnp.int32, x.shape, 1)
    return jnp.where((lane % 64) < 32, pltpu.roll(x, 96, 1), pltpu.roll(x, 32, 1))


def _proj_kernel(x_ref, nw_ref, shift_ref, scale_ref, w_ref, *rest, col_ops, chunk):
    if col_ops is None:
        (o_ref,) = rest
    else:
        cos_ref, sin_ref, o_ref = rest
    h = _norm_mod(x_ref[...], nw_ref[...], scale_ref[0], shift_ref[0]).astype(BF16)
    nout = o_ref.shape[0] * LANES
    for c0 in range(0, nout, chunk):
        acc = _dot(h, w_ref[:, c0:c0 + chunk])
        for hd in range(chunk // LANES):
            col = c0 + hd * LANES
            a = acc[:, hd * LANES:(hd + 1) * LANES]
            if col_ops is not None:
                rope, pre, post = col_ops[col // LANES]
                if pre != 1.0:
                    a = a * pre
                if rope:
                    a = a * cos_ref[...] + _swap_halves(a) * sin_ref[...]
                if post != 1.0:
                    a = a * post
            o_ref[col // LANES] = a


def _tile_mod_row(i, tiles_per_batch, ctx_tiles, batch):
    return jnp.where(i % tiles_per_batch < ctx_tiles, batch, i // tiles_per_batch)


def _mod_spec(chunk_idx, d, tiles_per_batch, ctx_tiles, batch):
    return pl.BlockSpec(
        (1, 1, d), lambda i: (_tile_mod_row(i, tiles_per_batch, ctx_tiles, batch), 0, chunk_idx))


def _in_projection(xs, nw, mods, w_bf, dims, col_ops=None, rope=None):
    nt, d = xs.shape
    nout = w_bf.shape[1]
    batch, n_ctx, seq = dims
    tpb = (n_ctx + seq) // ROW_TILE
    ctx_tiles = n_ctx // ROW_TILE
    in_specs = [pl.BlockSpec((ROW_TILE, d), lambda i: (i, 0)),
                pl.BlockSpec((1, d), lambda i: (0, 0)),
                _mod_spec(0, d, tpb, ctx_tiles, batch),
                _mod_spec(1, d, tpb, ctx_tiles, batch),
                pl.BlockSpec((d, nout), lambda i: (0, 0))]
    args = [xs, nw.reshape(1, d), mods, mods, w_bf]
    if col_ops is not None:
        rope_spec = pl.BlockSpec((ROW_TILE, LANES), lambda i: (i % tpb, 0))
        in_specs += [rope_spec, rope_spec]
        args += [rope[0], rope[1]]
    return pl.pallas_call(
        functools.partial(_proj_kernel, col_ops=col_ops, chunk=512),
        out_shape=jax.ShapeDtypeStruct((nout // LANES, nt, LANES), F32),
        grid=(nt // ROW_TILE,),
        in_specs=in_specs,
        out_specs=pl.BlockSpec((nout // LANES, ROW_TILE, LANES), lambda i: (0, i, 0)),
        compiler_params=_cparams(1),
    )(*args)


def _rope_tables(n_ctx, seq):
    n_rows = seq // GRID_W
    row = jnp.repeat(jnp.arange(n_rows, dtype=F32), GRID_W)
    col = jnp.tile(jnp.arange(GRID_W, dtype=F32), n_rows)
    axis_dim = HEAD_DIM // 2
    inv_freq = ROPE_THETA ** (-jnp.arange(0, axis_dim, 2, dtype=F32) / axis_dim)
    ang_r = row[:, None] * inv_freq[None, :]
    ang_c = col[:, None] * inv_freq[None, :]
    cos = jnp.concatenate([jnp.cos(ang_r), jnp.cos(ang_r), jnp.cos(ang_c), jnp.cos(ang_c)], axis=-1)
    sin = jnp.concatenate([-jnp.sin(ang_r), jnp.sin(ang_r), -jnp.sin(ang_c), jnp.sin(ang_c)], axis=-1)
    cos = jnp.concatenate([jnp.ones((n_ctx, HEAD_DIM), F32), cos], axis=0)
    sin = jnp.concatenate([jnp.zeros((n_ctx, HEAD_DIM), F32), sin], axis=0)
    return cos, sin


def _head_spec(rows, index_map):
    return pl.BlockSpec((None, rows, LANES), index_map)


HEAD_DMA_ROWS = 256


def _head_slabs(hbm_refs, head_offsets, bufs, sems, n_heads):
    step = pl.program_id(0) * n_heads + pl.program_id(1)
    n_steps = pl.num_programs(0) * n_heads
    rows = bufs[0].shape[1]
    piece = HEAD_DMA_ROWS
    assert rows % piece == 0

    def start(at_step, slot):
        b, h = at_step // n_heads, at_step % n_heads
        for j, (ref, off) in enumerate(zip(hbm_refs, head_offsets)):
            for p in range(rows // piece):
                pltpu.make_async_copy(ref.at[off + h, pl.ds(b * rows + p * piece, piece)],
                                      bufs[j].at[slot, pl.ds(p * piece, piece)], sems.at[j, slot]).start()

    slot = step % 2

    @pl.when(step == 0)
    def _():
        start(step, slot)

    @pl.when(step + 1 < n_steps)
    def _():
        start(step + 1, 1 - slot)

    for j, ref in enumerate(hbm_refs):
        pltpu.make_async_copy(ref.at[0, pl.ds(0, rows)], bufs[j].at[slot], sems.at[j, slot]).wait()
    return [buf.at[slot] for buf in bufs]


def _backward_chunk(t, ctx_chunks, n_chunks):
    return jnp.where(t < ctx_chunks, ctx_chunks - 1 - t, n_chunks - 1 - (t - ctx_chunks))


def _ret_kernel(raw_ref, proj_hbm, o_ref, qbuf, kbuf, vbuf, gbuf, sems, of_scr, ob_scr, s_scr, dec_scr, qd_scr,
                kd_scr, *, ctx_chunks, n_chunks):
    h = pl.program_id(1)
    c = CHUNK
    q_ref, k_ref, v_ref, g_ref = _head_slabs(
        [proj_hbm] * 4, [0, RET_HEADS, 2 * RET_HEADS, 3 * RET_HEADS], [qbuf, kbuf, vbuf, gbuf], sems, RET_HEADS)
    ii = lax.broadcasted_iota(jnp.int32, (c, c), 0).astype(F32)
    jj = lax.broadcasted_iota(jnp.int32, (c, c), 1).astype(F32)
    lg = []
    for d in range(2):
        lgd = -jnp.exp(jnp.full((c, c), raw_ref[d, h], F32))
        lg.append(lgd)
        rel = (ii - jj) if d == 0 else (jj - ii)
        dec_scr[d] = jnp.where(rel >= 0, jnp.exp(jnp.maximum(rel, 0.0) * lgd), 0.0)
        qd_scr[d] = jnp.exp(((ii + 1.0) if d == 0 else (c - ii)) * lgd)
        kd_scr[d] = jnp.exp(((c - 1.0 - ii) if d == 0 else ii) * lgd)

    def chunk(n, d):
        rows = pl.ds(pl.multiple_of(n * c, c), c)
        q = q_ref[rows, :]
        k = k_ref[rows, :]
        vb = v_ref[rows, :].astype(BF16)
        sc = _dot_nt(q.astype(BF16), k.astype(BF16)) * dec_scr[d]
        s = s_scr[d]
        o = _dot(sc.astype(BF16), vb) + _dot((q * qd_scr[d]).astype(BF16), s.astype(BF16))
        chunk_decay = jnp.exp(float(c) * lg[d][0:1, :])
        s_scr[d] = chunk_decay * s + _dot_tn((k * kd_scr[d]).astype(BF16), vb)
        return rows, o

    s_scr[...] = jnp.zeros_like(s_scr)

    def scan(t, carry):
        rows, o = chunk(t, 0)
        of_scr[rows, :] = o
        rows, o = chunk(_backward_chunk(t, ctx_chunks, n_chunks), 1)
        ob_scr[rows, :] = o
        return carry

    lax.fori_loop(0, n_chunks, scan, 0, unroll=2)

    def readout(n, carry):
        rows = pl.ds(pl.multiple_of(n * c, c), c)
        o = of_scr[rows, :] + ob_scr[rows, :]
        y = o * lax.rsqrt(jnp.mean(o * o, axis=-1, keepdims=True) + EPS)
        o_ref[rows, :] = (_silu(g_ref[rows, :]) * y).astype(BF16)
        return carry

    lax.fori_loop(0, n_chunks, readout, 0)


def _retention(proj, ret_decay_raw, dims):
    batch, n_ctx, seq = dims
    tb = n_ctx + seq
    slab = pltpu.VMEM((2, tb, LANES), F32)
    return pl.pallas_call(
        functools.partial(_ret_kernel, ctx_chunks=n_ctx // CHUNK, n_chunks=tb // CHUNK),
        out_shape=jax.ShapeDtypeStruct((RET_HEADS, batch * tb, LANES), BF16),
        grid=(batch, RET_HEADS),
        in_specs=[pl.BlockSpec(memory_space=pltpu.SMEM), pl.BlockSpec(memory_space=pl.ANY)],
        out_specs=_head_spec(tb, lambda b, h: (h, b, 0)),
        scratch_shapes=[slab, slab, slab, slab, pltpu.SemaphoreType.DMA((4, 2)),
                        pltpu.VMEM((tb, LANES), F32), pltpu.VMEM((tb, LANES), F32),
                        pltpu.VMEM((2, LANES, LANES), F32),
                        pltpu.VMEM((2, CHUNK, CHUNK), F32), pltpu.VMEM((2, CHUNK, LANES), F32),
                        pltpu.VMEM((2, CHUNK, LANES), F32)],
        compiler_params=_cparams(2),
    )(ret_decay_raw, proj)


def _att_kernel(sink_ref, q0_ref, q1_ref, k_ref, v_ref, o_ref, kb_scr, vb_scr, bias_scr, *, n_ctx, seq):
    hk = pl.program_id(1)
    c = CHUNK
    win = 3 * c
    kb_scr[...] = k_ref[...].astype(BF16)
    vb_scr[...] = v_ref[...].astype(BF16)
    kc = kb_scr[0:n_ctx, :]
    vc = vb_scr[0:n_ctx, :]
    ii = lax.broadcasted_iota(jnp.int32, (c, win), 0)
    jj = lax.broadcasted_iota(jnp.int32, (c, win), 1)
    for off in range(3):
        bias_scr[off] = jnp.where(jnp.abs(ii - jj + off * c) <= WINDOW, 0.0, NEG_INF)

    def softmax_out(s_ctx, sink, s_loc=None, v_loc=None):
        m = jnp.maximum(jnp.max(s_ctx, axis=-1, keepdims=True), sink)
        if s_loc is not None:
            m = jnp.maximum(m, jnp.max(s_loc, axis=-1, keepdims=True))
        p_ctx = jnp.exp2(s_ctx - m)
        den = jnp.sum(p_ctx, axis=-1, keepdims=True) + jnp.exp2(sink - m)
        o = _dot(p_ctx.astype(BF16), vc)
        if s_loc is not None:
            p_loc = jnp.exp2(s_loc - m)
            den = den + jnp.sum(p_loc, axis=-1, keepdims=True)
            o = o + _dot(p_loc.astype(BF16), v_loc)
        return o / den

    heads = []
    for g, q_ref in enumerate((q0_ref, q1_ref)):
        sink = jnp.full((c, 1), sink_ref[hk * ATT_GROUP + g], F32) * LOG2_E
        heads.append((q_ref, sink, g))

    for q_ref, sink, g in heads:
        for cc in range(n_ctx // c):
            qb = q_ref[cc * c:(cc + 1) * c, :].astype(BF16)
            o_ref[g, cc * c:(cc + 1) * c, :] = softmax_out(_dot_nt(qb, kc), sink).astype(BF16)

    def block(n, carry):
        q_rows = pl.ds(pl.multiple_of(n_ctx + n * c, c), c)
        start = jnp.clip(n * c - c, 0, seq - win)
        k_rows = pl.ds(pl.multiple_of(n_ctx + start, c), win)
        k_loc = kb_scr[k_rows, :]
        v_loc = vb_scr[k_rows, :]
        bias = bias_scr[(n * c - start) // c]
        for q_ref, sink, g in heads:
            qb = q_ref[q_rows, :].astype(BF16)
            o = softmax_out(_dot_nt(qb, kc), sink, _dot_nt(qb, k_loc) + bias, v_loc)
            o_ref[g, q_rows, :] = o.astype(BF16)
        return carry

    lax.fori_loop(0, seq // c, block, 0)


def _attention(proj, att_sink, dims, col0):
    batch, n_ctx, seq = dims
    tb = n_ctx + seq
    qb, kb, vb = col0, col0 + ATT_Q_HEADS, col0 + ATT_Q_HEADS + ATT_KV_HEADS
    return pl.pallas_call(
        functools.partial(_att_kernel, n_ctx=n_ctx, seq=seq),
        out_shape=jax.ShapeDtypeStruct((ATT_Q_HEADS, batch * tb, LANES), BF16),
        grid=(batch, ATT_KV_HEADS),
        in_specs=[pl.BlockSpec(memory_space=pltpu.SMEM),
                  _head_spec(tb, lambda b, h: (qb + ATT_GROUP * h, b, 0)),
                  _head_spec(tb, lambda b, h: (qb + ATT_GROUP * h + 1, b, 0)),
                  _head_spec(tb, lambda b, h: (kb + h, b, 0)),
                  _head_spec(tb, lambda b, h: (vb + h, b, 0))],
        out_specs=pl.BlockSpec((ATT_GROUP, tb, LANES), lambda b, h: (h, b, 0)),
        scratch_shapes=[pltpu.VMEM((tb, LANES), BF16), pltpu.VMEM((tb, LANES), BF16),
                        pltpu.VMEM((3, CHUNK, 3 * CHUNK), F32)],
        compiler_params=_cparams(2),
    )(att_sink, proj, proj, proj, proj)


HG_LEVELS = 7
HG_MXU_LEVELS = 3


def _hgrn_constants():
    c = CHUNK
    i = np.arange(c)[:, None]
    r = np.arange(c)[None, :]
    mats = [r <= i]
    for m in range(1, HG_MXU_LEVELS + 1):
        half = 1 << (m - 1)
        beta = (i // (2 * half)) * (2 * half) + half - 1
        upper = (i % (2 * half)) >= half
        mats.append((upper & (r > beta) & (r <= i)) | ((~upper) & (r > i) & (r <= beta)))
    fwd = [m.astype(np.float32) for m in mats]
    bwd = [m[::-1, ::-1] for m in fwd]
    stack = np.stack([np.concatenate(fwd, 0), np.concatenate(bwd, 0)])
    x = i ^ r
    level = np.where(r > i, -1, np.where(r == i, 0, np.floor(np.log2(np.maximum(x, 1))) + 1)).astype(np.int32)
    levels = np.stack([level, level[::-1, ::-1]])
    return stack, levels


def _hgrn_kernel(proj_hbm, lbl_ref, nw_ref, m_ref, lv_ref, o_ref, qbuf, zfbuf, zbbuf, vbuf, gbuf, sems,
                 of_scr, ob_scr, st_scr, *, layer, ctx_chunks, n_chunks):
    c = CHUNK
    q_ref, zf_ref, zb_ref, v_ref, g_ref = _head_slabs(
        [proj_hbm] * 5, [j * HG_HEADS for j in range(5)], [qbuf, zfbuf, zbbuf, vbuf, gbuf], sems, HG_HEADS)
    z = lbl_ref[...]
    e = jnp.exp(z - jnp.max(z, axis=0, keepdims=True))
    sm = e / jnp.sum(e, axis=0, keepdims=True)
    lb = jnp.sum(sm[1:layer + 1], axis=0, keepdims=True)
    row = lax.broadcasted_iota(jnp.int32, (c, LANES), 0)
    upper = [[((row if d == 0 else c - 1 - row) & (1 << (m - 1))) != 0 for m in range(1, HG_MXU_LEVELS + 1)]
             for d in range(2)]

    def level_operand(q, k, a, half, d):
        pieces = []
        for base in range(0, c, 2 * half):
            lo, hi = slice(base, base + half), slice(base + half, base + 2 * half)
            if d == 0:
                ref = a[base + half - 1:base + half]
                pieces += [k[lo] * jnp.exp2(ref - a[lo]), q[hi] * jnp.exp2(a[hi] - ref)]
            else:
                ref = a[base + half:base + half + 1]
                pieces += [q[lo] * jnp.exp2(a[lo] - ref), k[hi] * jnp.exp2(ref - a[hi])]
        return jnp.concatenate(pieces, axis=0).astype(BF16)

    def chunk(n, d, z_ref):
        rows = pl.ds(pl.multiple_of(n * c, c), c)
        q = q_ref[rows, :]
        v = v_ref[rows, :]
        f = lb + (1.0 - lb) * jax.nn.sigmoid(z_ref[rows, :])
        g = jnp.log2(f)
        k = 1.0 - f
        g_hi = g.astype(BF16)
        g_lo = (g - g_hi.astype(F32)).astype(BF16)
        res = _dot(m_ref[d], jnp.concatenate([g_hi, g_lo], axis=1))
        ex = res[:, :c] + res[:, c:]
        a = ex[0:c]
        st = st_scr[d]
        o = _dot_nt((q * jnp.exp2(a)).astype(BF16), st.astype(BF16))
        lv = lv_ref[d]
        sc = jnp.where(lv == 0, _dot_nt(q.astype(BF16), k.astype(BF16)), 0.0)
        for m in range(1, HG_LEVELS + 1):
            if m <= HG_MXU_LEVELS:
                zz = (jnp.where(upper[d][m - 1], q, k) * jnp.exp2(ex[m * c:(m + 1) * c])).astype(BF16)
            else:
                zz = level_operand(q, k, a, 1 << (m - 1), d)
            sc = jnp.where(lv == m, _dot_nt(zz, zz), sc)
        vb = v.astype(BF16)
        o = o + _dot(sc.astype(BF16), vb)
        a_last = a[c - 1:c] if d == 0 else a[0:1]
        kd = (k * jnp.exp2(a_last - a)).astype(BF16)
        st_scr[d] = st * jnp.exp2(a_last) + _dot_tn(vb, kd)
        return rows, o

    st_scr[...] = jnp.zeros_like(st_scr)

    def scan(t, carry):
        rows, o = chunk(t, 0, zf_ref)
        of_scr[rows, :] = o
        rows, o = chunk(_backward_chunk(t, ctx_chunks, n_chunks), 1, zb_ref)
        ob_scr[rows, :] = o
        return carry

    lax.fori_loop(0, n_chunks, scan, 0, unroll=2)

    def readout(n, carry):
        rows = pl.ds(pl.multiple_of(n * c, c), c)
        o = of_scr[rows, :] + ob_scr[rows, :]
        y = o * lax.rsqrt(jnp.mean(o * o, axis=-1, keepdims=True) + EPS) * nw_ref[...]
        o_ref[rows, :] = (y * _silu(g_ref[rows, :])).astype(BF16)
        return carry

    lax.fori_loop(0, n_chunks, readout, 0)


def _hgrn2(proj, hg_lb_logits, hg_norm_w, layer, dims):
    batch, n_ctx, seq = dims
    tb = n_ctx + seq
    depth = hg_lb_logits.shape[0]
    stack, levels = _hgrn_constants()
    slab = pltpu.VMEM((2, tb, LANES), F32)
    n_mats = 1 + HG_MXU_LEVELS
    return pl.pallas_call(
        functools.partial(_hgrn_kernel, layer=layer, ctx_chunks=n_ctx // CHUNK, n_chunks=tb // CHUNK),
        out_shape=jax.ShapeDtypeStruct((HG_HEADS, batch * tb, LANES), BF16),
        grid=(batch, HG_HEADS),
        in_specs=[pl.BlockSpec(memory_space=pl.ANY),
                  pl.BlockSpec((depth, LANES), lambda b, h: (0, h)),
                  pl.BlockSpec((1, LANES), lambda b, h: (0, 0)),
                  pl.BlockSpec((2, n_mats * CHUNK, CHUNK), lambda b, h: (0, 0, 0)),
                  pl.BlockSpec((2, CHUNK, CHUNK), lambda b, h: (0, 0, 0))],
        out_specs=_head_spec(tb, lambda b, h: (h, b, 0)),
        scratch_shapes=[slab, slab, slab, slab, slab, pltpu.SemaphoreType.DMA((5, 2)),
                        pltpu.VMEM((tb, LANES), F32), pltpu.VMEM((tb, LANES), F32),
                        pltpu.VMEM((2, LANES, LANES), F32)],
        compiler_params=_cparams(2),
    )(proj, hg_lb_logits, hg_norm_w.reshape(1, LANES),
      jnp.asarray(stack, BF16), jnp.asarray(levels, jnp.int32))


SUBLANES = 8


def _store_token_tiles(ref, x):
    rows, tiles = x.shape[0], x.shape[1] // LANES
    for s in range(tiles):
        ref[pl.ds(s, rows, stride=tiles), :] = x[:, s * LANES:(s + 1) * LANES]


def _load_token_tiles(ref, tiles):
    rows = ref.shape[0] // tiles
    return jnp.concatenate([ref[pl.ds(s, rows, stride=tiles), :] for s in range(tiles)], axis=1)


def _route(logits, tri, counts):
    lane = lax.broadcasted_iota(jnp.int32, logits.shape, 1).astype(F32)
    big = float(LANES)
    first = lambda hit: jnp.min(jnp.where(hit, lane, big), axis=-1, keepdims=True)
    is_g = lane < MOE_GROUPS
    gl = jnp.where(is_g, logits, NEG_INF)
    gmax = jnp.max(gl, axis=-1, keepdims=True)
    gsum = jnp.sum(jnp.where(is_g, jnp.exp(gl - gmax), 0.0), axis=-1, keepdims=True)
    g_p = 1.0 / gsum
    lo = MOE_GROUPS + first(gl == gmax) * MOE_EPG
    el = jnp.where((lane >= lo) & (lane < lo + MOE_EPG), logits, NEG_INF)
    e1v = jnp.max(el, axis=-1, keepdims=True)
    e1 = first(el == e1v)
    el2 = jnp.where(lane == e1, NEG_INF, el)
    e2v = jnp.max(el2, axis=-1, keepdims=True)
    e2 = first(el2 == e2v)
    t = jnp.exp(e2v - e1v)
    w1 = 1.0 / (1.0 + t)
    w2 = t * w1
    id1 = e1 - MOE_GROUPS
    id2 = e2 - MOE_GROUPS
    onehot = jnp.where((lane == id1) | (lane == id2), 1.0, 0.0)
    before = counts + _dot(tri, onehot.astype(BF16))
    rank1 = jnp.sum(jnp.where(lane == id1, before, 0.0), axis=-1, keepdims=True)
    rank2 = jnp.sum(jnp.where(lane == id2, before, 0.0), axis=-1, keepdims=True)
    slab = jnp.zeros_like(logits)
    for col, val in enumerate((id1, id2, g_p * w1, g_p * w2, rank1, rank2)):
        slab = jnp.where(lane == col, val, slab)
    return slab, counts + jnp.sum(onehot, axis=0, keepdims=True)


def _out_route_kernel(*refs, n_parts):
    a_refs = refs[:n_parts]
    (w_ref, x_ref, gate_ref, nw_ref, shift_ref, scale_ref, wr_ref, br_ref, tri_ref,
     xo_ref, h2_ref, route_ref, cnt_ref) = refs[n_parts:]

    @pl.when(pl.program_id(0) == 0)
    def _():
        cnt_ref[...] = jnp.zeros_like(cnt_ref)

    k0 = 0
    y = None
    for a_ref in a_refs:
        a = jnp.concatenate([a_ref[h] for h in range(a_ref.shape[0])], axis=1)
        kk = a.shape[1]
        t = _dot(a, w_ref[k0:k0 + kk, :])
        y = t if y is None else y + t
        k0 += kk
    x = x_ref[...] + gate_ref[0] * y
    xo_ref[...] = x
    h2 = _norm_mod(x, nw_ref[...], scale_ref[0], shift_ref[0])
    _store_token_tiles(h2_ref, h2)
    h_hi = h2.astype(BF16)
    h_lo = (h2 - h_hi.astype(F32)).astype(BF16)
    both = _dot(h_hi, wr_ref[...])
    logits = both[:, :LANES] + both[:, LANES:] + _dot(h_lo, wr_ref[:, :LANES]) + br_ref[...]
    slab, counts = _route(logits, tri_ref[...], cnt_ref[0:1, :])
    route_ref[...] = slab
    cnt_ref[...] = jnp.broadcast_to(counts, cnt_ref.shape)


def _out_projection_route(parts, w_bf, xs, nw, mods, wr, br, dims):
    nt, d = xs.shape
    batch, n_ctx, seq = dims
    tpb = (n_ctx + seq) // ROW_TILE
    ctx_tiles = n_ctx // ROW_TILE
    row = lambda width: pl.BlockSpec((ROW_TILE, width), lambda i: (i, 0))
    full = lambda shape: pl.BlockSpec(shape, lambda i: (0,) * len(shape))
    mod = lambda chunk_idx: _mod_spec(chunk_idx, d, tpb, ctx_tiles, batch)
    tri = jnp.asarray(np.tril(np.ones((ROW_TILE, ROW_TILE), np.float32), -1), BF16)
    return pl.pallas_call(
        functools.partial(_out_route_kernel, n_parts=len(parts)),
        out_shape=(jax.ShapeDtypeStruct((nt, d), F32), jax.ShapeDtypeStruct((nt * d // LANES, LANES), F32),
                   jax.ShapeDtypeStruct((nt, LANES), F32), jax.ShapeDtypeStruct((8, LANES), F32)),
        grid=(nt // ROW_TILE,),
        in_specs=[pl.BlockSpec((p.shape[0], ROW_TILE, LANES), lambda i: (0, i, 0)) for p in parts] + [
            full(w_bf.shape), row(d), mod(2), full((1, d)), mod(3), mod(4), full(wr.shape), full(br.shape),
            full(tri.shape)],
        out_specs=(row(d), pl.BlockSpec((ROW_TILE * d // LANES, LANES), lambda i: (i, 0)), row(LANES),
                   full((8, LANES))),
        compiler_params=_cparams(1),
    )(*parts, w_bf, xs, mods, nw.reshape(1, d), mods, mods, wr, br, tri)


SCATTER_DMA_PRIORITY = 1
MOE_BUFFERS = 4


def _dispatch_plan(route, counts_f, n_tok, tiles):
    n_assign = n_tok * MOE_TOP_K
    flat_e = route[:, 0:MOE_TOP_K].astype(jnp.int32).reshape(-1)
    rank = route[:, 4:4 + MOE_TOP_K].astype(jnp.int32).reshape(-1)
    counts = counts_f[0, :MOE_EXPERTS].astype(jnp.int32)
    padded = (counts + MOE_BLOCK - 1) // MOE_BLOCK * MOE_BLOCK
    pad_end = jnp.cumsum(padded)
    pad_start = pad_end - padded
    dest = pad_start[flat_e] + rank
    n_blocks = -(-n_assign // MOE_BLOCK) + MOE_EXPERTS
    n_slots = n_blocks * MOE_BLOCK
    block_start = jnp.arange(n_blocks, dtype=jnp.int32) * MOE_BLOCK
    block_expert = jnp.minimum(jnp.sum((pad_end[None, :] <= block_start[:, None]).astype(jnp.int32), axis=1),
                               MOE_EXPERTS - 1)
    slot_assign = jnp.full((n_slots,), -1, jnp.int32).at[dest].set(
        jnp.arange(n_assign, dtype=jnp.int32), unique_indices=True)
    lead, trail = MOE_BLOCK, (MOE_BUFFERS - 1) * MOE_BLOCK
    slot = jnp.arange(-lead, n_slots + trail, dtype=jnp.int32)
    spare = n_assign + ((slot // MOE_BLOCK) % MOE_BUFFERS) * MOE_BLOCK + slot % MOE_BLOCK
    assign = jnp.concatenate([jnp.full((lead,), -1, jnp.int32), slot_assign, jnp.full((trail,), -1, jnp.int32)])
    src = jnp.where(assign >= 0, assign // MOE_TOP_K, 0) * tiles
    dst = jnp.where(assign >= 0, (assign % MOE_TOP_K) * n_tok + assign // MOE_TOP_K, spare) * tiles
    n_used = (pad_end[-1:] // MOE_BLOCK).astype(jnp.int32)
    return block_expert, n_used, src[lead:], dst[:n_slots + lead], n_blocks


def _expert_kernel(be_ref, nused_ref, src_ref, dst_ref, h_hbm, w1_ref, w3_ref, w2_ref, out_hbm,
                   xbuf, ybuf, w1b, w3b, w2b, gsem, ssem):
    i = pl.program_id(0)
    n_used = nused_ref[0]
    nb = MOE_BUFFERS
    cur = i % nb
    buffer_of = lambda blk: (blk + nb) % nb

    tiles = xbuf.shape[1] // MOE_BLOCK

    def token(idx):
        return pl.ds(pl.multiple_of(idx, tiles), tiles)

    def start_gather(blk, buf):
        for r in range(MOE_BLOCK):
            pltpu.make_async_copy(h_hbm.at[token(src_ref[blk * MOE_BLOCK + r])],
                                  xbuf.at[buf, pl.ds(r * tiles, tiles)], gsem.at[buf]).start()

    def start_scatter(blk, buf):
        for r in range(MOE_BLOCK):
            pltpu.make_async_copy(ybuf.at[buf, pl.ds(r * tiles, tiles)],
                                  out_hbm.at[token(dst_ref[(blk + 1) * MOE_BLOCK + r])], ssem.at[buf]).start(
                                      priority=SCATTER_DMA_PRIORITY)

    def wait_gather(buf):
        pltpu.make_async_copy(h_hbm.at[pl.ds(0, MOE_BLOCK * tiles)], xbuf.at[buf], gsem.at[buf]).wait()

    def wait_scatter(buf):
        pltpu.make_async_copy(ybuf.at[buf], out_hbm.at[pl.ds(0, MOE_BLOCK * tiles)], ssem.at[buf]).wait()

    @pl.when(i < n_used)
    def _():
        @pl.when(i == 0)
        def _():
            ybuf[...] = jnp.zeros_like(ybuf)
            block_rows = MOE_BLOCK * tiles
            spare0 = out_hbm.shape[0] - nb * block_rows
            for b in range(nb):
                spare = pltpu.make_async_copy(ybuf.at[b], out_hbm.at[pl.ds(spare0 + b * block_rows, block_rows)],
                                              ssem.at[b])
                spare.start()
                spare.wait()
            for blk in range(nb - 1):
                start_gather(blk, blk)

        @pl.when((i == 0) | (be_ref[i] != be_ref[jnp.maximum(i - 1, 0)]))
        def _():
            w1b[...] = w1_ref[0, 0].astype(BF16)
            w3b[...] = w3_ref[0, 0].astype(BF16)
            w2b[...] = w2_ref[0, 0].astype(BF16)

        wait_gather(cur)

        @pl.when(i >= nb - 1)
        def _():
            wait_scatter(cur)

        xb = _load_token_tiles(xbuf.at[cur], tiles).astype(BF16)
        start_gather(i + nb - 1, buffer_of(i - 1))
        start_scatter(i - 1, buffer_of(i - 1))
        act = _silu(_dot(xb, w1b[...])) * _dot(xb, w3b[...])
        _store_token_tiles(ybuf.at[cur], _dot(act.astype(BF16), w2b[...]))

        @pl.when(i == n_used - 1)
        def _():
            start_scatter(i, cur)
            for back in range(nb):
                @pl.when(i - back >= -1)
                def _():
                    wait_scatter(buffer_of(i - back))
            for ahead in range(1, nb):
                wait_gather(buffer_of(i + ahead))


def _experts(h2, plan, w1, w3, w2, layer):
    block_expert, n_used, slot_src, slot_dst, n_blocks = plan
    d, ff = w1.shape[2], w1.shape[3]
    tiles = d // LANES
    nt = h2.shape[0] // tiles
    wspec = lambda shape: pl.BlockSpec((1, 1) + shape, lambda i, be, nu, sr, ds: (layer, be[i], 0, 0))
    row_buffers = pltpu.VMEM((MOE_BUFFERS, MOE_BLOCK * tiles, LANES), F32)
    grid_spec = pltpu.PrefetchScalarGridSpec(
        num_scalar_prefetch=4,
        grid=(n_blocks,),
        in_specs=[pl.BlockSpec(memory_space=pl.ANY), wspec((d, ff)), wspec((d, ff)), wspec((ff, d))],
        out_specs=pl.BlockSpec(memory_space=pl.ANY),
        scratch_shapes=[row_buffers, row_buffers,
                        pltpu.VMEM((d, ff), BF16), pltpu.VMEM((d, ff), BF16), pltpu.VMEM((ff, d), BF16),
                        pltpu.SemaphoreType.DMA((MOE_BUFFERS,)), pltpu.SemaphoreType.DMA((MOE_BUFFERS,))])
    return pl.pallas_call(
        _expert_kernel,
        out_shape=jax.ShapeDtypeStruct(((nt * MOE_TOP_K + MOE_BUFFERS * MOE_BLOCK) * tiles, LANES), F32),
        grid_spec=grid_spec,
        compiler_params=_cparams(1, has_side_effects=True, disable_bounds_checks=True),
    )(block_expert, n_used, slot_src, slot_dst, h2, w1, w3, w2)


def _combine_kernel(x_ref, y0_ref, y1_ref, route_ref, gate_ref, *rest, final):
    r = route_ref[...]
    tiles = x_ref.shape[1] // LANES
    y = r[:, 2:3] * _load_token_tiles(y0_ref, tiles) + r[:, 3:4] * _load_token_tiles(y1_ref, tiles)
    x = x_ref[...] + gate_ref[0] * y
    if final:
        fnw_ref, o_ref = rest
        x = x * lax.rsqrt(jnp.mean(x * x, axis=-1, keepdims=True) + EPS) * fnw_ref[...]
    else:
        (o_ref,) = rest
    o_ref[...] = x


def _combine(xs, y2, route, mods, dims, final_norm_w=None):
    nt, d = xs.shape
    batch, n_ctx, seq = dims
    tpb = (n_ctx + seq) // ROW_TILE
    ctx_tiles = n_ctx // ROW_TILE
    row = lambda width: pl.BlockSpec((ROW_TILE, width), lambda i: (i, 0))
    final = final_norm_w is not None
    first = pl.BlockSpec((ROW_TILE * d // LANES, LANES), lambda i: (i, 0))
    second = pl.BlockSpec((ROW_TILE * d // LANES, LANES), lambda i: (i + nt // ROW_TILE, 0))
    in_specs = [row(d), first, second, row(LANES), _mod_spec(5, d, tpb, ctx_tiles, batch)]
    args = [xs, y2, y2, route, mods]
    out_rows, out_spec = nt, row(d)
    if final:
        in_specs.append(pl.BlockSpec((1, d), lambda i: (0, 0)))
        args.append(final_norm_w.reshape(1, d))
        lat_tiles = seq // ROW_TILE
        out_rows = batch * seq
        out_spec = pl.BlockSpec(
            (ROW_TILE, d), lambda i: ((i // tpb) * lat_tiles + jnp.maximum(i % tpb - ctx_tiles, 0), 0))
    return pl.pallas_call(
        functools.partial(_combine_kernel, final=final),
        out_shape=jax.ShapeDtypeStruct((out_rows, d), F32),
        grid=(nt // ROW_TILE,),
        in_specs=in_specs,
        out_specs=out_spec,
        compiler_params=_cparams(1),
    )(*args)


def _even_col_ops():
    rh = RET_HEADS
    ops = [(True, 1.0, 1.0)] * rh + [(True, float(HEAD_DIM) ** -0.5, 1.0)] * rh + [(False, 1.0, 1.0)] * (2 * rh)
    ops += [(True, 1.0, float(HEAD_DIM) ** -0.5 * LOG2_E)] * ATT_Q_HEADS + [(True, 1.0, 1.0)] * ATT_KV_HEADS
    ops += [(False, 1.0, 1.0)] * ATT_KV_HEADS
    return tuple(ops)


def kernel(x, c, ctx, c_ctx, ada_w, ada_b, norm_w, final_norm_w, ev_w_in, ev_w_out, ret_decay_raw, att_sink,
           od_w_in, od_w_out, hg_lb_logits, hg_norm_w, moe_wg, moe_bg, moe_we, moe_be, moe_w1, moe_w3, moe_w2):
    batch, seq, d = x.shape
    n_ctx = ctx.shape[1]
    depth = ada_w.shape[0]
    assert batch + 1 <= MOD_ROWS and n_ctx % ROW_TILE == 0 and seq % ROW_TILE == 0
    assert seq % GRID_W == 0 and seq >= 3 * CHUNK
    dims = (batch, n_ctx, seq)
    tb = n_ctx + seq
    nt = batch * tb

    xs = jnp.concatenate([ctx, x], axis=1).reshape(nt, d)
    cvec = jnp.concatenate([c, c_ctx[None, :], jnp.zeros((MOD_ROWS - batch - 1, d), F32)], axis=0)
    mods_all = _ada_modulation(cvec, ada_w, ada_b)
    rope = _rope_tables(n_ctx, seq)
    even_ops = _even_col_ops()

    for l in range(depth):
        p = l // 2
        mods = mods_all[l].reshape(MOD_ROWS, 1, 6 * d)
        if l % 2 == 0:
            proj = _in_projection(xs, norm_w[l, 0], mods, ev_w_in[p].astype(BF16), dims, even_ops, rope)
            ret = _retention(proj, ret_decay_raw[p], dims)
            att = _attention(proj, att_sink[p], dims, 4 * RET_HEADS)
            parts, w_out = [ret, att], ev_w_out[p]
        else:
            proj = _in_projection(xs, norm_w[l, 0], mods, od_w_in[p].astype(BF16), dims)
            parts, w_out = [_hgrn2(proj, hg_lb_logits, hg_norm_w[p], l, dims)], od_w_out[p]
        n_logit = MOE_GROUPS + MOE_EXPERTS
        wr = jnp.concatenate([moe_wg[l], jnp.moveaxis(moe_we[l], 0, 1).reshape(d, MOE_EXPERTS),
                              jnp.zeros((d, LANES - n_logit), F32)], axis=1)
        wr_hi = wr.astype(BF16)
        wr = jnp.concatenate([wr_hi, (wr - wr_hi.astype(F32)).astype(BF16)], axis=1)
        br = jnp.concatenate([moe_bg[l], moe_be[l].reshape(-1), jnp.zeros((LANES - n_logit,), F32)])[None, :]
        xs, h2, route, counts = _out_projection_route(
            parts, w_out.astype(BF16), xs, norm_w[l, 1], mods, wr, br, dims)
        plan = _dispatch_plan(route, counts, nt, d // LANES)
        y2 = _experts(h2, plan, moe_w1, moe_w3, moe_w2, l)
        xs = _combine(xs, y2, route, mods, dims, final_norm_w if l == depth - 1 else None)

    return xs.reshape(batch, seq, d)
```

```python
import functools

import numpy as np
import jax
import jax.numpy as jnp
from jax import lax
from jax.experimental import pallas as pl
from jax.experimental.pallas import tpu as pltpu

F32 = jnp.float32
BF16 = jnp.bfloat16
HIGHEST = lax.Precision.HIGHEST

EPS = 1e-6
NEG_INF = -1e30
LOG2_E = 1.4426950408889634
LANES = 128
GRID_W = 64
ROPE_THETA = 10000.0
HEAD_DIM = 128
RET_HEADS = 4
ATT_Q_HEADS = 4
ATT_KV_HEADS = 2
ATT_GROUP = ATT_Q_HEADS // ATT_KV_HEADS
WINDOW = 128
HG_HEADS = 8
MOE_GROUPS = 4
MOE_EPG = 8
MOE_EXPERTS = MOE_GROUPS * MOE_EPG
MOE_TOP_K = 2
MOE_BLOCK = 128
CHUNK = 128
ROW_TILE = 256
MOD_ROWS = 8
VMEM_LIMIT = 56 * 1024 * 1024


def _cparams(n_axes, **kw):
    return pltpu.CompilerParams(dimension_semantics=("arbitrary",) * n_axes,
                                vmem_limit_bytes=VMEM_LIMIT, **kw)


def _dot(a, b):
    return jnp.dot(a, b, preferred_element_type=F32)


def _dot_nt(a, b):
    return lax.dot_general(a, b, (((1,), (1,)), ((), ())), preferred_element_type=F32)


def _dot_tn(a, b):
    return lax.dot_general(a, b, (((0,), (0,)), ((), ())), preferred_element_type=F32)


def _silu(x):
    return x * jax.nn.sigmoid(x)


def _norm_mod(x, nw, scale, shift):
    ms = jnp.mean(x * x, axis=-1, keepdims=True)
    y = x * lax.rsqrt(ms + EPS) * nw
    return y * (1.0 + scale) + shift


def _ada_kernel(c_ref, w_ref, b_ref, o_ref):
    s = _silu(c_ref[...])
    o_ref[0] = jnp.dot(s, w_ref[0], precision=HIGHEST, preferred_element_type=F32) + b_ref[0]


def _ada_modulation(cvec, ada_w, ada_b):
    depth, d, n6 = ada_w.shape
    tn = 1536
    return pl.pallas_call(
        _ada_kernel,
        out_shape=jax.ShapeDtypeStruct((depth, MOD_ROWS, n6), F32),
        grid=(depth, n6 // tn),
        in_specs=[pl.BlockSpec((MOD_ROWS, d), lambda l, j: (0, 0)),
                  pl.BlockSpec((1, d, tn), lambda l, j: (l, 0, j)),
                  pl.BlockSpec((1, 1, tn), lambda l, j: (l, 0, j))],
        out_specs=pl.BlockSpec((1, MOD_ROWS, tn), lambda l, j: (l, 0, j)),
        compiler_params=_cparams(2),
    )(cvec, ada_w, ada_b.reshape(depth, 1, n6))


def _swap_halves(x):
    lane = lax.broadcasted_iota(jnp.int32, x.shape, 1)
    return jnp.where((lane % 64) < 32, pltpu.roll(x, 96, 1), pltpu.roll(x, 32, 1))


def _proj_kernel(x_ref, nw_ref, shift_ref, scale_ref, w_ref, *rest, col_ops, chunk):
    if col_ops is None:
        (o_ref,) = rest
    else:
        cos_ref, sin_ref, o_ref = rest
    h = _norm_mod(x_ref[...], nw_ref[...], scale_ref[0], shift_ref[0]).astype(BF16)
    nout = o_ref.shape[0] * LANES
    for c0 in range(0, nout, chunk):
        acc = _dot(h, w_ref[:, c0:c0 + chunk])
        for hd in range(chunk // LANES):
            col = c0 + hd * LANES
            a = acc[:, hd * LANES:(hd + 1) * LANES]
            if col_ops is not None:
                rope, pre, post = col_ops[col // LANES]
                if pre != 1.0:
                    a = a * pre
                if rope:
                    a = a * cos_ref[...] + _swap_halves(a) * sin_ref[...]
                if post != 1.0:
                    a = a * post
            o_ref[col // LANES] = a


def _tile_mod_row(i, tiles_per_batch, ctx_tiles, batch):
    return jnp.where(i % tiles_per_batch < ctx_tiles, batch, i // tiles_per_batch)


def _mod_spec(chunk_idx, d, tiles_per_batch, ctx_tiles, batch):
    return pl.BlockSpec(
        (1, 1, d), lambda i: (_tile_mod_row(i, tiles_per_batch, ctx_tiles, batch), 0, chunk_idx))


def _in_projection(xs, nw, mods, w_bf, dims, col_ops=None, rope=None):
    nt, d = xs.shape
    nout = w_bf.shape[1]
    batch, n_ctx, seq = dims
    tpb = (n_ctx + seq) // ROW_TILE
    ctx_tiles = n_ctx // ROW_TILE
    in_specs = [pl.BlockSpec((ROW_TILE, d), lambda i: (i, 0)),
                pl.BlockSpec((1, d), lambda i: (0, 0)),
                _mod_spec(0, d, tpb, ctx_tiles, batch),
                _mod_spec(1, d, tpb, ctx_tiles, batch),
                pl.BlockSpec((d, nout), lambda i: (0, 0))]
    args = [xs, nw.reshape(1, d), mods, mods, w_bf]
    if col_ops is not None:
        rope_spec = pl.BlockSpec((ROW_TILE, LANES), lambda i: (i % tpb, 0))
        in_specs += [rope_spec, rope_spec]
        args += [rope[0], rope[1]]
    return pl.pallas_call(
        functools.partial(_proj_kernel, col_ops=col_ops, chunk=512),
        out_shape=jax.ShapeDtypeStruct((nout // LANES, nt, LANES), F32),
        grid=(nt // ROW_TILE,),
        in_specs=in_specs,
        out_specs=pl.BlockSpec((nout // LANES, ROW_TILE, LANES), lambda i: (0, i, 0)),
        compiler_params=_cparams(1),
    )(*args)


def _rope_tables(n_ctx, seq):
    n_rows = seq // GRID_W
    row = jnp.repeat(jnp.arange(n_rows, dtype=F32), GRID_W)
    col = jnp.tile(jnp.arange(GRID_W, dtype=F32), n_rows)
    axis_dim = HEAD_DIM // 2
    inv_freq = ROPE_THETA ** (-jnp.arange(0, axis_dim, 2, dtype=F32) / axis_dim)
    ang_r = row[:, None] * inv_freq[None, :]
    ang_c = col[:, None] * inv_freq[None, :]
    cos = jnp.concatenate([jnp.cos(ang_r), jnp.cos(ang_r), jnp.cos(ang_c), jnp.cos(ang_c)], axis=-1)
    sin = jnp.concatenate([-jnp.sin(ang_r), jnp.sin(ang_r), -jnp.sin(ang_c), jnp.sin(ang_c)], axis=-1)
    cos = jnp.concatenate([jnp.ones((n_ctx, HEAD_DIM), F32), cos], axis=0)
    sin = jnp.concatenate([jnp.zeros((n_ctx, HEAD_DIM), F32), sin], axis=0)
    return cos, sin


def _head_spec(rows, index_map):
    return pl.BlockSpec((None, rows, LANES), index_map)


HEAD_DMA_ROWS = 256


def _head_slabs(hbm_refs, head_offsets, bufs, sems, n_heads):
    step = pl.program_id(0) * n_heads + pl.program_id(1)
    n_steps = pl.num_programs(0) * n_heads
    rows = bufs[0].shape[1]
    piece = HEAD_DMA_ROWS
    assert rows % piece == 0

    def start(at_step, slot):
        b, h = at_step // n_heads, at_step % n_heads
        for j, (ref, off) in enumerate(zip(hbm_refs, head_offsets)):
            for p in range(rows // piece):
                pltpu.make_async_copy(ref.at[off + h, pl.ds(b * rows + p * piece, piece)],
                                      bufs[j].at[slot, pl.ds(p * piece, piece)], sems.at[j, slot]).start()

    slot = step % 2

    @pl.when(step == 0)
    def _():
        start(step, slot)

    @pl.when(step + 1 < n_steps)
    def _():
        start(step + 1, 1 - slot)

    for j, ref in enumerate(hbm_refs):
        pltpu.make_async_copy(ref.at[0, pl.ds(0, rows)], bufs[j].at[slot], sems.at[j, slot]).wait()
    return [buf.at[slot] for buf in bufs]


def _backward_chunk(t, ctx_chunks, n_chunks):
    return jnp.where(t < ctx_chunks, ctx_chunks - 1 - t, n_chunks - 1 - (t - ctx_chunks))


def _ret_kernel(raw_ref, proj_hbm, o_ref, qbuf, kbuf, vbuf, gbuf, sems, of_scr, ob_scr, s_scr, dec_scr, qd_scr,
                kd_scr, *, ctx_chunks, n_chunks):
    h = pl.program_id(1)
    c = CHUNK
    q_ref, k_ref, v_ref, g_ref = _head_slabs(
        [proj_hbm] * 4, [0, RET_HEADS, 2 * RET_HEADS, 3 * RET_HEADS], [qbuf, kbuf, vbuf, gbuf], sems, RET_HEADS)
    ii = lax.broadcasted_iota(jnp.int32, (c, c), 0).astype(F32)
    jj = lax.broadcasted_iota(jnp.int32, (c, c), 1).astype(F32)
    lg = []
    for d in range(2):
        lgd = -jnp.exp(jnp.full((c, c), raw_ref[d, h], F32))
        lg.append(lgd)
        rel = (ii - jj) if d == 0 else (jj - ii)
        dec_scr[d] = jnp.where(rel >= 0, jnp.exp(jnp.maximum(rel, 0.0) * lgd), 0.0)
        qd_scr[d] = jnp.exp(((ii + 1.0) if d == 0 else (c - ii)) * lgd)
        kd_scr[d] = jnp.exp(((c - 1.0 - ii) if d == 0 else ii) * lgd)

    def chunk(n, d):
        rows = pl.ds(pl.multiple_of(n * c, c), c)
        q = q_ref[rows, :]
        k = k_ref[rows, :]
        vb = v_ref[rows, :].astype(BF16)
        sc = _dot_nt(q.astype(BF16), k.astype(BF16)) * dec_scr[d]
        s = s_scr[d]
        o = _dot(sc.astype(BF16), vb) + _dot((q * qd_scr[d]).astype(BF16), s.astype(BF16))
        chunk_decay = jnp.exp(float(c) * lg[d][0:1, :])
        s_scr[d] = chunk_decay * s + _dot_tn((k * kd_scr[d]).astype(BF16), vb)
        return rows, o

    s_scr[...] = jnp.zeros_like(s_scr)

    def scan(t, carry):
        rows, o = chunk(t, 0)
        of_scr[rows, :] = o
        rows, o = chunk(_backward_chunk(t, ctx_chunks, n_chunks), 1)
        ob_scr[rows, :] = o
        return carry

    lax.fori_loop(0, n_chunks, scan, 0, unroll=2)

    def readout(n, carry):
        rows = pl.ds(pl.multiple_of(n * c, c), c)
        o = of_scr[rows, :] + ob_scr[rows, :]
        y = o * lax.rsqrt(jnp.mean(o * o, axis=-1, keepdims=True) + EPS)
        o_ref[rows, :] = (_silu(g_ref[rows, :]) * y).astype(BF16)
        return carry

    lax.fori_loop(0, n_chunks, readout, 0)


def _retention(proj, ret_decay_raw, dims):
    batch, n_ctx, seq = dims
    tb = n_ctx + seq
    slab = pltpu.VMEM((2, tb, LANES), F32)
    return pl.pallas_call(
        functools.partial(_ret_kernel, ctx_chunks=n_ctx // CHUNK, n_chunks=tb // CHUNK),
        out_shape=jax.ShapeDtypeStruct((RET_HEADS, batch * tb, LANES), BF16),
        grid=(batch, RET_HEADS),
        in_specs=[pl.BlockSpec(memory_space=pltpu.SMEM), pl.BlockSpec(memory_space=pl.ANY)],
        out_specs=_head_spec(tb, lambda b, h: (h, b, 0)),
        scratch_shapes=[slab, slab, slab, slab, pltpu.SemaphoreType.DMA((4, 2)),
                        pltpu.VMEM((tb, LANES), F32), pltpu.VMEM((tb, LANES), F32),
                        pltpu.VMEM((2, LANES, LANES), F32),
                        pltpu.VMEM((2, CHUNK, CHUNK), F32), pltpu.VMEM((2, CHUNK, LANES), F32),
                        pltpu.VMEM((2, CHUNK, LANES), F32)],
        compiler_params=_cparams(2),
    )(ret_decay_raw, proj)


def _att_kernel(sink_ref, q0_ref, q1_ref, k_ref, v_ref, o_ref, kb_scr, vb_scr, bias_scr, *, n_ctx, seq):
    hk = pl.program_id(1)
    c = CHUNK
    win = 3 * c
    kb_scr[...] = k_ref[...].astype(BF16)
    vb_scr[...] = v_ref[...].astype(BF16)
    kc = kb_scr[0:n_ctx, :]
    vc = vb_scr[0:n_ctx, :]
    ii = lax.broadcasted_iota(jnp.int32, (c, win), 0)
    jj = lax.broadcasted_iota(jnp.int32, (c, win), 1)
    for off in range(3):
        bias_scr[off] = jnp.where(jnp.abs(ii - jj + off * c) <= WINDOW, 0.0, NEG_INF)

    def softmax_out(s_ctx, sink, s_loc=None, v_loc=None):
        m = jnp.maximum(jnp.max(s_ctx, axis=-1, keepdims=True), sink)
        if s_loc is not None:
            m = jnp.maximum(m, jnp.max(s_loc, axis=-1, keepdims=True))
        p_ctx = jnp.exp2(s_ctx - m)
        den = jnp.sum(p_ctx, axis=-1, keepdims=True) + jnp.exp2(sink - m)
        o = _dot(p_ctx.astype(BF16), vc)
        if s_loc is not None:
            p_loc = jnp.exp2(s_loc - m)
            den = den + jnp.sum(p_loc, axis=-1, keepdims=True)
            o = o + _dot(p_loc.astype(BF16), v_loc)
        return o / den

    heads = []
    for g, q_ref in enumerate((q0_ref, q1_ref)):
        sink = jnp.full((c, 1), sink_ref[hk * ATT_GROUP + g], F32) * LOG2_E
        heads.append((q_ref, sink, g))

    for q_ref, sink, g in heads:
        for cc in range(n_ctx // c):
            qb = q_ref[cc * c:(cc + 1) * c, :].astype(BF16)
            o_ref[g, cc * c:(cc + 1) * c, :] = softmax_out(_dot_nt(qb, kc), sink).astype(BF16)

    def block(n, carry):
        q_rows = pl.ds(pl.multiple_of(n_ctx + n * c, c), c)
        start = jnp.clip(n * c - c, 0, seq - win)
        k_rows = pl.ds(pl.multiple_of(n_ctx + start, c), win)
        k_loc = kb_scr[k_rows, :]
        v_loc = vb_scr[k_rows, :]
        bias = bias_scr[(n * c - start) // c]
        for q_ref, sink, g in heads:
            qb = q_ref[q_rows, :].astype(BF16)
            o = softmax_out(_dot_nt(qb, kc), sink, _dot_nt(qb, k_loc) + bias, v_loc)
            o_ref[g, q_rows, :] = o.astype(BF16)
        return carry

    lax.fori_loop(0, seq // c, block, 0)


def _attention(proj, att_sink, dims, col0):
    batch, n_ctx, seq = dims
    tb = n_ctx + seq
    qb, kb, vb = col0, col0 + ATT_Q_HEADS, col0 + ATT_Q_HEADS + ATT_KV_HEADS
    return pl.pallas_call(
        functools.partial(_att_kernel, n_ctx=n_ctx, seq=seq),
        out_shape=jax.ShapeDtypeStruct((ATT_Q_HEADS, batch * tb, LANES), BF16),
        grid=(batch, ATT_KV_HEADS),
        in_specs=[pl.BlockSpec(memory_space=pltpu.SMEM),
                  _head_spec(tb, lambda b, h: (qb + ATT_GROUP * h, b, 0)),
                  _head_spec(tb, lambda b, h: (qb + ATT_GROUP * h + 1, b, 0)),
                  _head_spec(tb, lambda b, h: (kb + h, b, 0)),
                  _head_spec(tb, lambda b, h: (vb + h, b, 0))],
        out_specs=pl.BlockSpec((ATT_GROUP, tb, LANES), lambda b, h: (h, b, 0)),
        scratch_shapes=[pltpu.VMEM((tb, LANES), BF16), pltpu.VMEM((tb, LANES), BF16),
                        pltpu.VMEM((3, CHUNK, 3 * CHUNK), F32)],
        compiler_params=_cparams(2),
    )(att_sink, proj, proj, proj, proj)


HG_LEVELS = 7
HG_MXU_LEVELS = 3


def _hgrn_constants():
    c = CHUNK
    i = np.arange(c)[:, None]
    r = np.arange(c)[None, :]
    mats = [r <= i]
    for m in range(1, HG_MXU_LEVELS + 1):
        half = 1 << (m - 1)
        beta = (i // (2 * half)) * (2 * half) + half - 1
        upper = (i % (2 * half)) >= half
        mats.append((upper & (r > beta) & (r <= i)) | ((~upper) & (r > i) & (r <= beta)))
    fwd = [m.astype(np.float32) for m in mats]
    bwd = [m[::-1, ::-1] for m in fwd]
    stack = np.stack([np.concatenate(fwd, 0), np.concatenate(bwd, 0)])
    x = i ^ r
    level = np.where(r > i, -1, np.where(r == i, 0, np.floor(np.log2(np.maximum(x, 1))) + 1)).astype(np.int32)
    levels = np.stack([level, level[::-1, ::-1]])
    return stack, levels


def _hgrn_kernel(proj_hbm, lbl_ref, nw_ref, m_ref, lv_ref, o_ref, qbuf, zfbuf, zbbuf, vbuf, gbuf, sems,
                 of_scr, ob_scr, st_scr, *, layer, ctx_chunks, n_chunks):
    c = CHUNK
    q_ref, zf_ref, zb_ref, v_ref, g_ref = _head_slabs(
        [proj_hbm] * 5, [j * HG_HEADS for j in range(5)], [qbuf, zfbuf, zbbuf, vbuf, gbuf], sems, HG_HEADS)
    z = lbl_ref[...]
    e = jnp.exp(z - jnp.max(z, axis=0, keepdims=True))
    sm = e / jnp.sum(e, axis=0, keepdims=True)
    lb = jnp.sum(sm[1:layer + 1], axis=0, keepdims=True)
    row = lax.broadcasted_iota(jnp.int32, (c, LANES), 0)
    upper = [[((row if d == 0 else c - 1 - row) & (1 << (m - 1))) != 0 for m in range(1, HG_MXU_LEVELS + 1)]
             for d in range(2)]

    def level_operand(q, k, a, half, d):
        pieces = []
        for base in range(0, c, 2 * half):
            lo, hi = slice(base, base + half), slice(base + half, base + 2 * half)
            if d == 0:
                ref = a[base + half - 1:base + half]
                pieces += [k[lo] * jnp.exp2(ref - a[lo]), q[hi] * jnp.exp2(a[hi] - ref)]
            else:
                ref = a[base + half:base + half + 1]
                pieces += [q[lo] * jnp.exp2(a[lo] - ref), k[hi] * jnp.exp2(ref - a[hi])]
        return jnp.concatenate(pieces, axis=0).astype(BF16)

    def chunk(n, d, z_ref):
        rows = pl.ds(pl.multiple_of(n * c, c), c)
        q = q_ref[rows, :]
        v = v_ref[rows, :]
        f = lb + (1.0 - lb) * jax.nn.sigmoid(z_ref[rows, :])
        g = jnp.log2(f)
        k = 1.0 - f
        g_hi = g.astype(BF16)
        g_lo = (g - g_hi.astype(F32)).astype(BF16)
        res = _dot(m_ref[d], jnp.concatenate([g_hi, g_lo], axis=1))
        ex = res[:, :c] + res[:, c:]
        a = ex[0:c]
        st = st_scr[d]
        o = _dot_nt((q * jnp.exp2(a)).astype(BF16), st.astype(BF16))
        lv = lv_ref[d]
        sc = jnp.where(lv == 0, jnp.sum(q * k, axis=-1, keepdims=True), 0.0)
        for m in range(1, HG_LEVELS + 1):
            if m <= HG_MXU_LEVELS:
                zz = (jnp.where(upper[d][m - 1], q, k) * jnp.exp2(ex[m * c:(m + 1) * c])).astype(BF16)
            else:
                zz = level_operand(q, k, a, 1 << (m - 1), d)
            sc = jnp.where(lv == m, _dot_nt(zz, zz), sc)
        vb = v.astype(BF16)
        o = o + _dot(sc.astype(BF16), vb)
        a_last = a[c - 1:c] if d == 0 else a[0:1]
        kd = (k * jnp.exp2(a_last - a)).astype(BF16)
        st_scr[d] = st * jnp.exp2(a_last) + _dot_tn(vb, kd)
        return rows, o

    st_scr[...] = jnp.zeros_like(st_scr)

    def scan(t, carry):
        rows, o = chunk(t, 0, zf_ref)
        of_scr[rows, :] = o
        rows, o = chunk(_backward_chunk(t, ctx_chunks, n_chunks), 1, zb_ref)
        ob_scr[rows, :] = o
        return carry

    lax.fori_loop(0, n_chunks, scan, 0, unroll=2)

    def readout(n, carry):
        rows = pl.ds(pl.multiple_of(n * c, c), c)
        o = of_scr[rows, :] + ob_scr[rows, :]
        y = o * lax.rsqrt(jnp.mean(o * o, axis=-1, keepdims=True) + EPS) * nw_ref[...]
        o_ref[rows, :] = (y * _silu(g_ref[rows, :])).astype(BF16)
        return carry

    lax.fori_loop(0, n_chunks, readout, 0)


def _hgrn2(proj, hg_lb_logits, hg_norm_w, layer, dims):
    batch, n_ctx, seq = dims
    tb = n_ctx + seq
    depth = hg_lb_logits.shape[0]
    stack, levels = _hgrn_constants()
    slab = pltpu.VMEM((2, tb, LANES), F32)
    n_mats = 1 + HG_MXU_LEVELS
    return pl.pallas_call(
        functools.partial(_hgrn_kernel, layer=layer, ctx_chunks=n_ctx // CHUNK, n_chunks=tb // CHUNK),
        out_shape=jax.ShapeDtypeStruct((HG_HEADS, batch * tb, LANES), BF16),
        grid=(batch, HG_HEADS),
        in_specs=[pl.BlockSpec(memory_space=pl.ANY),
                  pl.BlockSpec((depth, LANES), lambda b, h: (0, h)),
                  pl.BlockSpec((1, LANES), lambda b, h: (0, 0)),
                  pl.BlockSpec((2, n_mats * CHUNK, CHUNK), lambda b, h: (0, 0, 0)),
                  pl.BlockSpec((2, CHUNK, CHUNK), lambda b, h: (0, 0, 0))],
        out_specs=_head_spec(tb, lambda b, h: (h, b, 0)),
        scratch_shapes=[slab, slab, slab, slab, slab, pltpu.SemaphoreType.DMA((5, 2)),
                        pltpu.VMEM((tb, LANES), F32), pltpu.VMEM((tb, LANES), F32),
                        pltpu.VMEM((2, LANES, LANES), F32)],
        compiler_params=_cparams(2),
    )(proj, hg_lb_logits, hg_norm_w.reshape(1, LANES),
      jnp.asarray(stack, BF16), jnp.asarray(levels, jnp.int32))


SUBLANES = 8


def _store_token_tiles(ref, x):
    rows, tiles = x.shape[0], x.shape[1] // LANES
    for s in range(tiles):
        ref[pl.ds(s, rows, stride=tiles), :] = x[:, s * LANES:(s + 1) * LANES]


def _load_token_tiles(ref, tiles):
    rows = ref.shape[0] // tiles
    return jnp.concatenate([ref[pl.ds(s, rows, stride=tiles), :] for s in range(tiles)], axis=1)


def _route(logits, tri, counts):
    lane = lax.broadcasted_iota(jnp.int32, logits.shape, 1).astype(F32)
    big = float(LANES)
    first = lambda hit: jnp.min(jnp.where(hit, lane, big), axis=-1, keepdims=True)
    is_g = lane < MOE_GROUPS
    gl = jnp.where(is_g, logits, NEG_INF)
    gmax = jnp.max(gl, axis=-1, keepdims=True)
    gsum = jnp.sum(jnp.where(is_g, jnp.exp(gl - gmax), 0.0), axis=-1, keepdims=True)
    g_p = 1.0 / gsum
    lo = MOE_GROUPS + first(gl == gmax) * MOE_EPG
    el = jnp.where((lane >= lo) & (lane < lo + MOE_EPG), logits, NEG_INF)
    e1v = jnp.max(el, axis=-1, keepdims=True)
    e1 = first(el == e1v)
    el2 = jnp.where(lane == e1, NEG_INF, el)
    e2v = jnp.max(el2, axis=-1, keepdims=True)
    e2 = first(el2 == e2v)
    t = jnp.exp(e2v - e1v)
    w1 = 1.0 / (1.0 + t)
    w2 = t * w1
    id1 = e1 - MOE_GROUPS
    id2 = e2 - MOE_GROUPS
    onehot = jnp.where((lane == id1) | (lane == id2), 1.0, 0.0)
    before = counts + _dot(tri, onehot.astype(BF16))
    rank1 = jnp.sum(jnp.where(lane == id1, before, 0.0), axis=-1, keepdims=True)
    rank2 = jnp.sum(jnp.where(lane == id2, before, 0.0), axis=-1, keepdims=True)
    slab = jnp.zeros_like(logits)
    for col, val in enumerate((id1, id2, g_p * w1, g_p * w2, rank1, rank2)):
        slab = jnp.where(lane == col, val, slab)
    return slab, counts + jnp.sum(onehot, axis=0, keepdims=True)


def _out_route_kernel(*refs, n_parts):
    a_refs = refs[:n_parts]
    (w_ref, x_ref, gate_ref, nw_ref, shift_ref, scale_ref, wr_ref, br_ref, tri_ref,
     xo_ref, h2_ref, route_ref, cnt_ref) = refs[n_parts:]

    @pl.when(pl.program_id(0) == 0)
    def _():
        cnt_ref[...] = jnp.zeros_like(cnt_ref)

    k0 = 0
    y = None
    for a_ref in a_refs:
        a = jnp.concatenate([a_ref[h] for h in range(a_ref.shape[0])], axis=1)
        kk = a.shape[1]
        t = _dot(a, w_ref[k0:k0 + kk, :])
        y = t if y is None else y + t
        k0 += kk
    x = x_ref[...] + gate_ref[0] * y
    xo_ref[...] = x
    h2 = _norm_mod(x, nw_ref[...], scale_ref[0], shift_ref[0])
    _store_token_tiles(h2_ref, h2)
    h_hi = h2.astype(BF16)
    h_lo = (h2 - h_hi.astype(F32)).astype(BF16)
    both = _dot(h_hi, wr_ref[...])
    logits = both[:, :LANES] + both[:, LANES:] + _dot(h_lo, wr_ref[:, :LANES]) + br_ref[...]
    slab, counts = _route(logits, tri_ref[...], cnt_ref[0:1, :])
    route_ref[...] = slab
    cnt_ref[...] = jnp.broadcast_to(counts, cnt_ref.shape)


def _out_projection_route(parts, w_bf, xs, nw, mods, wr, br, dims):
    nt, d = xs.shape
    batch, n_ctx, seq = dims
    tpb = (n_ctx + seq) // ROW_TILE
    ctx_tiles = n_ctx // ROW_TILE
    row = lambda width: pl.BlockSpec((ROW_TILE, width), lambda i: (i, 0))
    full = lambda shape: pl.BlockSpec(shape, lambda i: (0,) * len(shape))
    mod = lambda chunk_idx: _mod_spec(chunk_idx, d, tpb, ctx_tiles, batch)
    tri = jnp.asarray(np.tril(np.ones((ROW_TILE, ROW_TILE), np.float32), -1), BF16)
    return pl.pallas_call(
        functools.partial(_out_route_kernel, n_parts=len(parts)),
        out_shape=(jax.ShapeDtypeStruct((nt, d), F32), jax.ShapeDtypeStruct((nt * d // LANES, LANES), F32),
                   jax.ShapeDtypeStruct((nt, LANES), F32), jax.ShapeDtypeStruct((8, LANES), F32)),
        grid=(nt // ROW_TILE,),
        in_specs=[pl.BlockSpec((p.shape[0], ROW_TILE, LANES), lambda i: (0, i, 0)) for p in parts] + [
            full(w_bf.shape), row(d), mod(2), full((1, d)), mod(3), mod(4), full(wr.shape), full(br.shape),
            full(tri.shape)],
        out_specs=(row(d), pl.BlockSpec((ROW_TILE * d // LANES, LANES), lambda i: (i, 0)), row(LANES),
                   full((8, LANES))),
        compiler_params=_cparams(1),
    )(*parts, w_bf, xs, mods, nw.reshape(1, d), mods, mods, wr, br, tri)


MOE_BUFFERS = 4


def _dispatch_plan(route, counts_f, n_tok, tiles):
    n_assign = n_tok * MOE_TOP_K
    flat_e = route[:, 0:MOE_TOP_K].astype(jnp.int32).reshape(-1)
    rank = route[:, 4:4 + MOE_TOP_K].astype(jnp.int32).reshape(-1)
    counts = counts_f[0, :MOE_EXPERTS].astype(jnp.int32)
    padded = (counts + MOE_BLOCK - 1) // MOE_BLOCK * MOE_BLOCK
    pad_end = jnp.cumsum(padded)
    pad_start = pad_end - padded
    dest = pad_start[flat_e] + rank
    n_blocks = -(-n_assign // MOE_BLOCK) + MOE_EXPERTS
    n_slots = n_blocks * MOE_BLOCK
    block_start = jnp.arange(n_blocks, dtype=jnp.int32) * MOE_BLOCK
    block_expert = jnp.minimum(jnp.sum((pad_end[None, :] <= block_start[:, None]).astype(jnp.int32), axis=1),
                               MOE_EXPERTS - 1)
    slot_assign = jnp.full((n_slots,), -1, jnp.int32).at[dest].set(
        jnp.arange(n_assign, dtype=jnp.int32), unique_indices=True)
    lead, trail = MOE_BLOCK, (MOE_BUFFERS - 1) * MOE_BLOCK
    slot = jnp.arange(-lead, n_slots + trail, dtype=jnp.int32)
    spare = n_assign + ((slot // MOE_BLOCK) % MOE_BUFFERS) * MOE_BLOCK + slot % MOE_BLOCK
    assign = jnp.concatenate([jnp.full((lead,), -1, jnp.int32), slot_assign, jnp.full((trail,), -1, jnp.int32)])
    src = jnp.where(assign >= 0, assign // MOE_TOP_K, 0) * tiles
    dst = jnp.where(assign >= 0, (assign % MOE_TOP_K) * n_tok + assign // MOE_TOP_K, spare) * tiles
    n_used = (pad_end[-1:] // MOE_BLOCK).astype(jnp.int32)
    return block_expert, n_used, src[lead:], dst[:n_slots + lead], n_blocks


def _expert_kernel(be_ref, nused_ref, src_ref, dst_ref, h_hbm, w1_ref, w3_ref, w2_ref, out_hbm,
                   xbuf, ybuf, w1b, w3b, w2b, gsem, ssem):
    i = pl.program_id(0)
    n_used = nused_ref[0]
    nb = MOE_BUFFERS
    cur = i % nb
    buffer_of = lambda blk: (blk + nb) % nb

    tiles = xbuf.shape[1] // MOE_BLOCK

    def token(idx):
        return pl.ds(pl.multiple_of(idx, tiles), tiles)

    def start_gather(blk, buf):
        for r in range(MOE_BLOCK):
            pltpu.make_async_copy(h_hbm.at[token(src_ref[blk * MOE_BLOCK + r])],
                                  xbuf.at[buf, pl.ds(r * tiles, tiles)], gsem.at[buf]).start()

    def start_scatter(blk, buf):
        for r in range(MOE_BLOCK):
            pltpu.make_async_copy(ybuf.at[buf, pl.ds(r * tiles, tiles)],
                                  out_hbm.at[token(dst_ref[(blk + 1) * MOE_BLOCK + r])], ssem.at[buf]).start()

    def wait_gather(buf):
        pltpu.make_async_copy(h_hbm.at[pl.ds(0, MOE_BLOCK * tiles)], xbuf.at[buf], gsem.at[buf]).wait()

    def wait_scatter(buf):
        pltpu.make_async_copy(ybuf.at[buf], out_hbm.at[pl.ds(0, MOE_BLOCK * tiles)], ssem.at[buf]).wait()

    @pl.when(i < n_used)
    def _():
        @pl.when(i == 0)
        def _():
            ybuf[...] = jnp.zeros_like(ybuf)
            block_rows = MOE_BLOCK * tiles
            spare0 = out_hbm.shape[0] - nb * block_rows
            for b in range(nb):
                spare = pltpu.make_async_copy(ybuf.at[b], out_hbm.at[pl.ds(spare0 + b * block_rows, block_rows)],
                                              ssem.at[b])
                spare.start()
                spare.wait()
            for blk in range(nb - 1):
                start_gather(blk, blk)

        @pl.when((i == 0) | (be_ref[i] != be_ref[jnp.maximum(i - 1, 0)]))
        def _():
            w1b[...] = w1_ref[0, 0].astype(BF16)
            w3b[...] = w3_ref[0, 0].astype(BF16)
            w2b[...] = w2_ref[0, 0].astype(BF16)

        wait_gather(cur)

        @pl.when(i >= nb - 1)
        def _():
            wait_scatter(cur)

        xb = _load_token_tiles(xbuf.at[cur], tiles).astype(BF16)
        start_gather(i + nb - 1, buffer_of(i - 1))
        start_scatter(i - 1, buffer_of(i - 1))
        act = _silu(_dot(xb, w1b[...])) * _dot(xb, w3b[...])
        _store_token_tiles(ybuf.at[cur], _dot(act.astype(BF16), w2b[...]))

        @pl.when(i == n_used - 1)
        def _():
            start_scatter(i, cur)
            for back in range(nb):
                @pl.when(i - back >= -1)
                def _():
                    wait_scatter(buffer_of(i - back))
            for ahead in range(1, nb):
                wait_gather(buffer_of(i + ahead))


def _experts(h2, plan, w1, w3, w2, layer):
    block_expert, n_used, slot_src, slot_dst, n_blocks = plan
    d, ff = w1.shape[2], w1.shape[3]
    tiles = d // LANES
    nt = h2.shape[0] // tiles
    wspec = lambda shape: pl.BlockSpec((1, 1) + shape, lambda i, be, nu, sr, ds: (layer, be[i], 0, 0))
    row_buffers = pltpu.VMEM((MOE_BUFFERS, MOE_BLOCK * tiles, LANES), F32)
    grid_spec = pltpu.PrefetchScalarGridSpec(
        num_scalar_prefetch=4,
        grid=(n_blocks,),
        in_specs=[pl.BlockSpec(memory_space=pl.ANY), wspec((d, ff)), wspec((d, ff)), wspec((ff, d))],
        out_specs=pl.BlockSpec(memory_space=pl.ANY),
        scratch_shapes=[row_buffers, row_buffers,
                        pltpu.VMEM((d, ff), BF16), pltpu.VMEM((d, ff), BF16), pltpu.VMEM((ff, d), BF16),
                        pltpu.SemaphoreType.DMA((MOE_BUFFERS,)), pltpu.SemaphoreType.DMA((MOE_BUFFERS,))])
    return pl.pallas_call(
        _expert_kernel,
        out_shape=jax.ShapeDtypeStruct(((nt * MOE_TOP_K + MOE_BUFFERS * MOE_BLOCK) * tiles, LANES), F32),
        grid_spec=grid_spec,
        compiler_params=_cparams(1, has_side_effects=True, disable_bounds_checks=True),
    )(block_expert, n_used, slot_src, slot_dst, h2, w1, w3, w2)


def _combine_kernel(x_ref, y0_ref, y1_ref, route_ref, gate_ref, *rest, final):
    r = route_ref[...]
    tiles = x_ref.shape[1] // LANES
    y = r[:, 2:3] * _load_token_tiles(y0_ref, tiles) + r[:, 3:4] * _load_token_tiles(y1_ref, tiles)
    x = x_ref[...] + gate_ref[0] * y
    if final:
        fnw_ref, o_ref = rest
        x = x * lax.rsqrt(jnp.mean(x * x, axis=-1, keepdims=True) + EPS) * fnw_ref[...]
    else:
        (o_ref,) = rest
    o_ref[...] = x


def _combine(xs, y2, route, mods, dims, final_norm_w=None):
    nt, d = xs.shape
    batch, n_ctx, seq = dims
    tpb = (n_ctx + seq) // ROW_TILE
    ctx_tiles = n_ctx // ROW_TILE
    row = lambda width: pl.BlockSpec((ROW_TILE, width), lambda i: (i, 0))
    final = final_norm_w is not None
    first = pl.BlockSpec((ROW_TILE * d // LANES, LANES), lambda i: (i, 0))
    second = pl.BlockSpec((ROW_TILE * d // LANES, LANES), lambda i: (i + nt // ROW_TILE, 0))
    in_specs = [row(d), first, second, row(LANES), _mod_spec(5, d, tpb, ctx_tiles, batch)]
    args = [xs, y2, y2, route, mods]
    out_rows, out_spec = nt, row(d)
    if final:
        in_specs.append(pl.BlockSpec((1, d), lambda i: (0, 0)))
        args.append(final_norm_w.reshape(1, d))
        lat_tiles = seq // ROW_TILE
        out_rows = batch * seq
        out_spec = pl.BlockSpec(
            (ROW_TILE, d), lambda i: ((i // tpb) * lat_tiles + jnp.maximum(i % tpb - ctx_tiles, 0), 0))
    return pl.pallas_call(
        functools.partial(_combine_kernel, final=final),
        out_shape=jax.ShapeDtypeStruct((out_rows, d), F32),
        grid=(nt // ROW_TILE,),
        in_specs=in_specs,
        out_specs=out_spec,
        compiler_params=_cparams(1),
    )(*args)


def _even_col_ops():
    rh = RET_HEADS
    ops = [(True, 1.0, 1.0)] * rh + [(True, float(HEAD_DIM) ** -0.5, 1.0)] * rh + [(False, 1.0, 1.0)] * (2 * rh)
    ops += [(True, 1.0, float(HEAD_DIM) ** -0.5 * LOG2_E)] * ATT_Q_HEADS + [(True, 1.0, 1.0)] * ATT_KV_HEADS
    ops += [(False, 1.0, 1.0)] * ATT_KV_HEADS
    return tuple(ops)


def kernel(x, c, ctx, c_ctx, ada_w, ada_b, norm_w, final_norm_w, ev_w_in, ev_w_out, ret_decay_raw, att_sink,
           od_w_in, od_w_out, hg_lb_logits, hg_norm_w, moe_wg, moe_bg, moe_we, moe_be, moe_w1, moe_w3, moe_w2):
    batch, seq, d = x.shape
    n_ctx = ctx.shape[1]
    depth = ada_w.shape[0]
    assert batch + 1 <= MOD_ROWS and n_ctx % ROW_TILE == 0 and seq % ROW_TILE == 0
    assert seq % GRID_W == 0 and seq >= 3 * CHUNK
    dims = (batch, n_ctx, seq)
    tb = n_ctx + seq
    nt = batch * tb

    xs = jnp.concatenate([ctx, x], axis=1).reshape(nt, d)
    cvec = jnp.concatenate([c, c_ctx[None, :], jnp.zeros((MOD_ROWS - batch - 1, d), F32)], axis=0)
    mods_all = _ada_modulation(cvec, ada_w, ada_b)
    rope = _rope_tables(n_ctx, seq)
    even_ops = _even_col_ops()

    for l in range(depth):
        p = l // 2
        mods = mods_all[l].reshape(MOD_ROWS, 1, 6 * d)
        if l % 2 == 0:
            proj = _in_projection(xs, norm_w[l, 0], mods, ev_w_in[p].astype(BF16), dims, even_ops, rope)
            ret = _retention(proj, ret_decay_raw[p], dims)
            att = _attention(proj, att_sink[p], dims, 4 * RET_HEADS)
            parts, w_out = [ret, att], ev_w_out[p]
        else:
            proj = _in_projection(xs, norm_w[l, 0], mods, od_w_in[p].astype(BF16), dims)
            parts, w_out = [_hgrn2(proj, hg_lb_logits, hg_norm_w[p], l, dims)], od_w_out[p]
        n_logit = MOE_GROUPS + MOE_EXPERTS
        wr = jnp.concatenate([moe_wg[l], jnp.moveaxis(moe_we[l], 0, 1).reshape(d, MOE_EXPERTS),
                              jnp.zeros((d, LANES - n_logit), F32)], axis=1)
        wr_hi = wr.astype(BF16)
        wr = jnp.concatenate([wr_hi, (wr - wr_hi.astype(F32)).astype(BF16)], axis=1)
        br = jnp.concatenate([moe_bg[l], moe_be[l].reshape(-1), jnp.zeros((LANES - n_logit,), F32)])[None, :]
        xs, h2, route, counts = _out_projection_route(
            parts, w_out.astype(BF16), xs, norm_w[l, 1], mods, wr, br, dims)
        plan = _dispatch_plan(route, counts, nt, d // LANES)
        y2 = _experts(h2, plan, moe_w1, moe_w3, moe_w2, l)
        xs = _combine(xs, y2, route, mods, dims, final_norm_w if l == depth - 1 else None)

    return xs.reshape(batch, seq, d)
```

```python
import functools

import numpy as np
import jax
import jax.numpy as jnp
from jax import lax
from jax.experimental import pallas as pl
from jax.experimental.pallas import tpu as pltpu

F32 = jnp.float32
BF16 = jnp.bfloat16
HIGHEST = lax.Precision.HIGHEST

EPS = 1e-6
NEG_INF = -1e30
LOG2_E = 1.4426950408889634
LANES = 128
GRID_W = 64
ROPE_THETA = 10000.0
HEAD_DIM = 128
RET_HEADS = 4
ATT_Q_HEADS = 4
ATT_KV_HEADS = 2
ATT_GROUP = ATT_Q_HEADS // ATT_KV_HEADS
WINDOW = 128
HG_HEADS = 8
MOE_GROUPS = 4
MOE_EPG = 8
MOE_EXPERTS = MOE_GROUPS * MOE_EPG
MOE_TOP_K = 2
MOE_BLOCK = 128
CHUNK = 128
ROW_TILE = 256
MOD_ROWS = 8
VMEM_LIMIT = 56 * 1024 * 1024


def _cparams(n_axes, **kw):
    return pltpu.CompilerParams(dimension_semantics=("arbitrary",) * n_axes,
                                vmem_limit_bytes=VMEM_LIMIT, **kw)


def _dot(a, b):
    return jnp.dot(a, b, preferred_element_type=F32)


def _dot_nt(a, b):
    return lax.dot_general(a, b, (((1,), (1,)), ((), ())), preferred_element_type=F32)


def _dot_tn(a, b):
    return lax.dot_general(a, b, (((0,), (0,)), ((), ())), preferred_element_type=F32)


def _silu(x):
    return x * jax.nn.sigmoid(x)


def _norm_mod(x, nw, scale, shift):
    ms = jnp.mean(x * x, axis=-1, keepdims=True)
    y = x * lax.rsqrt(ms + EPS) * nw
    return y * (1.0 + scale) + shift


def _ada_kernel(c_ref, w_ref, b_ref, o_ref):
    s = _silu(c_ref[...])
    o_ref[0] = jnp.dot(s, w_ref[0], precision=HIGHEST, preferred_element_type=F32) + b_ref[0]


def _ada_modulation(cvec, ada_w, ada_b):
    depth, d, n6 = ada_w.shape
    tn = 1536
    return pl.pallas_call(
        _ada_kernel,
        out_shape=jax.ShapeDtypeStruct((depth, MOD_ROWS, n6), F32),
        grid=(depth, n6 // tn),
        in_specs=[pl.BlockSpec((MOD_ROWS, d), lambda l, j: (0, 0)),
                  pl.BlockSpec((1, d, tn), lambda l, j: (l, 0, j)),
                  pl.BlockSpec((1, 1, tn), lambda l, j: (l, 0, j))],
        out_specs=pl.BlockSpec((1, MOD_ROWS, tn), lambda l, j: (l, 0, j)),
        compiler_params=_cparams(2),
    )(cvec, ada_w, ada_b.reshape(depth, 1, n6))


def _swap_halves(x):
    lane = lax.broadcasted_iota(jnp.int32, x.shape, 1)
    return jnp.where((lane % 64) < 32, pltpu.roll(x, 96, 1), pltpu.roll(x, 32, 1))


def _proj_kernel(x_ref, nw_ref, shift_ref, scale_ref, w_ref, *rest, col_ops, chunk):
    if col_ops is None:
        (o_ref,) = rest
    else:
        cos_ref, sin_ref, o_ref = rest
    h = _norm_mod(x_ref[...], nw_ref[...], scale_ref[0], shift_ref[0]).astype(BF16)
    nout = o_ref.shape[0] * LANES
    for c0 in range(0, nout, chunk):
        acc = _dot(h, w_ref[:, c0:c0 + chunk])
        for hd in range(chunk // LANES):
            col = c0 + hd * LANES
            a = acc[:, hd * LANES:(hd + 1) * LANES]
            if col_ops is not None:
                rope, pre, post = col_ops[col // LANES]
                if pre != 1.0:
                    a = a * pre
                if rope:
                    a = a * cos_ref[...] + _swap_halves(a) * sin_ref[...]
                if post != 1.0:
                    a = a * post
            o_ref[col // LANES] = a


def _tile_mod_row(i, tiles_per_batch, ctx_tiles, batch):
    return jnp.where(i % tiles_per_batch < ctx_tiles, batch, i // tiles_per_batch)


def _mod_spec(chunk_idx, d, tiles_per_batch, ctx_tiles, batch):
    return pl.BlockSpec(
        (1, 1, d), lambda i: (_tile_mod_row(i, tiles_per_batch, ctx_tiles, batch), 0, chunk_idx))


def _in_projection(xs, nw, mods, w_bf, dims, col_ops=None, rope=None):
    nt, d = xs.shape
    nout = w_bf.shape[1]
    batch, n_ctx, seq = dims
    tpb = (n_ctx + seq) // ROW_TILE
    ctx_tiles = n_ctx // ROW_TILE
    in_specs = [pl.BlockSpec((ROW_TILE, d), lambda i: (i, 0)),
                pl.BlockSpec((1, d), lambda i: (0, 0)),
                _mod_spec(0, d, tpb, ctx_tiles, batch),
                _mod_spec(1, d, tpb, ctx_tiles, batch),
                pl.BlockSpec((d, nout), lambda i: (0, 0))]
    args = [xs, nw.reshape(1, d), mods, mods, w_bf]
    if col_ops is not None:
        rope_spec = pl.BlockSpec((ROW_TILE, LANES), lambda i: (i % tpb, 0))
        in_specs += [rope_spec, rope_spec]
        args += [rope[0], rope[1]]
    return pl.pallas_call(
        functools.partial(_proj_kernel, col_ops=col_ops, chunk=512),
        out_shape=jax.ShapeDtypeStruct((nout // LANES, nt, LANES), F32),
        grid=(nt // ROW_TILE,),
        in_specs=in_specs,
        out_specs=pl.BlockSpec((nout // LANES, ROW_TILE, LANES), lambda i: (0, i, 0)),
        compiler_params=_cparams(1),
    )(*args)


def _rope_tables(n_ctx, seq):
    n_rows = seq // GRID_W
    row = jnp.repeat(jnp.arange(n_rows, dtype=F32), GRID_W)
    col = jnp.tile(jnp.arange(GRID_W, dtype=F32), n_rows)
    axis_dim = HEAD_DIM // 2
    inv_freq = ROPE_THETA ** (-jnp.arange(0, axis_dim, 2, dtype=F32) / axis_dim)
    ang_r = row[:, None] * inv_freq[None, :]
    ang_c = col[:, None] * inv_freq[None, :]
    cos = jnp.concatenate([jnp.cos(ang_r), jnp.cos(ang_r), jnp.cos(ang_c), jnp.cos(ang_c)], axis=-1)
    sin = jnp.concatenate([-jnp.sin(ang_r), jnp.sin(ang_r), -jnp.sin(ang_c), jnp.sin(ang_c)], axis=-1)
    cos = jnp.concatenate([jnp.ones((n_ctx, HEAD_DIM), F32), cos], axis=0)
    sin = jnp.concatenate([jnp.zeros((n_ctx, HEAD_DIM), F32), sin], axis=0)
    return cos, sin


def _head_spec(rows, index_map):
    return pl.BlockSpec((None, rows, LANES), index_map)


HEAD_DMA_ROWS = 256


def _head_slabs(hbm_refs, head_offsets, bufs, sems, n_heads):
    step = pl.program_id(0) * n_heads + pl.program_id(1)
    n_steps = pl.num_programs(0) * n_heads
    rows = bufs[0].shape[1]
    piece = HEAD_DMA_ROWS
    assert rows % piece == 0

    def start(at_step, slot):
        b, h = at_step // n_heads, at_step % n_heads
        for j, (ref, off) in enumerate(zip(hbm_refs, head_offsets)):
            for p in range(rows // piece):
                pltpu.make_async_copy(ref.at[off + h, pl.ds(b * rows + p * piece, piece)],
                                      bufs[j].at[slot, pl.ds(p * piece, piece)], sems.at[j, slot]).start()

    slot = step % 2

    @pl.when(step == 0)
    def _():
        start(step, slot)

    @pl.when(step + 1 < n_steps)
    def _():
        start(step + 1, 1 - slot)

    for j, ref in enumerate(hbm_refs):
        pltpu.make_async_copy(ref.at[0, pl.ds(0, rows)], bufs[j].at[slot], sems.at[j, slot]).wait()
    return [buf.at[slot] for buf in bufs]


def _backward_chunk(t, ctx_chunks, n_chunks):
    return jnp.where(t < ctx_chunks, ctx_chunks - 1 - t, n_chunks - 1 - (t - ctx_chunks))


def _ret_kernel(raw_ref, proj_hbm, o_ref, qbuf, kbuf, vbuf, gbuf, sems, of_scr, ob_scr, s_scr, dec_scr, qd_scr,
                kd_scr, *, ctx_chunks, n_chunks):
    h = pl.program_id(1)
    c = CHUNK
    q_ref, k_ref, v_ref, g_ref = _head_slabs(
        [proj_hbm] * 4, [0, RET_HEADS, 2 * RET_HEADS, 3 * RET_HEADS], [qbuf, kbuf, vbuf, gbuf], sems, RET_HEADS)
    ii = lax.broadcasted_iota(jnp.int32, (c, c), 0).astype(F32)
    jj = lax.broadcasted_iota(jnp.int32, (c, c), 1).astype(F32)
    lg = []
    for d in range(2):
        lgd = -jnp.exp(jnp.full((c, c), raw_ref[d, h], F32))
        lg.append(lgd)
        rel = (ii - jj) if d == 0 else (jj - ii)
        dec_scr[d] = jnp.where(rel >= 0, jnp.exp(jnp.maximum(rel, 0.0) * lgd), 0.0)
        qd_scr[d] = jnp.exp(((ii + 1.0) if d == 0 else (c - ii)) * lgd)
        kd_scr[d] = jnp.exp(((c - 1.0 - ii) if d == 0 else ii) * lgd)

    def chunk(n, d):
        rows = pl.ds(pl.multiple_of(n * c, c), c)
        q = q_ref[rows, :]
        k = k_ref[rows, :]
        vb = v_ref[rows, :].astype(BF16)
        sc = _dot_nt(q.astype(BF16), k.astype(BF16)) * dec_scr[d]
        s = s_scr[d]
        o = _dot(sc.astype(BF16), vb) + _dot((q * qd_scr[d]).astype(BF16), s.astype(BF16))
        chunk_decay = jnp.exp(float(c) * lg[d][0:1, :])
        s_scr[d] = chunk_decay * s + _dot_tn((k * kd_scr[d]).astype(BF16), vb)
        return rows, o

    s_scr[...] = jnp.zeros_like(s_scr)

    def scan(t, carry):
        rows, o = chunk(t, 0)
        of_scr[rows, :] = o
        rows, o = chunk(_backward_chunk(t, ctx_chunks, n_chunks), 1)
        ob_scr[rows, :] = o
        return carry

    lax.fori_loop(0, n_chunks, scan, 0, unroll=2)

    def readout(n, carry):
        rows = pl.ds(pl.multiple_of(n * c, c), c)
        o = of_scr[rows, :] + ob_scr[rows, :]
        y = o * lax.rsqrt(jnp.mean(o * o, axis=-1, keepdims=True) + EPS)
        o_ref[rows, :] = (_silu(g_ref[rows, :]) * y).astype(BF16)
        return carry

    lax.fori_loop(0, n_chunks, readout, 0)


def _retention(proj, ret_decay_raw, dims):
    batch, n_ctx, seq = dims
    tb = n_ctx + seq
    slab = pltpu.VMEM((2, tb, LANES), F32)
    return pl.pallas_call(
        functools.partial(_ret_kernel, ctx_chunks=n_ctx // CHUNK, n_chunks=tb // CHUNK),
        out_shape=jax.ShapeDtypeStruct((RET_HEADS, batch * tb, LANES), BF16),
        grid=(batch, RET_HEADS),
        in_specs=[pl.BlockSpec(memory_space=pltpu.SMEM), pl.BlockSpec(memory_space=pl.ANY)],
        out_specs=_head_spec(tb, lambda b, h: (h, b, 0)),
        scratch_shapes=[slab, slab, slab, slab, pltpu.SemaphoreType.DMA((4, 2)),
                        pltpu.VMEM((tb, LANES), F32), pltpu.VMEM((tb, LANES), F32),
                        pltpu.VMEM((2, LANES, LANES), F32),
                        pltpu.VMEM((2, CHUNK, CHUNK), F32), pltpu.VMEM((2, CHUNK, LANES), F32),
                        pltpu.VMEM((2, CHUNK, LANES), F32)],
        compiler_params=_cparams(2),
    )(ret_decay_raw, proj)


def _att_kernel(sink_ref, q0_ref, q1_ref, k_ref, v_ref, o_ref, kb_scr, vb_scr, bias_scr, *, n_ctx, seq):
    hk = pl.program_id(1)
    c = CHUNK
    win = 3 * c
    kb_scr[...] = k_ref[...].astype(BF16)
    vb_scr[...] = v_ref[...].astype(BF16)
    kc = kb_scr[0:n_ctx, :]
    vc = vb_scr[0:n_ctx, :]
    ii = lax.broadcasted_iota(jnp.int32, (c, win), 0)
    jj = lax.broadcasted_iota(jnp.int32, (c, win), 1)
    for off in range(3):
        bias_scr[off] = jnp.where(jnp.abs(ii - jj + off * c) <= WINDOW, 0.0, NEG_INF)

    def softmax_out(s_ctx, sink, s_loc=None, v_loc=None):
        m = jnp.maximum(jnp.max(s_ctx, axis=-1, keepdims=True), sink)
        if s_loc is not None:
            m = jnp.maximum(m, jnp.max(s_loc, axis=-1, keepdims=True))
        p_ctx = jnp.exp2(s_ctx - m)
        den = jnp.sum(p_ctx, axis=-1, keepdims=True) + jnp.exp2(sink - m)
        o = _dot(p_ctx.astype(BF16), vc)
        if s_loc is not None:
            p_loc = jnp.exp2(s_loc - m)
            den = den + jnp.sum(p_loc, axis=-1, keepdims=True)
            o = o + _dot(p_loc.astype(BF16), v_loc)
        return o / den

    heads = []
    for g, q_ref in enumerate((q0_ref, q1_ref)):
        sink = jnp.full((c, 1), sink_ref[hk * ATT_GROUP + g], F32) * LOG2_E
        heads.append((q_ref, sink, g))

    for q_ref, sink, g in heads:
        for cc in range(n_ctx // c):
            qb = q_ref[cc * c:(cc + 1) * c, :].astype(BF16)
            o_ref[g, cc * c:(cc + 1) * c, :] = softmax_out(_dot_nt(qb, kc), sink).astype(BF16)

    def block(n, carry):
        q_rows = pl.ds(pl.multiple_of(n_ctx + n * c, c), c)
        start = jnp.clip(n * c - c, 0, seq - win)
        k_rows = pl.ds(pl.multiple_of(n_ctx + start, c), win)
        k_loc = kb_scr[k_rows, :]
        v_loc = vb_scr[k_rows, :]
        bias = bias_scr[(n * c - start) // c]
        for q_ref, sink, g in heads:
            qb = q_ref[q_rows, :].astype(BF16)
            o = softmax_out(_dot_nt(qb, kc), sink, _dot_nt(qb, k_loc) + bias, v_loc)
            o_ref[g, q_rows, :] = o.astype(BF16)
        return carry

    lax.fori_loop(0, seq // c, block, 0, unroll=2)


def _attention(proj, att_sink, dims, col0):
    batch, n_ctx, seq = dims
    tb = n_ctx + seq
    qb, kb, vb = col0, col0 + ATT_Q_HEADS, col0 + ATT_Q_HEADS + ATT_KV_HEADS
    return pl.pallas_call(
        functools.partial(_att_kernel, n_ctx=n_ctx, seq=seq),
        out_shape=jax.ShapeDtypeStruct((ATT_Q_HEADS, batch * tb, LANES), BF16),
        grid=(batch, ATT_KV_HEADS),
        in_specs=[pl.BlockSpec(memory_space=pltpu.SMEM),
                  _head_spec(tb, lambda b, h: (qb + ATT_GROUP * h, b, 0)),
                  _head_spec(tb, lambda b, h: (qb + ATT_GROUP * h + 1, b, 0)),
                  _head_spec(tb, lambda b, h: (kb + h, b, 0)),
                  _head_spec(tb, lambda b, h: (vb + h, b, 0))],
        out_specs=pl.BlockSpec((ATT_GROUP, tb, LANES), lambda b, h: (h, b, 0)),
        scratch_shapes=[pltpu.VMEM((tb, LANES), BF16), pltpu.VMEM((tb, LANES), BF16),
                        pltpu.VMEM((3, CHUNK, 3 * CHUNK), F32)],
        compiler_params=_cparams(2),
    )(att_sink, proj, proj, proj, proj)


HG_LEVELS = 7
HG_MXU_LEVELS = 3


def _hgrn_constants():
    c = CHUNK
    i = np.arange(c)[:, None]
    r = np.arange(c)[None, :]
    mats = [r <= i]
    for m in range(1, HG_MXU_LEVELS + 1):
        half = 1 << (m - 1)
        beta = (i // (2 * half)) * (2 * half) + half - 1
        upper = (i % (2 * half)) >= half
        mats.append((upper & (r > beta) & (r <= i)) | ((~upper) & (r > i) & (r <= beta)))
    fwd = [m.astype(np.float32) for m in mats]
    bwd = [m[::-1, ::-1] for m in fwd]
    stack = np.stack([np.concatenate(fwd, 0), np.concatenate(bwd, 0)])
    x = i ^ r
    level = np.where(r > i, -1, np.where(r == i, 0, np.floor(np.log2(np.maximum(x, 1))) + 1)).astype(np.int32)
    levels = np.stack([level, level[::-1, ::-1]])
    return stack, levels


def _hgrn_kernel(proj_hbm, lbl_ref, nw_ref, m_ref, lv_ref, o_ref, qbuf, zfbuf, zbbuf, vbuf, gbuf, sems,
                 of_scr, ob_scr, st_scr, *, layer, ctx_chunks, n_chunks):
    c = CHUNK
    q_ref, zf_ref, zb_ref, v_ref, g_ref = _head_slabs(
        [proj_hbm] * 5, [j * HG_HEADS for j in range(5)], [qbuf, zfbuf, zbbuf, vbuf, gbuf], sems, HG_HEADS)
    z = lbl_ref[...]
    e = jnp.exp(z - jnp.max(z, axis=0, keepdims=True))
    sm = e / jnp.sum(e, axis=0, keepdims=True)
    lb = jnp.sum(sm[1:layer + 1], axis=0, keepdims=True)
    row = lax.broadcasted_iota(jnp.int32, (c, LANES), 0)
    upper = [[((row if d == 0 else c - 1 - row) & (1 << (m - 1))) != 0 for m in range(1, HG_MXU_LEVELS + 1)]
             for d in range(2)]

    def level_operand(q, k, a, half, d):
        pieces = []
        for base in range(0, c, 2 * half):
            lo, hi = slice(base, base + half), slice(base + half, base + 2 * half)
            if d == 0:
                ref = a[base + half - 1:base + half]
                pieces += [k[lo] * jnp.exp2(ref - a[lo]), q[hi] * jnp.exp2(a[hi] - ref)]
            else:
                ref = a[base + half:base + half + 1]
                pieces += [q[lo] * jnp.exp2(a[lo] - ref), k[hi] * jnp.exp2(ref - a[hi])]
        return jnp.concatenate(pieces, axis=0).astype(BF16)

    def chunk(n, d, z_ref):
        rows = pl.ds(pl.multiple_of(n * c, c), c)
        q = q_ref[rows, :]
        v = v_ref[rows, :]
        f = lb + (1.0 - lb) * jax.nn.sigmoid(z_ref[rows, :])
        g = jnp.log2(f)
        k = 1.0 - f
        g_hi = g.astype(BF16)
        g_lo = (g - g_hi.astype(F32)).astype(BF16)
        res = _dot(m_ref[d], jnp.concatenate([g_hi, g_lo], axis=1))
        ex = res[:, :c] + res[:, c:]
        a = ex[0:c]
        st = st_scr[d]
        o = _dot_nt((q * jnp.exp2(a)).astype(BF16), st.astype(BF16))
        lv = lv_ref[d]
        sc = jnp.where(lv == 0, jnp.sum(q * k, axis=-1, keepdims=True), 0.0)
        for m in range(1, HG_LEVELS + 1):
            if m <= HG_MXU_LEVELS:
                zz = (jnp.where(upper[d][m - 1], q, k) * jnp.exp2(ex[m * c:(m + 1) * c])).astype(BF16)
            else:
                zz = level_operand(q, k, a, 1 << (m - 1), d)
            sc = jnp.where(lv == m, _dot_nt(zz, zz), sc)
        vb = v.astype(BF16)
        o = o + _dot(sc.astype(BF16), vb)
        a_last = a[c - 1:c] if d == 0 else a[0:1]
        kd = (k * jnp.exp2(a_last - a)).astype(BF16)
        st_scr[d] = st * jnp.exp2(a_last) + _dot_tn(vb, kd)
        return rows, o

    st_scr[...] = jnp.zeros_like(st_scr)

    def scan(t, carry):
        rows, o = chunk(t, 0, zf_ref)
        of_scr[rows, :] = o
        rows, o = chunk(_backward_chunk(t, ctx_chunks, n_chunks), 1, zb_ref)
        ob_scr[rows, :] = o
        return carry

    lax.fori_loop(0, n_chunks, scan, 0, unroll=2)

    def readout(n, carry):
        rows = pl.ds(pl.multiple_of(n * c, c), c)
        o = of_scr[rows, :] + ob_scr[rows, :]
        y = o * lax.rsqrt(jnp.mean(o * o, axis=-1, keepdims=True) + EPS) * nw_ref[...]
        o_ref[rows, :] = (y * _silu(g_ref[rows, :])).astype(BF16)
        return carry

    lax.fori_loop(0, n_chunks, readout, 0)


def _hgrn2(proj, hg_lb_logits, hg_norm_w, layer, dims):
    batch, n_ctx, seq = dims
    tb = n_ctx + seq
    depth = hg_lb_logits.shape[0]
    stack, levels = _hgrn_constants()
    slab = pltpu.VMEM((2, tb, LANES), F32)
    n_mats = 1 + HG_MXU_LEVELS
    return pl.pallas_call(
        functools.partial(_hgrn_kernel, layer=layer, ctx_chunks=n_ctx // CHUNK, n_chunks=tb // CHUNK),
        out_shape=jax.ShapeDtypeStruct((HG_HEADS, batch * tb, LANES), BF16),
        grid=(batch, HG_HEADS),
        in_specs=[pl.BlockSpec(memory_space=pl.ANY),
                  pl.BlockSpec((depth, LANES), lambda b, h: (0, h)),
                  pl.BlockSpec((1, LANES), lambda b, h: (0, 0)),
                  pl.BlockSpec((2, n_mats * CHUNK, CHUNK), lambda b, h: (0, 0, 0)),
                  pl.BlockSpec((2, CHUNK, CHUNK), lambda b, h: (0, 0, 0))],
        out_specs=_head_spec(tb, lambda b, h: (h, b, 0)),
        scratch_shapes=[slab, slab, slab, slab, slab, pltpu.SemaphoreType.DMA((5, 2)),
                        pltpu.VMEM((tb, LANES), F32), pltpu.VMEM((tb, LANES), F32),
                        pltpu.VMEM((2, LANES, LANES), F32)],
        compiler_params=_cparams(2),
    )(proj, hg_lb_logits, hg_norm_w.reshape(1, LANES),
      jnp.asarray(stack, BF16), jnp.asarray(levels, jnp.int32))


SUBLANES = 8


def _store_token_tiles(ref, x):
    rows, tiles = x.shape[0], x.shape[1] // LANES
    for s in range(tiles):
        ref[pl.ds(s, rows, stride=tiles), :] = x[:, s * LANES:(s + 1) * LANES]


def _load_token_tiles(ref, tiles):
    rows = ref.shape[0] // tiles
    return jnp.concatenate([ref[pl.ds(s, rows, stride=tiles), :] for s in range(tiles)], axis=1)


def _route(logits, tri, counts):
    lane = lax.broadcasted_iota(jnp.int32, logits.shape, 1).astype(F32)
    big = float(LANES)
    first = lambda hit: jnp.min(jnp.where(hit, lane, big), axis=-1, keepdims=True)
    is_g = lane < MOE_GROUPS
    gl = jnp.where(is_g, logits, NEG_INF)
    gmax = jnp.max(gl, axis=-1, keepdims=True)
    gsum = jnp.sum(jnp.where(is_g, jnp.exp(gl - gmax), 0.0), axis=-1, keepdims=True)
    g_p = 1.0 / gsum
    lo = MOE_GROUPS + first(gl == gmax) * MOE_EPG
    el = jnp.where((lane >= lo) & (lane < lo + MOE_EPG), logits, NEG_INF)
    e1v = jnp.max(el, axis=-1, keepdims=True)
    e1 = first(el == e1v)
    el2 = jnp.where(lane == e1, NEG_INF, el)
    e2v = jnp.max(el2, axis=-1, keepdims=True)
    e2 = first(el2 == e2v)
    t = jnp.exp(e2v - e1v)
    w1 = 1.0 / (1.0 + t)
    w2 = t * w1
    id1 = e1 - MOE_GROUPS
    id2 = e2 - MOE_GROUPS
    onehot = jnp.where((lane == id1) | (lane == id2), 1.0, 0.0)
    before = counts + _dot(tri, onehot.astype(BF16))
    rank1 = jnp.sum(jnp.where(lane == id1, before, 0.0), axis=-1, keepdims=True)
    rank2 = jnp.sum(jnp.where(lane == id2, before, 0.0), axis=-1, keepdims=True)
    slab = jnp.zeros_like(logits)
    for col, val in enumerate((id1, id2, g_p * w1, g_p * w2, rank1, rank2)):
        slab = jnp.where(lane == col, val, slab)
    return slab, counts + jnp.sum(onehot, axis=0, keepdims=True)


def _out_route_kernel(*refs, n_parts):
    a_refs = refs[:n_parts]
    (w_ref, x_ref, gate_ref, nw_ref, shift_ref, scale_ref, wr_ref, br_ref, tri_ref,
     xo_ref, h2_ref, route_ref, cnt_ref) = refs[n_parts:]

    @pl.when(pl.program_id(0) == 0)
    def _():
        cnt_ref[...] = jnp.zeros_like(cnt_ref)

    k0 = 0
    y = None
    for a_ref in a_refs:
        a = jnp.concatenate([a_ref[h] for h in range(a_ref.shape[0])], axis=1)
        kk = a.shape[1]
        t = _dot(a, w_ref[k0:k0 + kk, :])
        y = t if y is None else y + t
        k0 += kk
    x = x_ref[...] + gate_ref[0] * y
    xo_ref[...] = x
    h2 = _norm_mod(x, nw_ref[...], scale_ref[0], shift_ref[0])
    _store_token_tiles(h2_ref, h2)
    h_hi = h2.astype(BF16)
    h_lo = (h2 - h_hi.astype(F32)).astype(BF16)
    both = _dot(h_hi, wr_ref[...])
    logits = both[:, :LANES] + both[:, LANES:] + _dot(h_lo, wr_ref[:, :LANES]) + br_ref[...]
    slab, counts = _route(logits, tri_ref[...], cnt_ref[0:1, :])
    route_ref[...] = slab
    cnt_ref[...] = jnp.broadcast_to(counts, cnt_ref.shape)


def _out_projection_route(parts, w_bf, xs, nw, mods, wr, br, dims):
    nt, d = xs.shape
    batch, n_ctx, seq = dims
    tpb = (n_ctx + seq) // ROW_TILE
    ctx_tiles = n_ctx // ROW_TILE
    row = lambda width: pl.BlockSpec((ROW_TILE, width), lambda i: (i, 0))
    full = lambda shape: pl.BlockSpec(shape, lambda i: (0,) * len(shape))
    mod = lambda chunk_idx: _mod_spec(chunk_idx, d, tpb, ctx_tiles, batch)
    tri = jnp.asarray(np.tril(np.ones((ROW_TILE, ROW_TILE), np.float32), -1), BF16)
    return pl.pallas_call(
        functools.partial(_out_route_kernel, n_parts=len(parts)),
        out_shape=(jax.ShapeDtypeStruct((nt, d), F32), jax.ShapeDtypeStruct((nt * d // LANES, LANES), F32),
                   jax.ShapeDtypeStruct((nt, LANES), F32), jax.ShapeDtypeStruct((8, LANES), F32)),
        grid=(nt // ROW_TILE,),
        in_specs=[pl.BlockSpec((p.shape[0], ROW_TILE, LANES), lambda i: (0, i, 0)) for p in parts] + [
            full(w_bf.shape), row(d), mod(2), full((1, d)), mod(3), mod(4), full(wr.shape), full(br.shape),
            full(tri.shape)],
        out_specs=(row(d), pl.BlockSpec((ROW_TILE * d // LANES, LANES), lambda i: (i, 0)), row(LANES),
                   full((8, LANES))),
        compiler_params=_cparams(1),
    )(*parts, w_bf, xs, mods, nw.reshape(1, d), mods, mods, wr, br, tri)


MOE_BUFFERS = 4


def _dispatch_plan(route, counts_f, n_tok, tiles):
    n_assign = n_tok * MOE_TOP_K
    flat_e = route[:, 0:MOE_TOP_K].astype(jnp.int32).reshape(-1)
    rank = route[:, 4:4 + MOE_TOP_K].astype(jnp.int32).reshape(-1)
    counts = counts_f[0, :MOE_EXPERTS].astype(jnp.int32)
    padded = (counts + MOE_BLOCK - 1) // MOE_BLOCK * MOE_BLOCK
    pad_end = jnp.cumsum(padded)
    pad_start = pad_end - padded
    dest = pad_start[flat_e] + rank
    n_blocks = -(-n_assign // MOE_BLOCK) + MOE_EXPERTS
    n_slots = n_blocks * MOE_BLOCK
    block_start = jnp.arange(n_blocks, dtype=jnp.int32) * MOE_BLOCK
    block_expert = jnp.minimum(jnp.sum((pad_end[None, :] <= block_start[:, None]).astype(jnp.int32), axis=1),
                               MOE_EXPERTS - 1)
    slot_assign = jnp.full((n_slots,), -1, jnp.int32).at[dest].set(
        jnp.arange(n_assign, dtype=jnp.int32), unique_indices=True)
    lead, trail = MOE_BLOCK, (MOE_BUFFERS - 1) * MOE_BLOCK
    slot = jnp.arange(-lead, n_slots + trail, dtype=jnp.int32)
    spare = n_assign + ((slot // MOE_BLOCK) % MOE_BUFFERS) * MOE_BLOCK + slot % MOE_BLOCK
    assign = jnp.concatenate([jnp.full((lead,), -1, jnp.int32), slot_assign, jnp.full((trail,), -1, jnp.int32)])
    src = jnp.where(assign >= 0, assign // MOE_TOP_K, 0) * tiles
    dst = jnp.where(assign >= 0, (assign % MOE_TOP_K) * n_tok + assign // MOE_TOP_K, spare) * tiles
    n_used = (pad_end[-1:] // MOE_BLOCK).astype(jnp.int32)
    return block_expert, n_used, src[lead:], dst[:n_slots + lead], n_blocks


def _expert_kernel(be_ref, nused_ref, src_ref, dst_ref, h_hbm, w1_ref, w3_ref, w2_ref, out_hbm,
                   xbuf, ybuf, w1b, w3b, w2b, gsem, ssem):
    i = pl.program_id(0)
    n_used = nused_ref[0]
    nb = MOE_BUFFERS
    cur = i % nb
    buffer_of = lambda blk: (blk + nb) % nb

    tiles = xbuf.shape[1] // MOE_BLOCK

    def token(idx):
        return pl.ds(pl.multiple_of(idx, tiles), tiles)

    def start_gather(blk, buf):
        for r in range(MOE_BLOCK):
            pltpu.make_async_copy(h_hbm.at[token(src_ref[blk * MOE_BLOCK + r])],
                                  xbuf.at[buf, pl.ds(r * tiles, tiles)], gsem.at[buf]).start()

    def start_scatter(blk, buf):
        for r in range(MOE_BLOCK):
            pltpu.make_async_copy(ybuf.at[buf, pl.ds(r * tiles, tiles)],
                                  out_hbm.at[token(dst_ref[(blk + 1) * MOE_BLOCK + r])], ssem.at[buf]).start()

    def wait_gather(buf):
        pltpu.make_async_copy(h_hbm.at[pl.ds(0, MOE_BLOCK * tiles)], xbuf.at[buf], gsem.at[buf]).wait()

    def wait_scatter(buf):
        pltpu.make_async_copy(ybuf.at[buf], out_hbm.at[pl.ds(0, MOE_BLOCK * tiles)], ssem.at[buf]).wait()

    @pl.when(i < n_used)
    def _():
        @pl.when(i == 0)
        def _():
            ybuf[...] = jnp.zeros_like(ybuf)
            block_rows = MOE_BLOCK * tiles
            spare0 = out_hbm.shape[0] - nb * block_rows
            for b in range(nb):
                spare = pltpu.make_async_copy(ybuf.at[b], out_hbm.at[pl.ds(spare0 + b * block_rows, block_rows)],
                                              ssem.at[b])
                spare.start()
                spare.wait()
            for blk in range(nb - 1):
                start_gather(blk, blk)

        @pl.when((i == 0) | (be_ref[i] != be_ref[jnp.maximum(i - 1, 0)]))
        def _():
            w1b[...] = w1_ref[0, 0].astype(BF16)
            w3b[...] = w3_ref[0, 0].astype(BF16)
            w2b[...] = w2_ref[0, 0].astype(BF16)

        wait_gather(cur)

        @pl.when(i >= nb - 1)
        def _():
            wait_scatter(cur)

        xb = _load_token_tiles(xbuf.at[cur], tiles).astype(BF16)
        start_gather(i + nb - 1, buffer_of(i - 1))
        start_scatter(i - 1, buffer_of(i - 1))
        act = _silu(_dot(xb, w1b[...])) * _dot(xb, w3b[...])
        _store_token_tiles(ybuf.at[cur], _dot(act.astype(BF16), w2b[...]))

        @pl.when(i == n_used - 1)
        def _():
            start_scatter(i, cur)
            for back in range(nb):
                @pl.when(i - back >= -1)
                def _():
                    wait_scatter(buffer_of(i - back))
            for ahead in range(1, nb):
                wait_gather(buffer_of(i + ahead))


def _experts(h2, plan, w1, w3, w2, layer):
    block_expert, n_used, slot_src, slot_dst, n_blocks = plan
    d, ff = w1.shape[2], w1.shape[3]
    tiles = d // LANES
    nt = h2.shape[0] // tiles
    wspec = lambda shape: pl.BlockSpec((1, 1) + shape, lambda i, be, nu, sr, ds: (layer, be[i], 0, 0))
    row_buffers = pltpu.VMEM((MOE_BUFFERS, MOE_BLOCK * tiles, LANES), F32)
    grid_spec = pltpu.PrefetchScalarGridSpec(
        num_scalar_prefetch=4,
        grid=(n_blocks,),
        in_specs=[pl.BlockSpec(memory_space=pl.ANY), wspec((d, ff)), wspec((d, ff)), wspec((ff, d))],
        out_specs=pl.BlockSpec(memory_space=pl.ANY),
        scratch_shapes=[row_buffers, row_buffers,
                        pltpu.VMEM((d, ff), BF16), pltpu.VMEM((d, ff), BF16), pltpu.VMEM((ff, d), BF16),
                        pltpu.SemaphoreType.DMA((MOE_BUFFERS,)), pltpu.SemaphoreType.DMA((MOE_BUFFERS,))])
    return pl.pallas_call(
        _expert_kernel,
        out_shape=jax.ShapeDtypeStruct(((nt * MOE_TOP_K + MOE_BUFFERS * MOE_BLOCK) * tiles, LANES), F32),
        grid_spec=grid_spec,
        compiler_params=_cparams(1, has_side_effects=True, disable_bounds_checks=True),
    )(block_expert, n_used, slot_src, slot_dst, h2, w1, w3, w2)


def _combine_kernel(x_ref, y0_ref, y1_ref, route_ref, gate_ref, *rest, final):
    r = route_ref[...]
    tiles = x_ref.shape[1] // LANES
    y = r[:, 2:3] * _load_token_tiles(y0_ref, tiles) + r[:, 3:4] * _load_token_tiles(y1_ref, tiles)
    x = x_ref[...] + gate_ref[0] * y
    if final:
        fnw_ref, o_ref = rest
        x = x * lax.rsqrt(jnp.mean(x * x, axis=-1, keepdims=True) + EPS) * fnw_ref[...]
    else:
        (o_ref,) = rest
    o_ref[...] = x


def _combine(xs, y2, route, mods, dims, final_norm_w=None):
    nt, d = xs.shape
    batch, n_ctx, seq = dims
    tpb = (n_ctx + seq) // ROW_TILE
    ctx_tiles = n_ctx // ROW_TILE
    row = lambda width: pl.BlockSpec((ROW_TILE, width), lambda i: (i, 0))
    final = final_norm_w is not None
    first = pl.BlockSpec((ROW_TILE * d // LANES, LANES), lambda i: (i, 0))
    second = pl.BlockSpec((ROW_TILE * d // LANES, LANES), lambda i: (i + nt // ROW_TILE, 0))
    in_specs = [row(d), first, second, row(LANES), _mod_spec(5, d, tpb, ctx_tiles, batch)]
    args = [xs, y2, y2, route, mods]
    out_rows, out_spec = nt, row(d)
    if final:
        in_specs.append(pl.BlockSpec((1, d), lambda i: (0, 0)))
        args.append(final_norm_w.reshape(1, d))
        lat_tiles = seq // ROW_TILE
        out_rows = batch * seq
        out_spec = pl.BlockSpec(
            (ROW_TILE, d), lambda i: ((i // tpb) * lat_tiles + jnp.maximum(i % tpb - ctx_tiles, 0), 0))
    return pl.pallas_call(
        functools.partial(_combine_kernel, final=final),
        out_shape=jax.ShapeDtypeStruct((out_rows, d), F32),
        grid=(nt // ROW_TILE,),
        in_specs=in_specs,
        out_specs=out_spec,
        compiler_params=_cparams(1),
    )(*args)


def _even_col_ops():
    rh = RET_HEADS
    ops = [(True, 1.0, 1.0)] * rh + [(True, float(HEAD_DIM) ** -0.5, 1.0)] * rh + [(False, 1.0, 1.0)] * (2 * rh)
    ops += [(True, 1.0, float(HEAD_DIM) ** -0.5 * LOG2_E)] * ATT_Q_HEADS + [(True, 1.0, 1.0)] * ATT_KV_HEADS
    ops += [(False, 1.0, 1.0)] * ATT_KV_HEADS
    return tuple(ops)


def kernel(x, c, ctx, c_ctx, ada_w, ada_b, norm_w, final_norm_w, ev_w_in, ev_w_out, ret_decay_raw, att_sink,
           od_w_in, od_w_out, hg_lb_logits, hg_norm_w, moe_wg, moe_bg, moe_we, moe_be, moe_w1, moe_w3, moe_w2):
    batch, seq, d = x.shape
    n_ctx = ctx.shape[1]
    depth = ada_w.shape[0]
    assert batch + 1 <= MOD_ROWS and n_ctx % ROW_TILE == 0 and seq % ROW_TILE == 0
    assert seq % GRID_W == 0 and seq >= 3 * CHUNK
    dims = (batch, n_ctx, seq)
    tb = n_ctx + seq
    nt = batch * tb

    xs = jnp.concatenate([ctx, x], axis=1).reshape(nt, d)
    cvec = jnp.concatenate([c, c_ctx[None, :], jnp.zeros((MOD_ROWS - batch - 1, d), F32)], axis=0)
    mods_all = _ada_modulation(cvec, ada_w, ada_b)
    rope = _rope_tables(n_ctx, seq)
    even_ops = _even_col_ops()

    for l in range(depth):
        p = l // 2
        mods = mods_all[l].reshape(MOD_ROWS, 1, 6 * d)
        if l % 2 == 0:
            proj = _in_projection(xs, norm_w[l, 0], mods, ev_w_in[p].astype(BF16), dims, even_ops, rope)
            ret = _retention(proj, ret_decay_raw[p], dims)
            att = _attention(proj, att_sink[p], dims, 4 * RET_HEADS)
            parts, w_out = [ret, att], ev_w_out[p]
        else:
            proj = _in_projection(xs, norm_w[l, 0], mods, od_w_in[p].astype(BF16), dims)
            parts, w_out = [_hgrn2(proj, hg_lb_logits, hg_norm_w[p], l, dims)], od_w_out[p]
        n_logit = MOE_GROUPS + MOE_EXPERTS
        wr = jnp.concatenate([moe_wg[l], jnp.moveaxis(moe_we[l], 0, 1).reshape(d, MOE_EXPERTS),
                              jnp.zeros((d, LANES - n_logit), F32)], axis=1)
        wr_hi = wr.astype(BF16)
        wr = jnp.concatenate([wr_hi, (wr - wr_hi.astype(F32)).astype(BF16)], axis=1)
        br = jnp.concatenate([moe_bg[l], moe_be[l].reshape(-1), jnp.zeros((LANES - n_logit,), F32)])[None, :]
        xs, h2, route, counts = _out_projection_route(
            parts, w_out.astype(BF16), xs, norm_w[l, 1], mods, wr, br, dims)
        plan = _dispatch_plan(route, counts, nt, d // LANES)
        y2 = _experts(h2, plan, moe_w1, moe_w3, moe_w2, l)
        xs = _combine(xs, y2, route, mods, dims, final_norm_w if l == depth - 1 else None)

    return xs.reshape(batch, seq, d)
```
